```python
import jax
import jax.numpy as jnp
from jax import lax
import numpy as np

D_MODEL = 1024
BATCH = 8
SEQ = 8192
DEPTH = 1

MEM_LEN = 256
SB_HEAD_DIM = 64
SB_WIDTH = D_MODEL // 2
SB_HEADS = SB_WIDTH // SB_HEAD_DIM
Q_BLOCK = 128
LRU_WIDTH = D_MODEL // 2
LRU_BLOCKS = 8
LRU_BLOCK_DIM = LRU_WIDTH // LRU_BLOCKS
CONV_WIDTH = 4
LRU_C = 8.0
X_HEADS = 4
X_WIDTH = D_MODEL // 2
X_HEAD_DIM = X_WIDTH // X_HEADS
N_BRANCH = 3
IN_COLS = 3 * SB_WIDTH + 2 * LRU_WIDTH + X_WIDTH + N_BRANCH * D_MODEL
N_GROUPS = 4
EXPERTS_PER_GROUP = 8
TOP_K = 2
EXPERT_FF = D_MODEL // 4
EPS = 1e-6

kernel_name = 'hybrid_sb_rglru_memx_hmoe_layer'


def rms_norm(x, g):
    xf = x.astype(jnp.float32)
    y = xf * lax.rsqrt(jnp.mean(xf * xf, axis=-1, keepdims=True) + EPS)
    return (y * g.astype(jnp.float32)).astype(x.dtype)


def stick_breaking_attention(q, k, v):
    B, S, H, Dh = q.shape
    n_blk = S // Q_BLOCK
    scale = Dh ** -0.5
    key_pos = jnp.arange(S)
    q_blocks = q.reshape(B, n_blk, Q_BLOCK, H, Dh).transpose(1, 0, 2, 3, 4)

    def one_block(args):
        qb, blk = args
        z = jnp.einsum('bqhd,bkhd->bhqk', qb, k, preferred_element_type=jnp.float32) * scale
        q_pos = blk * Q_BLOCK + jnp.arange(Q_BLOCK)
        causal = key_pos[None, :] < q_pos[:, None]
        sp = jax.nn.softplus(z)
        log_1m = jnp.where(causal, -sp, 0.0)
        rev = lax.cumsum(log_1m, axis=3, reverse=True)
        between = jnp.concatenate([rev[..., 1:], jnp.zeros_like(rev[..., :1])], axis=-1)
        weights = jnp.where(causal, jnp.exp(z - sp + between), 0.0)
        return jnp.einsum('bhqk,bkhd->bqhd', weights.astype(v.dtype), v)

    out = lax.map(one_block, (q_blocks, jnp.arange(n_blk)))
    return out.transpose(1, 0, 2, 3, 4).reshape(B, S, H * Dh)


def _linear_recurrence_combine(left, right):
    a_l, b_l = left
    a_r, b_r = right
    return (a_l * a_r, a_r * b_l + b_r)


def rglru_branch(x_in, y_in, conv_w, conv_b, w_a, b_a, w_i, b_i, lam):
    B, S, C = x_in.shape
    xc = lax.conv_general_dilated(
        x_in, conv_w[:, None, :], window_strides=(1,), padding=[(CONV_WIDTH - 1, 0)],
        dimension_numbers=('NWC', 'WIO', 'NWC'), feature_group_count=C) + conv_b
    xb = xc.reshape(B, S, LRU_BLOCKS, LRU_BLOCK_DIM)
    r = jax.nn.sigmoid(jnp.einsum('bsni,nij->bsnj', xb, w_a).reshape(B, S, C) + b_a)
    i = jax.nn.sigmoid(jnp.einsum('bsni,nij->bsnj', xb, w_i).reshape(B, S, C) + b_i)
    log_a = (-LRU_C * r.astype(jnp.float32)) * jax.nn.softplus(-lam.astype(jnp.float32))
    a = jnp.exp(log_a)
    b = jnp.sqrt(-jnp.expm1(2.0 * log_a)) * (i * xc).astype(jnp.float32)
    _, h = lax.associative_scan(_linear_recurrence_combine, (a, b), axis=1)
    return h.astype(x_in.dtype) * jax.nn.gelu(y_in)


def memory_cross_attention(q_in, mem, g_mem, w_mem_kv, g_q, g_k):
    B, S, _ = q_in.shape
    M = mem.shape[1]
    kv = jnp.einsum('bmd,dc->bmc', rms_norm(mem, g_mem), w_mem_kv)
    k, v = jnp.split(kv, 2, axis=-1)
    q = rms_norm(q_in.reshape(B, S, X_HEADS, X_HEAD_DIM), g_q)
    k = rms_norm(k.reshape(B, M, X_HEADS, X_HEAD_DIM), g_k)
    v = v.reshape(B, M, X_HEADS, X_HEAD_DIM)
    s = jnp.einsum('bshd,bmhd->bhsm', q, k, preferred_element_type=jnp.float32) * (X_HEAD_DIM ** -0.5)
    p = jax.nn.softmax(s, axis=-1)
    o = jnp.einsum('bhsm,bmhd->bshd', p.astype(v.dtype), v)
    return o.reshape(B, S, X_WIDTH)


def hierarchical_moe(h, w_group, b_group, w_expert, b_expert, w_gate, w_up, w_down):
    B, S, D = h.shape
    T = B * S
    t = h.reshape(T, D)
    group_logits = jnp.einsum('td,dg->tg', t, w_group).astype(jnp.float32) + b_group
    group_probs = jax.nn.softmax(group_logits, axis=-1)
    g_prob, g_idx = lax.top_k(group_probs, 1)
    expert_logits = (jnp.einsum('td,de->te', t, w_expert).astype(jnp.float32) + b_expert)
    expert_logits = expert_logits.reshape(T, N_GROUPS, EXPERTS_PER_GROUP)
    in_group = expert_logits[jnp.arange(T), g_idx[:, 0]]
    top_vals, top_idx = lax.top_k(in_group, TOP_K)
    top_w = jax.nn.softmax(top_vals, axis=-1)
    expert_w = jnp.sum(top_w[..., None] * jax.nn.one_hot(top_idx, EXPERTS_PER_GROUP, dtype=jnp.float32), axis=1)
    combine = (jax.nn.one_hot(g_idx[:, 0], N_GROUPS, dtype=jnp.float32)[:, :, None]
               * (g_prob * expert_w)[:, None, :]).astype(h.dtype)
    out = jnp.zeros_like(t)
    for g in range(N_GROUPS):
        gate = jnp.einsum('td,edf->tef', t, w_gate[g])
        up = jnp.einsum('td,edf->tef', t, w_up[g])
        act = jax.nn.silu(gate) * up * combine[:, g, :, None]
        out = out + jnp.einsum('tef,efd->td', act, w_down[g])
    return out.reshape(B, S, D)


def hybrid_layer(x, mem, g_mix, w_in, g_q_sb, g_k_sb, conv_w, conv_b, lru_w_a, lru_b_a,
                 lru_w_i, lru_b_i, lru_lambda, g_mem, w_mem_kv, g_q_x, g_k_x, w_branch, w_out,
                 g_ffn, w_group, b_group, w_expert, b_expert, w_gate, w_up, w_down):
    B, S, D = x.shape
    h = rms_norm(x, g_mix)
    proj = jnp.einsum('bsd,dc->bsc', h, w_in)
    splits = [SB_WIDTH, 2 * SB_WIDTH, 3 * SB_WIDTH, 3 * SB_WIDTH + LRU_WIDTH,
              3 * SB_WIDTH + 2 * LRU_WIDTH, 3 * SB_WIDTH + 2 * LRU_WIDTH + X_WIDTH]
    q_sb, k_sb, v_sb, x_lru, y_lru, q_x, gate_logits = jnp.split(proj, splits, axis=-1)

    q_sb = rms_norm(q_sb.reshape(B, S, SB_HEADS, SB_HEAD_DIM), g_q_sb)
    k_sb = rms_norm(k_sb.reshape(B, S, SB_HEADS, SB_HEAD_DIM), g_k_sb)
    v_sb = v_sb.reshape(B, S, SB_HEADS, SB_HEAD_DIM)
    o_sb = stick_breaking_attention(q_sb, k_sb, v_sb)
    o_lru = rglru_branch(x_lru, y_lru, conv_w, conv_b, lru_w_a, lru_b_a, lru_w_i, lru_b_i, lru_lambda)
    o_x = memory_cross_attention(q_x, mem, g_mem, w_mem_kv, g_q_x, g_k_x)

    branches = jnp.stack([o_sb, o_lru, o_x], axis=2)
    u = jnp.einsum('bsnc,ncd->bsnd', branches, w_branch)
    gates = jax.nn.sigmoid(gate_logits.reshape(B, S, N_BRANCH, D))
    merged = jnp.sum(gates * u, axis=2)
    x = x + jnp.einsum('bsd,de->bse', merged, w_out)

    x = x + hierarchical_moe(rms_norm(x, g_ffn), w_group, b_group, w_expert, b_expert,
                             w_gate, w_up, w_down)
    return x


def setup_inputs(seed: int = 0) -> dict:
    key = jax.random.key(seed)
    ks = jax.random.split(key, 32)
    f32 = jnp.float32

    def nrm(k, shape, scale):
        return jax.random.normal(k, shape, f32) * scale

    def gain(k, n):
        return 1.0 + 0.01 * jax.random.normal(k, (DEPTH, n), f32)

    u = jax.random.uniform(ks[12], (DEPTH, LRU_WIDTH), f32, 0.9, 0.999)
    s = u ** (1.0 / LRU_C)
    lru_lambda = jnp.log(s) - jnp.log1p(-s)

    return {
        'x': nrm(ks[0], (BATCH, SEQ, D_MODEL), 1.0),
        'mem': nrm(ks[1], (BATCH, MEM_LEN, D_MODEL), 1.0),
        'g_mix': gain(ks[2], D_MODEL),
        'w_in': nrm(ks[3], (DEPTH, D_MODEL, IN_COLS), D_MODEL ** -0.5),
        'g_q_sb': gain(ks[4], SB_HEAD_DIM),
        'g_k_sb': gain(ks[5], SB_HEAD_DIM),
        'conv_w': nrm(ks[6], (DEPTH, CONV_WIDTH, LRU_WIDTH), CONV_WIDTH ** -0.5),
        'conv_b': nrm(ks[7], (DEPTH, LRU_WIDTH), 0.01),
        'lru_w_a': nrm(ks[8], (DEPTH, LRU_BLOCKS, LRU_BLOCK_DIM, LRU_BLOCK_DIM), LRU_BLOCK_DIM ** -0.5),
        'lru_b_a': nrm(ks[9], (DEPTH, LRU_WIDTH), 0.01),
        'lru_w_i': nrm(ks[10], (DEPTH, LRU_BLOCKS, LRU_BLOCK_DIM, LRU_BLOCK_DIM), LRU_BLOCK_DIM ** -0.5),
        'lru_b_i': nrm(ks[11], (DEPTH, LRU_WIDTH), 0.01),
        'lru_lambda': lru_lambda,
        'g_mem': gain(ks[13], D_MODEL),
        'w_mem_kv': nrm(ks[14], (DEPTH, D_MODEL, 2 * X_WIDTH), D_MODEL ** -0.5),
        'g_q_x': gain(ks[15], X_HEAD_DIM),
        'g_k_x': gain(ks[16], X_HEAD_DIM),
        'w_branch': nrm(ks[17], (DEPTH, N_BRANCH, SB_WIDTH, D_MODEL), SB_WIDTH ** -0.5),
        'w_out': nrm(ks[18], (DEPTH, D_MODEL, D_MODEL), D_MODEL ** -0.5),
        'g_ffn': gain(ks[19], D_MODEL),
        'w_group': nrm(ks[20], (DEPTH, D_MODEL, N_GROUPS), D_MODEL ** -0.5),
        'b_group': nrm(ks[21], (DEPTH, N_GROUPS), 0.01),
        'w_expert': nrm(ks[22], (DEPTH, D_MODEL, N_GROUPS * EXPERTS_PER_GROUP), D_MODEL ** -0.5),
        'b_expert': nrm(ks[23], (DEPTH, N_GROUPS * EXPERTS_PER_GROUP), 0.01),
        'w_gate': nrm(ks[24], (DEPTH, N_GROUPS, EXPERTS_PER_GROUP, D_MODEL, EXPERT_FF), D_MODEL ** -0.5),
        'w_up': nrm(ks[25], (DEPTH, N_GROUPS, EXPERTS_PER_GROUP, D_MODEL, EXPERT_FF), D_MODEL ** -0.5),
        'w_down': nrm(ks[26], (DEPTH, N_GROUPS, EXPERTS_PER_GROUP, EXPERT_FF, D_MODEL), EXPERT_FF ** -0.5),
    }


def reference(x, mem, g_mix, w_in, g_q_sb, g_k_sb, conv_w, conv_b, lru_w_a, lru_b_a,
              lru_w_i, lru_b_i, lru_lambda, g_mem, w_mem_kv, g_q_x, g_k_x, w_branch, w_out,
              g_ffn, w_group, b_group, w_expert, b_expert, w_gate, w_up, w_down):
    for l in range(DEPTH):
        x = hybrid_layer(x, mem, g_mix[l], w_in[l], g_q_sb[l], g_k_sb[l], conv_w[l], conv_b[l],
                         lru_w_a[l], lru_b_a[l], lru_w_i[l], lru_b_i[l], lru_lambda[l], g_mem[l],
                         w_mem_kv[l], g_q_x[l], g_k_x[l], w_branch[l], w_out[l], g_ffn[l],
                         w_group[l], b_group[l], w_expert[l], b_expert[l], w_gate[l], w_up[l],
                         w_down[l])
    return x
```

```python
import functools

import jax
import jax.numpy as jnp
from jax import lax
from jax.experimental import pallas as pl
from jax.experimental.pallas import tpu as pltpu

F32 = jnp.float32
BF16 = jnp.bfloat16

EPS = 1e-6
SB_HEAD_DIM = 64
SB_WIDTH = 512
LRU_WIDTH = 512
LRU_BLOCKS = 8
CONV_WIDTH = 4
LRU_C = 8.0
X_HEADS = 4
X_WIDTH = 512
X_HEAD_DIM = 128
N_BRANCH = 3
N_GROUPS = 4
EXPERTS_PER_GROUP = 8
N_EXPERTS = N_GROUPS * EXPERTS_PER_GROUP
EXPERT_FF = 256

LANES = 128
HALO = 8
VMEM_LIMIT = 56 * 1024 * 1024

COL_Q, COL_K, COL_V, COL_XL, COL_YL, COL_QX, COL_GATE = 0, 1, 2, 3, 4, 5, 6
PROJ_TILE = 512
KIND_HEAD64, KIND_HEAD128, KIND_PLAIN, KIND_SIGMOID = 0, 1, 2, 3
PROJ_KINDS = (KIND_HEAD64, KIND_HEAD64, KIND_PLAIN, KIND_PLAIN, KIND_PLAIN, KIND_HEAD128) + (KIND_SIGMOID,) * 6


def _params(sem):
    return pltpu.CompilerParams(dimension_semantics=sem, vmem_limit_bytes=VMEM_LIMIT)


def _rms(xf, g):
    return xf * lax.rsqrt(jnp.mean(xf * xf, axis=-1, keepdims=True) + EPS) * g


def _group_mean_matrix(n, group):
    shift = group.bit_length() - 1
    r = lax.shift_right_logical(lax.broadcasted_iota(jnp.int32, (n, n), 0), shift)
    c = lax.shift_right_logical(lax.broadcasted_iota(jnp.int32, (n, n), 1), shift)
    return jnp.where(r == c, 1.0 / group, 0.0).astype(BF16)


def _proj_kernel(x_ref, g_ref, w_ref, cg_ref, o_ref, hn_ref):
    j = pl.program_id(1)

    @pl.when(j == 0)
    def _():
        hn_ref[...] = _rms(x_ref[...], g_ref[...]).astype(BF16)

    acc = jnp.dot(hn_ref[...], w_ref[...], preferred_element_type=F32)

    def head_norm(group):
        ms = jnp.dot((acc * acc).astype(BF16), _group_mean_matrix(PROJ_TILE, group),
                     preferred_element_type=F32)
        return acc * lax.rsqrt(ms + EPS) * cg_ref[...]

    for kind in sorted(set(PROJ_KINDS)):
        cols = [c for c, k in enumerate(PROJ_KINDS) if k == kind]
        cond = functools.reduce(jnp.logical_or, [j == c for c in cols])

        @pl.when(cond)
        def _(kind=kind):
            if kind == KIND_HEAD64:
                y = head_norm(SB_HEAD_DIM)
            elif kind == KIND_HEAD128:
                y = head_norm(X_HEAD_DIM)
            elif kind == KIND_SIGMOID:
                y = jax.nn.sigmoid(acc)
            else:
                y = acc
            o_ref[...] = y.astype(o_ref.dtype)


def _in_proj(x2, g_mix, w_cat, col_gain, tm):
    t, d = x2.shape
    ncol = w_cat.shape[1] // PROJ_TILE
    return pl.pallas_call(
        _proj_kernel,
        grid=(t // tm, ncol),
        in_specs=[
            pl.BlockSpec((tm, d), lambda i, j: (i, 0)),
            pl.BlockSpec((1, d), lambda i, j: (0, 0)),
            pl.BlockSpec((d, PROJ_TILE), lambda i, j: (0, j)),
            pl.BlockSpec((1, PROJ_TILE), lambda i, j: (0, j)),
        ],
        out_specs=pl.BlockSpec((tm, PROJ_TILE), lambda i, j: (i, j)),
        out_shape=jax.ShapeDtypeStruct((t, w_cat.shape[1]), BF16),
        scratch_shapes=[pltpu.VMEM((tm, d), BF16)],
        compiler_params=_params(("parallel", "arbitrary")),
        name="in_proj",
    )(x2, g_mix, w_cat, col_gain)


def _sb_attn_kernel(q_ref, k_ref, v_ref, o_ref, acc_ref, carry_ref, *, tq):
    qi = pl.program_id(2)
    q2 = q_ref[0]
    lane = lax.broadcasted_iota(jnp.int32, (tq, LANES), 1)
    head_lanes = [lane < SB_HEAD_DIM, lane >= SB_HEAD_DIM]
    keep_head = lambda m, a: jnp.where(m, a.astype(F32), 0.0).astype(BF16)
    q_h = [keep_head(m, q2) for m in head_lanes]
    row = lax.broadcasted_iota(jnp.int32, (tq, tq), 0)
    col = lax.broadcasted_iota(jnp.int32, (tq, tq), 1)
    strict_upper = jnp.where(row > col, 1.0, 0.0).astype(BF16)
    causal = col < row

    acc_ref[...] = jnp.zeros_like(acc_ref)
    carry_ref[...] = jnp.zeros_like(carry_ref)

    def block(kb, diag):
        start = pl.multiple_of(kb * tq, tq)
        k2 = k_ref[0, pl.ds(start, tq), :]
        v2 = v_ref[0, pl.ds(start, tq), :]
        for h in range(2):
            z = lax.dot_general(q_h[h], k2, (((1,), (1,)), ((), ())), preferred_element_type=F32)
            l = jnp.log(1.0 + jnp.exp(-jnp.abs(z)))
            sp = jnp.maximum(z, 0.0) + l
            log_beta = jnp.minimum(z, 0.0) - l
            if diag:
                sp = jnp.where(causal, sp, 0.0)
            hi = sp.astype(BF16)
            lo = (sp - hi.astype(F32)).astype(BF16)
            later = (jnp.dot(hi, strict_upper, preferred_element_type=F32)
                     + jnp.dot(lo, strict_upper, preferred_element_type=F32))
            carry = carry_ref[h]
            carry_t = jnp.concatenate([carry] * (tq // LANES), axis=1)
            w = jnp.exp(log_beta - later - carry_t)
            if diag:
                w = jnp.where(causal, w, 0.0)
            v_h = keep_head(head_lanes[h], v2)
            acc_ref[...] += jnp.dot(w.astype(BF16), v_h, preferred_element_type=F32)
            carry_ref[h] = carry + jnp.sum(sp, axis=-1, keepdims=True)

    block(qi, True)

    def body(i, c):
        block(qi - 1 - i, False)
        return c

    lax.fori_loop(0, qi, body, 0)
    o_ref[0] = acc_ref[...].astype(o_ref.dtype)


def _sb_attention(proj3, tq):
    b, s, _ = proj3.shape
    pairs = SB_WIDTH // LANES
    qoff, koff, voff = (COL_Q * PROJ_TILE // LANES, COL_K * PROJ_TILE // LANES, COL_V * PROJ_TILE // LANES)
    return pl.pallas_call(
        functools.partial(_sb_attn_kernel, tq=tq),
        grid=(b, pairs, s // tq),
        in_specs=[
            pl.BlockSpec((1, tq, LANES), lambda bi, p, qi: (bi, qi, qoff + p)),
            pl.BlockSpec((1, s, LANES), lambda bi, p, qi: (bi, 0, koff + p)),
            pl.BlockSpec((1, s, LANES), lambda bi, p, qi: (bi, 0, voff + p)),
        ],
        out_specs=pl.BlockSpec((1, tq, LANES), lambda bi, p, qi: (bi, qi, p)),
        out_shape=jax.ShapeDtypeStruct((b, s, SB_WIDTH), BF16),
        scratch_shapes=[pltpu.VMEM((tq, LANES), F32), pltpu.VMEM((2, tq, LANES), F32)],
        compiler_params=_params(("parallel", "parallel", "arbitrary")),
        name="sb_attention",
    )(proj3, proj3, proj3)


def _lru_kernel(x_ref, y_ref, cw_ref, cb_ref, wa_ref, ba_ref, wi_ref, bi_ref, lam_ref, o_ref,
                xbuf_ref, h_ref, *, ts):
    si = pl.program_id(1)

    @pl.when(si == 0)
    def _():
        xbuf_ref[pl.ds(0, HALO), :] = jnp.zeros((HALO, LRU_WIDTH), F32)
        h_ref[...] = jnp.zeros_like(h_ref)

    @pl.when(si > 0)
    def _():
        xbuf_ref[pl.ds(0, HALO), :] = xbuf_ref[pl.ds(ts, HALO), :]

    xbuf_ref[pl.ds(HALO, ts), :] = x_ref[0].astype(F32)
    xc = cb_ref[...] + jnp.zeros((ts, LRU_WIDTH), F32)
    for j in range(CONV_WIDTH):
        xc = xc + cw_ref[pl.ds(j, 1), :] * xbuf_ref[pl.ds(HALO - (CONV_WIDTH - 1) + j, ts), :]

    xcb = xc.astype(BF16)
    r = jax.nn.sigmoid(jnp.dot(xcb, wa_ref[...], preferred_element_type=F32) + ba_ref[...])
    gi = jax.nn.sigmoid(jnp.dot(xcb, wi_ref[...], preferred_element_type=F32) + bi_ref[...])
    nlam = -lam_ref[...]
    sp_nlam = jnp.maximum(nlam, 0.0) + jnp.log(1.0 + jnp.exp(-jnp.abs(nlam)))
    log_a = (-LRU_C * r) * sp_nlam
    a = jnp.exp(log_a)
    th = jnp.tanh(log_a)
    b = jnp.sqrt(-2.0 * th / (1.0 - th)) * (gi * xc)

    rows = lax.broadcasted_iota(jnp.int32, (ts, LRU_WIDTH), 0)
    d = 1
    while d < ts:
        keep = rows >= d
        b = jnp.where(keep, a * pltpu.roll(b, d, 0), 0.0) + b
        a = jnp.where(keep, a * pltpu.roll(a, d, 0), a)
        d *= 2
    h = b + a * h_ref[pl.ds(0, 1), :]
    h_ref[...] = jnp.broadcast_to(h[ts - 1:ts, :], h_ref.shape)
    o_ref[0] = (h * jax.nn.gelu(y_ref[0].astype(F32))).astype(o_ref.dtype)


def _rglru(proj3, conv_w, conv_b, wa_bd, b_a, wi_bd, b_i, lam, ts):
    b, s, _ = proj3.shape
    vec = lambda: pl.BlockSpec((1, LRU_WIDTH), lambda bi, si: (0, 0))
    mat = lambda: pl.BlockSpec((LRU_WIDTH, LRU_WIDTH), lambda bi, si: (0, 0))
    return pl.pallas_call(
        functools.partial(_lru_kernel, ts=ts),
        grid=(b, s // ts),
        in_specs=[
            pl.BlockSpec((1, ts, LRU_WIDTH), lambda bi, si: (bi, si, COL_XL)),
            pl.BlockSpec((1, ts, LRU_WIDTH), lambda bi, si: (bi, si, COL_YL)),
            pl.BlockSpec((CONV_WIDTH, LRU_WIDTH), lambda bi, si: (0, 0)),
            vec(), mat(), vec(), mat(), vec(), vec(),
        ],
        out_specs=pl.BlockSpec((1, ts, LRU_WIDTH), lambda bi, si: (bi, si, 0)),
        out_shape=jax.ShapeDtypeStruct((b, s, LRU_WIDTH), BF16),
        scratch_shapes=[pltpu.VMEM((ts + HALO, LRU_WIDTH), F32), pltpu.VMEM((HALO, LRU_WIDTH), F32)],
        compiler_params=_params(("parallel", "arbitrary")),
        name="rglru",
    )(proj3, proj3, conv_w, conv_b, wa_bd, b_a, wi_bd, b_i, lam)


def _mem_kv_kernel(m_ref, g_ref, w_ref, gk_ref, k_ref, v_ref):
    mn = _rms(m_ref[0], g_ref[...]).astype(BF16)
    kv = jnp.dot(mn, w_ref[...], preferred_element_type=F32)
    k = kv[:, :X_WIDTH]
    ms = jnp.dot((k * k).astype(BF16), _group_mean_matrix(X_WIDTH, X_HEAD_DIM), preferred_element_type=F32)
    k_ref[0] = (k * lax.rsqrt(ms + EPS) * gk_ref[...]).astype(BF16)
    v_ref[0] = kv[:, X_WIDTH:].astype(BF16)


def _mem_kv(mem, g_mem, w_kv, gk_cols):
    b, m, d = mem.shape
    return pl.pallas_call(
        _mem_kv_kernel,
        grid=(b,),
        in_specs=[
            pl.BlockSpec((1, m, d), lambda bi: (bi, 0, 0)),
            pl.BlockSpec((1, d), lambda bi: (0, 0)),
            pl.BlockSpec((d, 2 * X_WIDTH), lambda bi: (0, 0)),
            pl.BlockSpec((1, X_WIDTH), lambda bi: (0, 0)),
        ],
        out_specs=[pl.BlockSpec((1, m, X_WIDTH), lambda bi: (bi, 0, 0))] * 2,
        out_shape=[jax.ShapeDtypeStruct((b, m, X_WIDTH), BF16)] * 2,
        compiler_params=_params(("parallel",)),
        name="mem_kv",
    )(mem, g_mem, w_kv, gk_cols)


def _split_bf16(v):
    hi = v.astype(BF16)
    return hi, (v - hi.astype(F32)).astype(BF16)


def _router(logits):
    lane = lax.broadcasted_iota(jnp.int32, logits.shape, 1).astype(F32)
    ninf = -jnp.inf
    far = float(LANES)
    is_group = (lane >= N_EXPERTS) & (lane < N_EXPERTS + N_GROUPS)
    gl = jnp.where(is_group, logits, ninf)
    gmax = jnp.max(gl, axis=-1, keepdims=True)
    gidx = jnp.min(jnp.where(gl == gmax, lane, far), axis=-1, keepdims=True) - N_EXPERTS
    g_prob = 1.0 / jnp.sum(jnp.exp(gl - gmax), axis=-1, keepdims=True)
    first = gidx * EXPERTS_PER_GROUP
    el = jnp.where((lane >= first) & (lane < first + EXPERTS_PER_GROUP), logits, ninf)
    m1 = jnp.max(el, axis=-1, keepdims=True)
    i1 = jnp.min(jnp.where(el == m1, lane, far), axis=-1, keepdims=True)
    el2 = jnp.where(lane == i1, ninf, el)
    m2 = jnp.max(el2, axis=-1, keepdims=True)
    i2 = jnp.min(jnp.where(el2 == m2, lane, far), axis=-1, keepdims=True)
    e2 = jnp.exp(m2 - m1)
    w1 = 1.0 / (1.0 + e2)
    w2 = e2 / (1.0 + e2)
    return g_prob * (jnp.where(lane == i1, w1, 0.0) + jnp.where(lane == i2, w2, 0.0))


def _merge_kernel(qx_ref, g0_ref, g1_ref, g2_ref, osb_ref, olru_ref, kx_ref, vx_ref, x_ref,
                  wb_ref, wo_ref, gf_ref, wr_ref, br_ref, x1_ref, hn_ref, comb_ref):
    qx = qx_ref[0]
    heads = []
    for h in range(X_HEADS):
        sl = slice(h * X_HEAD_DIM, (h + 1) * X_HEAD_DIM)
        s = lax.dot_general(qx[:, sl], kx_ref[0][:, sl], (((1,), (1,)), ((), ())),
                            preferred_element_type=F32)
        p = jnp.exp(s - jnp.max(s, axis=-1, keepdims=True))
        p = p / jnp.sum(p, axis=-1, keepdims=True)
        heads.append(jnp.dot(p.astype(BF16), vx_ref[0][:, sl], preferred_element_type=F32))
    o_x = jnp.concatenate(heads, axis=1).astype(BF16)

    merged = g0_ref[0].astype(F32) * jnp.dot(osb_ref[0], wb_ref[0], preferred_element_type=F32)
    merged += g1_ref[0].astype(F32) * jnp.dot(olru_ref[0], wb_ref[1], preferred_element_type=F32)
    merged += g2_ref[0].astype(F32) * jnp.dot(o_x, wb_ref[2], preferred_element_type=F32)
    x1 = x_ref[0] + jnp.dot(merged.astype(BF16), wo_ref[...], preferred_element_type=F32)
    x1_ref[0] = x1

    hn = _rms(x1, gf_ref[...])
    hn_ref[0] = hn.astype(BF16)
    h_hi, h_lo = _split_bf16(hn)
    w_hi, w_lo = _split_bf16(wr_ref[...])
    logits = (jnp.dot(h_hi, w_hi, preferred_element_type=F32)
              + jnp.dot(h_hi, w_lo, preferred_element_type=F32)
              + jnp.dot(h_lo, w_hi, preferred_element_type=F32)) + br_ref[...]
    comb_ref[0] = _router(logits)


def _merge(proj3, o_sb, o_lru, kx, vx, x, w_branch, w_out, g_ffn, w_router, b_router, tm):
    b, s, d = x.shape
    m = kx.shape[1]
    gate_blk = lambda n: pl.BlockSpec((1, tm, d), lambda bi, si, n=n: (bi, si, COL_GATE * PROJ_TILE // d + n))
    tok512 = lambda: pl.BlockSpec((1, tm, SB_WIDTH), lambda bi, si: (bi, si, 0))
    const2 = lambda shape: pl.BlockSpec(shape, lambda bi, si: (0, 0))
    return pl.pallas_call(
        _merge_kernel,
        grid=(b, s // tm),
        in_specs=[
            pl.BlockSpec((1, tm, X_WIDTH), lambda bi, si: (bi, si, COL_QX)),
            gate_blk(0), gate_blk(1), gate_blk(2),
            tok512(), tok512(),
            pl.BlockSpec((1, m, X_WIDTH), lambda bi, si: (bi, 0, 0)),
            pl.BlockSpec((1, m, X_WIDTH), lambda bi, si: (bi, 0, 0)),
            pl.BlockSpec((1, tm, d), lambda bi, si: (bi, si, 0)),
            pl.BlockSpec((N_BRANCH, SB_WIDTH, d), lambda bi, si: (0, 0, 0)),
            const2((d, d)), const2((1, d)), const2((d, LANES)), const2((1, LANES)),
        ],
        out_specs=[
            pl.BlockSpec((1, tm, d), lambda bi, si: (bi, si, 0)),
            pl.BlockSpec((1, tm, d), lambda bi, si: (bi, si, 0)),
            pl.BlockSpec((1, tm, LANES), lambda bi, si: (bi, si, 0)),
        ],
        out_shape=[
            jax.ShapeDtypeStruct((b, s, d), F32),
            jax.ShapeDtypeStruct((b, s, d), BF16),
            jax.ShapeDtypeStruct((b, s, LANES), F32),
        ],
        compiler_params=_params(("parallel", "parallel")),
        name="merge_router",
    )(proj3, proj3, proj3, proj3, o_sb, o_lru, kx, vx, x, w_branch, w_out, g_ffn, w_router, b_router)


def _moe_kernel(hn_ref, comb_ref, x1_ref, wgu_ref, wd_ref, o_ref, acc_ref):
    e = pl.program_id(1)

    @pl.when(e == 0)
    def _():
        acc_ref[...] = jnp.zeros_like(acc_ref)

    gu = jnp.dot(hn_ref[...], wgu_ref[0], preferred_element_type=F32)
    gate, up = gu[:, :EXPERT_FF], gu[:, EXPERT_FF:]
    comb = comb_ref[...]
    lane = lax.broadcasted_iota(jnp.int32, comb.shape, 1)
    c = jnp.sum(jnp.where(lane == e, comb, 0.0), axis=-1, keepdims=True)
    act = jax.nn.silu(gate) * up * c
    acc_ref[...] += jnp.dot(act.astype(BF16), wd_ref[0], preferred_element_type=F32)

    @pl.when(e == pl.num_programs(1) - 1)
    def _():
        o_ref[...] = x1_ref[...] + acc_ref[...]


def _moe(hn2, comb2, x1_2, w_gu, w_down, tm):
    t, d = hn2.shape
    return pl.pallas_call(
        _moe_kernel,
        grid=(t // tm, N_EXPERTS),
        in_specs=[
            pl.BlockSpec((tm, d), lambda i, e: (i, 0)),
            pl.BlockSpec((tm, LANES), lambda i, e: (i, 0)),
            pl.BlockSpec((tm, d), lambda i, e: (i, 0)),
            pl.BlockSpec((1, d, 2 * EXPERT_FF), lambda i, e: (e, 0, 0)),
            pl.BlockSpec((1, EXPERT_FF, d), lambda i, e: (e, 0, 0)),
        ],
        out_specs=pl.BlockSpec((tm, d), lambda i, e: (i, 0)),
        out_shape=jax.ShapeDtypeStruct((t, d), F32),
        scratch_shapes=[pltpu.VMEM((tm, d), F32)],
        compiler_params=_params(("parallel", "arbitrary")),
        name="moe",
    )(hn2, comb2, x1_2, w_gu, w_down)


def _block_diag(w):
    n, bd, _ = w.shape
    eye = jnp.eye(n, dtype=w.dtype)
    return jnp.einsum("nij,nm->nimj", w, eye).reshape(n * bd, n * bd)


def _tile(n, pref):
    while n % pref:
        pref //= 2
    return pref


def _layer(x, mem, g_mix, w_in, g_q_sb, g_k_sb, conv_w, conv_b, lru_w_a, lru_b_a, lru_w_i, lru_b_i,
           lru_lambda, g_mem, w_mem_kv, g_q_x, g_k_x, w_branch, w_out, g_ffn, w_group, b_group,
           w_expert, b_expert, w_gate, w_up, w_down):
    b, s, d = x.shape
    t = b * s
    row = lambda v: v.reshape(1, -1).astype(F32)

    ones = jnp.ones((PROJ_TILE,), F32)
    col_gain = jnp.concatenate([
        jnp.tile(g_q_sb, SB_WIDTH // SB_HEAD_DIM) * SB_HEAD_DIM ** -0.5,
        jnp.tile(g_k_sb, SB_WIDTH // SB_HEAD_DIM),
        ones, ones, ones,
        jnp.tile(g_q_x, X_HEADS) * X_HEAD_DIM ** -0.5,
    ] + [ones] * 6).reshape(1, -1)
    w_router = jnp.zeros((d, LANES), F32).at[:, :N_EXPERTS].set(w_expert)
    w_router = w_router.at[:, N_EXPERTS:N_EXPERTS + N_GROUPS].set(w_group)
    b_router = jnp.zeros((1, LANES), F32).at[0, :N_EXPERTS].set(b_expert)
    b_router = b_router.at[0, N_EXPERTS:N_EXPERTS + N_GROUPS].set(b_group)
    w_gu = jnp.concatenate([w_gate, w_up], axis=-1).reshape(N_EXPERTS, d, 2 * EXPERT_FF).astype(BF16)
    w_dn = w_down.reshape(N_EXPERTS, EXPERT_FF, d).astype(BF16)

    proj = _in_proj(x.reshape(t, d), row(g_mix), w_in.astype(BF16), col_gain, _tile(t, 1024))
    proj3 = proj.reshape(b, s, -1)
    o_sb = _sb_attention(proj3, _tile(s, 256))
    o_lru = _rglru(proj3, conv_w, row(conv_b), _block_diag(lru_w_a).astype(BF16), row(lru_b_a),
                   _block_diag(lru_w_i).astype(BF16), row(lru_b_i), row(lru_lambda), _tile(s, 256))
    kx, vx = _mem_kv(mem, row(g_mem), w_mem_kv.astype(BF16), row(jnp.tile(g_k_x, X_HEADS)))
    x1, hn, comb = _merge(proj3, o_sb, o_lru, kx, vx, x, w_branch.astype(BF16), w_out.astype(BF16),
                          row(g_ffn), w_router, b_router, _tile(s, 512))
    out = _moe(hn.reshape(t, d), comb.reshape(t, LANES), x1.reshape(t, d), w_gu, w_dn, _tile(t, 1024))
    return out.reshape(b, s, d)


def kernel(x, mem, g_mix, w_in, g_q_sb, g_k_sb, conv_w, conv_b, lru_w_a, lru_b_a, lru_w_i, lru_b_i, lru_lambda, g_mem, w_mem_kv, g_q_x, g_k_x, w_branch, w_out, g_ffn, w_group, b_group, w_expert, b_expert, w_gate, w_up, w_down):
    params = (g_mix, w_in, g_q_sb, g_k_sb, conv_w, conv_b, lru_w_a, lru_b_a, lru_w_i, lru_b_i, lru_lambda,
              g_mem, w_mem_kv, g_q_x, g_k_x, w_branch, w_out, g_ffn, w_group, b_group, w_expert, b_expert,
              w_gate, w_up, w_down)
    for layer in range(g_mix.shape[0]):
        x = _layer(x, mem, *[p[layer] for p in params])
    return x
```

```python
import functools

import jax
import jax.numpy as jnp
from jax import lax
from jax.experimental import pallas as pl
from jax.experimental.pallas import tpu as pltpu

F32 = jnp.float32
BF16 = jnp.bfloat16

EPS = 1e-6
SB_HEAD_DIM = 64
SB_WIDTH = 512
LRU_WIDTH = 512
LRU_BLOCKS = 8
CONV_WIDTH = 4
LRU_C = 8.0
X_HEADS = 4
X_WIDTH = 512
X_HEAD_DIM = 128
N_BRANCH = 3
N_GROUPS = 4
EXPERTS_PER_GROUP = 8
N_EXPERTS = N_GROUPS * EXPERTS_PER_GROUP
EXPERT_FF = 256

LOG2E = 1.4426950408889634
SB_GROUP = 4
SB_GROUP_WIDTH = SB_GROUP * SB_HEAD_DIM

LANES = 128
HALO = 8
VMEM_LIMIT = 56 * 1024 * 1024

COL_Q, COL_K, COL_V, COL_XL, COL_YL, COL_QX, COL_GATE = 0, 1, 2, 3, 4, 5, 6
PROJ_TILE = 512
KIND_HEAD64, KIND_HEAD128, KIND_PLAIN, KIND_SIGMOID = 0, 1, 2, 3
PROJ_KINDS = (KIND_HEAD64, KIND_HEAD64, KIND_PLAIN, KIND_PLAIN, KIND_PLAIN, KIND_HEAD128) + (KIND_SIGMOID,) * 6


def _params(sem):
    return pltpu.CompilerParams(dimension_semantics=sem, vmem_limit_bytes=VMEM_LIMIT)


def _rms(xf, g):
    return xf * lax.rsqrt(jnp.mean(xf * xf, axis=-1, keepdims=True) + EPS) * g


def _group_mean_matrix(n, group):
    shift = group.bit_length() - 1
    r = lax.shift_right_logical(lax.broadcasted_iota(jnp.int32, (n, n), 0), shift)
    c = lax.shift_right_logical(lax.broadcasted_iota(jnp.int32, (n, n), 1), shift)
    return jnp.where(r == c, 1.0 / group, 0.0).astype(BF16)


def _proj_kernel(x_ref, g_ref, w_ref, cg_ref, o_ref, hn_ref):
    j = pl.program_id(1)

    @pl.when(j == 0)
    def _():
        hn_ref[...] = _rms(x_ref[...], g_ref[...]).astype(BF16)

    acc = jnp.dot(hn_ref[...], w_ref[...], preferred_element_type=F32)

    def head_norm(group):
        ms = jnp.dot((acc * acc).astype(BF16), _group_mean_matrix(PROJ_TILE, group),
                     preferred_element_type=F32)
        return acc * lax.rsqrt(ms + EPS) * cg_ref[...]

    for kind in sorted(set(PROJ_KINDS)):
        cols = [c for c, k in enumerate(PROJ_KINDS) if k == kind]
        cond = functools.reduce(jnp.logical_or, [j == c for c in cols])

        @pl.when(cond)
        def _(kind=kind):
            if kind == KIND_HEAD64:
                y = head_norm(SB_HEAD_DIM)
            elif kind == KIND_HEAD128:
                y = head_norm(X_HEAD_DIM)
            elif kind == KIND_SIGMOID:
                y = jax.nn.sigmoid(acc)
            else:
                y = acc
            o_ref[...] = y.astype(o_ref.dtype)


def _in_proj(x2, g_mix, w_cat, col_gain, tm):
    t, d = x2.shape
    ncol = w_cat.shape[1] // PROJ_TILE
    return pl.pallas_call(
        _proj_kernel,
        grid=(t // tm, ncol),
        in_specs=[
            pl.BlockSpec((tm, d), lambda i, j: (i, 0)),
            pl.BlockSpec((1, d), lambda i, j: (0, 0)),
            pl.BlockSpec((d, PROJ_TILE), lambda i, j: (0, j)),
            pl.BlockSpec((1, PROJ_TILE), lambda i, j: (0, j)),
        ],
        out_specs=pl.BlockSpec((tm, PROJ_TILE), lambda i, j: (i, j)),
        out_shape=jax.ShapeDtypeStruct((t, w_cat.shape[1]), BF16),
        scratch_shapes=[pltpu.VMEM((tm, d), BF16)],
        compiler_params=_params(("parallel", "arbitrary")),
        name="in_proj",
    )(x2, g_mix, w_cat, col_gain)


def _sb_attn_kernel(q_ref, k_ref, v_ref, o_ref, vexp_ref, acc_ref, carry_ref, cprev_ref, z_ref, d_ref, w_ref,
                    *, tq, nkb):
    qi = pl.program_id(2)
    lane = lax.broadcasted_iota(jnp.int32, (tq, SB_GROUP_WIDTH), 1)
    head_lanes = [(lane >= h * SB_HEAD_DIM) & (lane < (h + 1) * SB_HEAD_DIM) for h in range(SB_GROUP)]
    keep_head = lambda m, a: jnp.where(m, a.astype(F32), 0.0).astype(BF16)

    @pl.when(qi == 0)
    def _():
        def fill(j, c):
            v4 = v_ref[0, pl.ds(pl.multiple_of(j * tq, tq), tq), :]
            for h in range(SB_GROUP):
                vexp_ref[j, pl.ds(h * tq, tq), :] = keep_head(head_lanes[h], v4)
            return c
        lax.fori_loop(0, nkb, fill, 0)

    q4 = q_ref[0]
    q_h = [keep_head(m, q4) for m in head_lanes]
    row = lax.broadcasted_iota(jnp.int32, (tq, tq), 0)
    col = lax.broadcasted_iota(jnp.int32, (tq, tq), 1)
    later_or_self = jnp.where(row >= col, 1.0, 0.0).astype(BF16)
    causal = col < row
    heads = range(SB_GROUP)
    sign_bit = jnp.uint32(0x80000000)

    def keys(kb):
        return k_ref[0, pl.ds(pl.multiple_of(kb * tq, tq), tq), :]

    def scores(h, k4):
        z_ref[h] = lax.dot_general(q_h[h], k4, (((1,), (1,)), ((), ())), preferred_element_type=F32)

    def sums(h, diag):
        z = z_ref[h]
        neg_abs = lax.bitcast_convert_type(lax.bitcast_convert_type(z, jnp.uint32) | sign_bit, F32)
        sp = jnp.maximum(z, 0.0) + jnp.log(1.0 + jnp.exp2(neg_abs)) * LOG2E
        if diag:
            sp = jnp.where(causal, sp, 0.0)
        hi = sp.astype(BF16)
        lo = (sp - hi.astype(F32)).astype(BF16)
        d = z - (jnp.dot(hi, later_or_self, preferred_element_type=F32)
                 + jnp.dot(lo, later_or_self, preferred_element_type=F32))
        d_ref[h] = jnp.where(causal, d, -jnp.inf) if diag else d
        carry = carry_ref[h]
        cprev_ref[h] = carry
        carry_ref[h] = carry + jnp.sum(sp, axis=-1, keepdims=True)

    def weights(h):
        carry_t = jnp.concatenate([cprev_ref[h]] * (tq // LANES), axis=1)
        w_ref[:, pl.ds(h * tq, tq)] = jnp.exp2(d_ref[h] - carry_t).astype(BF16)

    def values(kb):
        acc_ref[...] += jnp.dot(w_ref[...], vexp_ref[kb], preferred_element_type=F32)

    acc_ref[...] = jnp.zeros_like(acc_ref)
    carry_ref[...] = jnp.zeros_like(carry_ref)
    w_ref[...] = jnp.zeros_like(w_ref)
    k_diag = keys(qi)
    for h in heads:
        scores(h, k_diag)
    k_next = keys(jnp.maximum(qi - 1, 0))
    for h in heads:
        sums(h, True)
        scores(h, k_next)

    def body(j, c):
        values(jnp.minimum(qi - j + 2, nkb - 1))
        for h in heads:
            weights(h)
        k_next = keys(jnp.maximum(qi - j - 1, 0))
        for h in heads:
            sums(h, False)
            scores(h, k_next)
        return c

    lax.fori_loop(1, qi + 1, body, 0)
    values(jnp.minimum(1, nkb - 1))
    for h in heads:
        weights(h)
    values(0)
    o_ref[0] = acc_ref[...].astype(o_ref.dtype)


def _sb_attention(proj3, tq):
    b, s, _ = proj3.shape
    w = SB_GROUP_WIDTH
    groups = SB_WIDTH // w
    qoff, koff, voff = (COL_Q * PROJ_TILE // w, COL_K * PROJ_TILE // w, COL_V * PROJ_TILE // w)
    return pl.pallas_call(
        functools.partial(_sb_attn_kernel, tq=tq, nkb=s // tq),
        grid=(b, groups, s // tq),
        in_specs=[
            pl.BlockSpec((1, tq, w), lambda bi, p, qi: (bi, qi, qoff + p)),
            pl.BlockSpec((1, s, w), lambda bi, p, qi: (bi, 0, koff + p)),
            pl.BlockSpec((1, s, w), lambda bi, p, qi: (bi, 0, voff + p)),
        ],
        out_specs=pl.BlockSpec((1, tq, w), lambda bi, p, qi: (bi, qi, p)),
        out_shape=jax.ShapeDtypeStruct((b, s, SB_WIDTH), BF16),
        scratch_shapes=[pltpu.VMEM((s // tq, SB_GROUP * tq, w), BF16),
                        pltpu.VMEM((tq, w), F32), pltpu.VMEM((SB_GROUP, tq, LANES), F32),
                        pltpu.VMEM((SB_GROUP, tq, LANES), F32), pltpu.VMEM((SB_GROUP, tq, tq), F32),
                        pltpu.VMEM((SB_GROUP, tq, tq), F32), pltpu.VMEM((tq, SB_GROUP * tq), BF16)],
        compiler_params=_params(("parallel", "parallel", "arbitrary")),
        name="sb_attention",
    )(proj3, proj3, proj3)


def _lru_kernel(x_ref, y_ref, cw_ref, cb_ref, wa_ref, ba_ref, wi_ref, bi_ref, lam_ref, o_ref,
                xbuf_ref, h_ref, *, ts):
    si = pl.program_id(1)

    @pl.when(si == 0)
    def _():
        xbuf_ref[pl.ds(0, HALO), :] = jnp.zeros((HALO, LRU_WIDTH), F32)
        h_ref[...] = jnp.zeros_like(h_ref)

    @pl.when(si > 0)
    def _():
        xbuf_ref[pl.ds(0, HALO), :] = xbuf_ref[pl.ds(ts, HALO), :]

    xbuf_ref[pl.ds(HALO, ts), :] = x_ref[0].astype(F32)
    xc = cb_ref[...] + jnp.zeros((ts, LRU_WIDTH), F32)
    for j in range(CONV_WIDTH):
        xc = xc + cw_ref[pl.ds(j, 1), :] * xbuf_ref[pl.ds(HALO - (CONV_WIDTH - 1) + j, ts), :]

    xcb = xc.astype(BF16)
    r = jax.nn.sigmoid(jnp.dot(xcb, wa_ref[...], preferred_element_type=F32) + ba_ref[...])
    gi = jax.nn.sigmoid(jnp.dot(xcb, wi_ref[...], preferred_element_type=F32) + bi_ref[...])
    nlam = -lam_ref[...]
    sp_nlam = jnp.maximum(nlam, 0.0) + jnp.log(1.0 + jnp.exp(-jnp.abs(nlam)))
    log_a = (-LRU_C * r) * sp_nlam
    a = jnp.exp(log_a)
    th = jnp.tanh(log_a)
    b = jnp.sqrt(-2.0 * th / (1.0 - th)) * (gi * xc)

    rows = lax.broadcasted_iota(jnp.int32, (ts, LRU_WIDTH), 0)
    d = 1
    while d < ts:
        keep = rows >= d
        b = jnp.where(keep, a * pltpu.roll(b, d, 0), 0.0) + b
        a = jnp.where(keep, a * pltpu.roll(a, d, 0), a)
        d *= 2
    h = b + a * h_ref[pl.ds(0, 1), :]
    h_ref[...] = jnp.broadcast_to(h[ts - 1:ts, :], h_ref.shape)
    o_ref[0] = (h * jax.nn.gelu(y_ref[0].astype(F32))).astype(o_ref.dtype)


def _rglru(proj3, conv_w, conv_b, wa_bd, b_a, wi_bd, b_i, lam, ts):
    b, s, _ = proj3.shape
    vec = lambda: pl.BlockSpec((1, LRU_WIDTH), lambda bi, si: (0, 0))
    mat = lambda: pl.BlockSpec((LRU_WIDTH, LRU_WIDTH), lambda bi, si: (0, 0))
    return pl.pallas_call(
        functools.partial(_lru_kernel, ts=ts),
        grid=(b, s // ts),
        in_specs=[
            pl.BlockSpec((1, ts, LRU_WIDTH), lambda bi, si: (bi, si, COL_XL)),
            pl.BlockSpec((1, ts, LRU_WIDTH), lambda bi, si: (bi, si, COL_YL)),
            pl.BlockSpec((CONV_WIDTH, LRU_WIDTH), lambda bi, si: (0, 0)),
            vec(), mat(), vec(), mat(), vec(), vec(),
        ],
        out_specs=pl.BlockSpec((1, ts, LRU_WIDTH), lambda bi, si: (bi, si, 0)),
        out_shape=jax.ShapeDtypeStruct((b, s, LRU_WIDTH), BF16),
        scratch_shapes=[pltpu.VMEM((ts + HALO, LRU_WIDTH), F32), pltpu.VMEM((HALO, LRU_WIDTH), F32)],
        compiler_params=_params(("parallel", "arbitrary")),
        name="rglru",
    )(proj3, proj3, conv_w, conv_b, wa_bd, b_a, wi_bd, b_i, lam)


def _mem_kv_kernel(m_ref, g_ref, w_ref, gk_ref, k_ref, v_ref):
    mn = _rms(m_ref[0], g_ref[...]).astype(BF16)
    kv = jnp.dot(mn, w_ref[...], preferred_element_type=F32)
    k = kv[:, :X_WIDTH]
    ms = jnp.dot((k * k).astype(BF16), _group_mean_matrix(X_WIDTH, X_HEAD_DIM), preferred_element_type=F32)
    k_ref[0] = (k * lax.rsqrt(ms + EPS) * gk_ref[...]).astype(BF16)
    v_ref[0] = kv[:, X_WIDTH:].astype(BF16)


def _mem_kv(mem, g_mem, w_kv, gk_cols):
    b, m, d = mem.shape
    return pl.pallas_call(
        _mem_kv_kernel,
        grid=(b,),
        in_specs=[
            pl.BlockSpec((1, m, d), lambda bi: (bi, 0, 0)),
            pl.BlockSpec((1, d), lambda bi: (0, 0)),
            pl.BlockSpec((d, 2 * X_WIDTH), lambda bi: (0, 0)),
            pl.BlockSpec((1, X_WIDTH), lambda bi: (0, 0)),
        ],
        out_specs=[pl.BlockSpec((1, m, X_WIDTH), lambda bi: (bi, 0, 0))] * 2,
        out_shape=[jax.ShapeDtypeStruct((b, m, X_WIDTH), BF16)] * 2,
        compiler_params=_params(("parallel",)),
        name="mem_kv",
    )(mem, g_mem, w_kv, gk_cols)


def _split_bf16(v):
    hi = v.astype(BF16)
    return hi, (v - hi.astype(F32)).astype(BF16)


def _router(logits):
    lane = lax.broadcasted_iota(jnp.int32, logits.shape, 1).astype(F32)
    ninf = -jnp.inf
    far = float(LANES)
    is_group = (lane >= N_EXPERTS) & (lane < N_EXPERTS + N_GROUPS)
    gl = jnp.where(is_group, logits, ninf)
    gmax = jnp.max(gl, axis=-1, keepdims=True)
    gidx = jnp.min(jnp.where(gl == gmax, lane, far), axis=-1, keepdims=True) - N_EXPERTS
    g_prob = 1.0 / jnp.sum(jnp.exp(gl - gmax), axis=-1, keepdims=True)
    first = gidx * EXPERTS_PER_GROUP
    el = jnp.where((lane >= first) & (lane < first + EXPERTS_PER_GROUP), logits, ninf)
    m1 = jnp.max(el, axis=-1, keepdims=True)
    i1 = jnp.min(jnp.where(el == m1, lane, far), axis=-1, keepdims=True)
    el2 = jnp.where(lane == i1, ninf, el)
    m2 = jnp.max(el2, axis=-1, keepdims=True)
    i2 = jnp.min(jnp.where(el2 == m2, lane, far), axis=-1, keepdims=True)
    e2 = jnp.exp(m2 - m1)
    w1 = 1.0 / (1.0 + e2)
    w2 = e2 / (1.0 + e2)
    return g_prob * (jnp.where(lane == i1, w1, 0.0) + jnp.where(lane == i2, w2, 0.0))


def _merge_kernel(qx_ref, g0_ref, g1_ref, g2_ref, osb_ref, olru_ref, kx_ref, vx_ref, x_ref,
                  wb_ref, wo_ref, gf_ref, wr_ref, br_ref, x1_ref, hn_ref, comb_ref):
    qx = qx_ref[0]
    heads = []
    for h in range(X_HEADS):
        sl = slice(h * X_HEAD_DIM, (h + 1) * X_HEAD_DIM)
        s = lax.dot_general(qx[:, sl], kx_ref[0][:, sl], (((1,), (1,)), ((), ())),
                            preferred_element_type=F32)
        p = jnp.exp(s - jnp.max(s, axis=-1, keepdims=True))
        p = p / jnp.sum(p, axis=-1, keepdims=True)
        heads.append(jnp.dot(p.astype(BF16), vx_ref[0][:, sl], preferred_element_type=F32))
    o_x = jnp.concatenate(heads, axis=1).astype(BF16)

    merged = g0_ref[0].astype(F32) * jnp.dot(osb_ref[0], wb_ref[0], preferred_element_type=F32)
    merged += g1_ref[0].astype(F32) * jnp.dot(olru_ref[0], wb_ref[1], preferred_element_type=F32)
    merged += g2_ref[0].astype(F32) * jnp.dot(o_x, wb_ref[2], preferred_element_type=F32)
    x1 = x_ref[0] + jnp.dot(merged.astype(BF16), wo_ref[...], preferred_element_type=F32)
    x1_ref[0] = x1

    hn = _rms(x1, gf_ref[...])
    hn_ref[0] = hn.astype(BF16)
    h_hi, h_lo = _split_bf16(hn)
    w_hi, w_lo = _split_bf16(wr_ref[...])
    logits = (jnp.dot(h_hi, w_hi, preferred_element_type=F32)
              + jnp.dot(h_hi, w_lo, preferred_element_type=F32)
              + jnp.dot(h_lo, w_hi, preferred_element_type=F32)) + br_ref[...]
    comb_ref[0] = _router(logits)


def _merge(proj3, o_sb, o_lru, kx, vx, x, w_branch, w_out, g_ffn, w_router, b_router, tm):
    b, s, d = x.shape
    m = kx.shape[1]
    gate_blk = lambda n: pl.BlockSpec((1, tm, d), lambda bi, si, n=n: (bi, si, COL_GATE * PROJ_TILE // d + n))
    tok512 = lambda: pl.BlockSpec((1, tm, SB_WIDTH), lambda bi, si: (bi, si, 0))
    const2 = lambda shape: pl.BlockSpec(shape, lambda bi, si: (0, 0))
    return pl.pallas_call(
        _merge_kernel,
        grid=(b, s // tm),
        in_specs=[
            pl.BlockSpec((1, tm, X_WIDTH), lambda bi, si: (bi, si, COL_QX)),
            gate_blk(0), gate_blk(1), gate_blk(2),
            tok512(), tok512(),
            pl.BlockSpec((1, m, X_WIDTH), lambda bi, si: (bi, 0, 0)),
            pl.BlockSpec((1, m, X_WIDTH), lambda bi, si: (bi, 0, 0)),
            pl.BlockSpec((1, tm, d), lambda bi, si: (bi, si, 0)),
            pl.BlockSpec((N_BRANCH, SB_WIDTH, d), lambda bi, si: (0, 0, 0)),
            const2((d, d)), const2((1, d)), const2((d, LANES)), const2((1, LANES)),
        ],
        out_specs=[
            pl.BlockSpec((1, tm, d), lambda bi, si: (bi, si, 0)),
            pl.BlockSpec((1, tm, d), lambda bi, si: (bi, si, 0)),
            pl.BlockSpec((1, tm, LANES), lambda bi, si: (bi, si, 0)),
        ],
        out_shape=[
            jax.ShapeDtypeStruct((b, s, d), F32),
            jax.ShapeDtypeStruct((b, s, d), BF16),
            jax.ShapeDtypeStruct((b, s, LANES), F32),
        ],
        compiler_params=_params(("parallel", "parallel")),
        name="merge_router",
    )(proj3, proj3, proj3, proj3, o_sb, o_lru, kx, vx, x, w_branch, w_out, g_ffn, w_router, b_router)


def _moe_kernel(hn_ref, comb_ref, x1_ref, wgu_ref, wd_ref, o_ref, acc_ref):
    e = pl.program_id(1)

    @pl.when(e == 0)
    def _():
        acc_ref[...] = jnp.zeros_like(acc_ref)

    gu = jnp.dot(hn_ref[...], wgu_ref[0], preferred_element_type=F32)
    gate, up = gu[:, :EXPERT_FF], gu[:, EXPERT_FF:]
    comb = comb_ref[...]
    lane = lax.broadcasted_iota(jnp.int32, comb.shape, 1)
    c = jnp.sum(jnp.where(lane == e, comb, 0.0), axis=-1, keepdims=True)
    act = jax.nn.silu(gate) * up * c
    acc_ref[...] += jnp.dot(act.astype(BF16), wd_ref[0], preferred_element_type=F32)

    @pl.when(e == pl.num_programs(1) - 1)
    def _():
        o_ref[...] = x1_ref[...] + acc_ref[...]


def _moe(hn2, comb2, x1_2, w_gu, w_down, tm):
    t, d = hn2.shape
    return pl.pallas_call(
        _moe_kernel,
        grid=(t // tm, N_EXPERTS),
        in_specs=[
            pl.BlockSpec((tm, d), lambda i, e: (i, 0)),
            pl.BlockSpec((tm, LANES), lambda i, e: (i, 0)),
            pl.BlockSpec((tm, d), lambda i, e: (i, 0)),
            pl.BlockSpec((1, d, 2 * EXPERT_FF), lambda i, e: (e, 0, 0)),
            pl.BlockSpec((1, EXPERT_FF, d), lambda i, e: (e, 0, 0)),
        ],
        out_specs=pl.BlockSpec((tm, d), lambda i, e: (i, 0)),
        out_shape=jax.ShapeDtypeStruct((t, d), F32),
        scratch_shapes=[pltpu.VMEM((tm, d), F32)],
        compiler_params=_params(("parallel", "arbitrary")),
        name="moe",
    )(hn2, comb2, x1_2, w_gu, w_down)


def _block_diag(w):
    n, bd, _ = w.shape
    eye = jnp.eye(n, dtype=w.dtype)
    return jnp.einsum("nij,nm->nimj", w, eye).reshape(n * bd, n * bd)


def _tile(n, pref):
    while n % pref:
        pref //= 2
    return pref


def _layer(x, mem, g_mix, w_in, g_q_sb, g_k_sb, conv_w, conv_b, lru_w_a, lru_b_a, lru_w_i, lru_b_i,
           lru_lambda, g_mem, w_mem_kv, g_q_x, g_k_x, w_branch, w_out, g_ffn, w_group, b_group,
           w_expert, b_expert, w_gate, w_up, w_down):
    b, s, d = x.shape
    t = b * s
    row = lambda v: v.reshape(1, -1).astype(F32)

    ones = jnp.ones((PROJ_TILE,), F32)
    col_gain = jnp.concatenate([
        jnp.tile(g_q_sb, SB_WIDTH // SB_HEAD_DIM) * (SB_HEAD_DIM ** -0.5 * LOG2E),
        jnp.tile(g_k_sb, SB_WIDTH // SB_HEAD_DIM),
        ones, ones, ones,
        jnp.tile(g_q_x, X_HEADS) * X_HEAD_DIM ** -0.5,
    ] + [ones] * 6).reshape(1, -1)
    w_router = jnp.zeros((d, LANES), F32).at[:, :N_EXPERTS].set(w_expert)
    w_router = w_router.at[:, N_EXPERTS:N_EXPERTS + N_GROUPS].set(w_group)
    b_router = jnp.zeros((1, LANES), F32).at[0, :N_EXPERTS].set(b_expert)
    b_router = b_router.at[0, N_EXPERTS:N_EXPERTS + N_GROUPS].set(b_group)
    w_gu = jnp.concatenate([w_gate, w_up], axis=-1).reshape(N_EXPERTS, d, 2 * EXPERT_FF).astype(BF16)
    w_dn = w_down.reshape(N_EXPERTS, EXPERT_FF, d).astype(BF16)

    proj = _in_proj(x.reshape(t, d), row(g_mix), w_in.astype(BF16), col_gain, _tile(t, 1024))
    proj3 = proj.reshape(b, s, -1)
    o_sb = _sb_attention(proj3, _tile(s, 256))
    o_lru = _rglru(proj3, conv_w, row(conv_b), _block_diag(lru_w_a).astype(BF16), row(lru_b_a),
                   _block_diag(lru_w_i).astype(BF16), row(lru_b_i), row(lru_lambda), _tile(s, 256))
    kx, vx = _mem_kv(mem, row(g_mem), w_mem_kv.astype(BF16), row(jnp.tile(g_k_x, X_HEADS)))
    x1, hn, comb = _merge(proj3, o_sb, o_lru, kx, vx, x, w_branch.astype(BF16), w_out.astype(BF16),
                          row(g_ffn), w_router, b_router, _tile(s, 512))
    out = _moe(hn.reshape(t, d), comb.reshape(t, LANES), x1.reshape(t, d), w_gu, w_dn, _tile(t, 1024))
    return out.reshape(b, s, d)


def kernel(x, mem, g_mix, w_in, g_q_sb, g_k_sb, conv_w, conv_b, lru_w_a, lru_b_a, lru_w_i, lru_b_i, lru_lambda, g_mem, w_mem_kv, g_q_x, g_k_x, w_branch, w_out, g_ffn, w_group, b_group, w_expert, b_expert, w_gate, w_up, w_down):
    params = (g_mix, w_in, g_q_sb, g_k_sb, conv_w, conv_b, lru_w_a, lru_b_a, lru_w_i, lru_b_i, lru_lambda,
              g_mem, w_mem_kv, g_q_x, g_k_x, w_branch, w_out, g_ffn, w_group, b_group, w_expert, b_expert,
              w_gate, w_up, w_down)
    for layer in range(g_mix.shape[0]):
        x = _layer(x, mem, *[p[layer] for p in params])
    return x
```

```python
import functools

import jax
import jax.numpy as jnp
from jax import lax
from jax.experimental import pallas as pl
from jax.experimental.pallas import tpu as pltpu

F32 = jnp.float32
BF16 = jnp.bfloat16

EPS = 1e-6
SB_HEAD_DIM = 64
SB_WIDTH = 512
LRU_WIDTH = 512
LRU_BLOCKS = 8
CONV_WIDTH = 4
LRU_C = 8.0
X_HEADS = 4
X_WIDTH = 512
X_HEAD_DIM = 128
N_BRANCH = 3
N_GROUPS = 4
EXPERTS_PER_GROUP = 8
N_EXPERTS = N_GROUPS * EXPERTS_PER_GROUP
EXPERT_FF = 256

LOG2E = 1.4426950408889634
SB_GROUP = 4
SB_GROUP_WIDTH = SB_GROUP * SB_HEAD_DIM
BF16_ZERO_EXP = 160.0

LANES = 128
HALO = 8
VMEM_LIMIT = 56 * 1024 * 1024

COL_Q, COL_K, COL_V, COL_XL, COL_YL, COL_QX, COL_GATE = 0, 1, 2, 3, 4, 5, 6
PROJ_TILE = 512
KIND_HEAD64, KIND_HEAD128, KIND_PLAIN, KIND_SIGMOID = 0, 1, 2, 3
PROJ_KINDS = (KIND_HEAD64, KIND_HEAD64, KIND_PLAIN, KIND_PLAIN, KIND_PLAIN, KIND_HEAD128) + (KIND_SIGMOID,) * 6


def _params(sem):
    return pltpu.CompilerParams(dimension_semantics=sem, vmem_limit_bytes=VMEM_LIMIT)


def _rms(xf, g):
    return xf * lax.rsqrt(jnp.mean(xf * xf, axis=-1, keepdims=True) + EPS) * g


def _group_mean_matrix(n, group):
    shift = group.bit_length() - 1
    r = lax.shift_right_logical(lax.broadcasted_iota(jnp.int32, (n, n), 0), shift)
    c = lax.shift_right_logical(lax.broadcasted_iota(jnp.int32, (n, n), 1), shift)
    return jnp.where(r == c, 1.0 / group, 0.0).astype(BF16)


def _proj_kernel(x_ref, g_ref, w_ref, cg_ref, o_ref, hn_ref):
    j = pl.program_id(1)

    @pl.when(j == 0)
    def _():
        hn_ref[...] = _rms(x_ref[...], g_ref[...]).astype(BF16)

    acc = jnp.dot(hn_ref[...], w_ref[...], preferred_element_type=F32)

    def head_norm(group):
        ms = jnp.dot((acc * acc).astype(BF16), _group_mean_matrix(PROJ_TILE, group),
                     preferred_element_type=F32)
        return acc * lax.rsqrt(ms + EPS) * cg_ref[...]

    for kind in sorted(set(PROJ_KINDS)):
        cols = [c for c, k in enumerate(PROJ_KINDS) if k == kind]
        cond = functools.reduce(jnp.logical_or, [j == c for c in cols])

        @pl.when(cond)
        def _(kind=kind):
            if kind == KIND_HEAD64:
                y = head_norm(SB_HEAD_DIM)
            elif kind == KIND_HEAD128:
                y = head_norm(X_HEAD_DIM)
            elif kind == KIND_SIGMOID:
                y = jax.nn.sigmoid(acc)
            else:
                y = acc
            o_ref[...] = y.astype(o_ref.dtype)


def _in_proj(x2, g_mix, w_cat, col_gain, tm):
    t, d = x2.shape
    ncol = w_cat.shape[1] // PROJ_TILE
    return pl.pallas_call(
        _proj_kernel,
        grid=(t // tm, ncol),
        in_specs=[
            pl.BlockSpec((tm, d), lambda i, j: (i, 0)),
            pl.BlockSpec((1, d), lambda i, j: (0, 0)),
            pl.BlockSpec((d, PROJ_TILE), lambda i, j: (0, j)),
            pl.BlockSpec((1, PROJ_TILE), lambda i, j: (0, j)),
        ],
        out_specs=pl.BlockSpec((tm, PROJ_TILE), lambda i, j: (i, j)),
        out_shape=jax.ShapeDtypeStruct((t, w_cat.shape[1]), BF16),
        scratch_shapes=[pltpu.VMEM((tm, d), BF16)],
        compiler_params=_params(("parallel", "arbitrary")),
        name="in_proj",
    )(x2, g_mix, w_cat, col_gain)


def _sb_attn_kernel(done_ref, q_ref, k_ref, v_ref, o_ref, vexp_ref, acc_ref, carry_ref, cprev_ref, z_ref, d_ref, w_ref,
                    *, tq, nkb):
    qi = pl.program_id(2)
    lane = lax.broadcasted_iota(jnp.int32, (tq, SB_GROUP_WIDTH), 1)
    head_lanes = [(lane >= h * SB_HEAD_DIM) & (lane < (h + 1) * SB_HEAD_DIM) for h in range(SB_GROUP)]
    keep_head = lambda m, a: jnp.where(m, a.astype(F32), 0.0).astype(BF16)

    @pl.when(qi == 0)
    def _():
        def fill(j, c):
            v4 = v_ref[0, pl.ds(pl.multiple_of(j * tq, tq), tq), :]
            for h in range(SB_GROUP):
                vexp_ref[j, pl.ds(h * tq, tq), :] = keep_head(head_lanes[h], v4)
            return c
        lax.fori_loop(0, nkb, fill, 0)

    q4 = q_ref[0]
    q_h = [keep_head(m, q4) for m in head_lanes]
    row = lax.broadcasted_iota(jnp.int32, (tq, tq), 0)
    col = lax.broadcasted_iota(jnp.int32, (tq, tq), 1)
    later_or_self = jnp.where(row >= col, 1.0, 0.0).astype(BF16)
    causal = col < row
    heads = range(SB_GROUP)
    sign_bit = jnp.uint32(0x80000000)

    def keys(kb):
        return k_ref[0, pl.ds(pl.multiple_of(kb * tq, tq), tq), :]

    def scores(h, k4):
        z_ref[h] = lax.dot_general(q_h[h], k4, (((1,), (1,)), ((), ())), preferred_element_type=F32)

    def sums(h, diag):
        z = z_ref[h]
        neg_abs = lax.bitcast_convert_type(lax.bitcast_convert_type(z, jnp.uint32) | sign_bit, F32)
        sp = jnp.maximum(z, 0.0) + jnp.log(1.0 + jnp.exp2(neg_abs)) * LOG2E
        if diag:
            sp = jnp.where(causal, sp, 0.0)
        hi = sp.astype(BF16)
        lo = (sp - hi.astype(F32)).astype(BF16)
        d = z - (jnp.dot(hi, later_or_self, preferred_element_type=F32)
                 + jnp.dot(lo, later_or_self, preferred_element_type=F32))
        d_ref[h] = jnp.where(causal, d, -jnp.inf) if diag else d
        carry = carry_ref[h]
        cprev_ref[h] = carry
        carry_ref[h] = carry + jnp.sum(sp, axis=-1, keepdims=True)

    def weights(h):
        carry_t = jnp.concatenate([cprev_ref[h]] * (tq // LANES), axis=1)
        w_ref[:, pl.ds(h * tq, tq)] = jnp.exp2(d_ref[h] - carry_t).astype(BF16)

    def values(kb):
        acc_ref[...] += jnp.dot(w_ref[...], vexp_ref[kb], preferred_element_type=F32)

    acc_ref[...] = jnp.zeros_like(acc_ref)
    carry_ref[...] = jnp.zeros_like(carry_ref)
    w_ref[...] = jnp.zeros_like(w_ref)
    k_diag = keys(qi)
    for h in heads:
        scores(h, k_diag)
    k_next = keys(jnp.maximum(qi - 1, 0))
    for h in heads:
        sums(h, True)
        scores(h, k_next)

    def stick_left():
        least = functools.reduce(jnp.minimum, [carry_ref[h] for h in heads])
        return jnp.min(least) < done_ref[0, 0]

    def body(c):
        j, _ = c
        values(jnp.minimum(qi - j + 2, nkb - 1))
        for h in heads:
            weights(h)
        k_next = keys(jnp.maximum(qi - j - 1, 0))
        for h in heads:
            sums(h, False)
            scores(h, k_next)
        return j + 1, stick_left()

    j_end, _ = lax.while_loop(lambda c: (c[0] <= qi) & c[1], body, (jnp.int32(1), stick_left()))
    values(jnp.minimum(qi - j_end + 2, nkb - 1))
    for h in heads:
        weights(h)
    values(qi - j_end + 1)
    o_ref[0] = acc_ref[...].astype(o_ref.dtype)


def _sb_attention(proj3, carry_done, tq):
    b, s, _ = proj3.shape
    w = SB_GROUP_WIDTH
    groups = SB_WIDTH // w
    qoff, koff, voff = (COL_Q * PROJ_TILE // w, COL_K * PROJ_TILE // w, COL_V * PROJ_TILE // w)
    return pl.pallas_call(
        functools.partial(_sb_attn_kernel, tq=tq, nkb=s // tq),
        grid=(b, groups, s // tq),
        in_specs=[
            pl.BlockSpec(memory_space=pltpu.SMEM),
            pl.BlockSpec((1, tq, w), lambda bi, p, qi: (bi, qi, qoff + p)),
            pl.BlockSpec((1, s, w), lambda bi, p, qi: (bi, 0, koff + p)),
            pl.BlockSpec((1, s, w), lambda bi, p, qi: (bi, 0, voff + p)),
        ],
        out_specs=pl.BlockSpec((1, tq, w), lambda bi, p, qi: (bi, qi, p)),
        out_shape=jax.ShapeDtypeStruct((b, s, SB_WIDTH), BF16),
        scratch_shapes=[pltpu.VMEM((s // tq, SB_GROUP * tq, w), BF16),
                        pltpu.VMEM((tq, w), F32), pltpu.VMEM((SB_GROUP, tq, LANES), F32),
                        pltpu.VMEM((SB_GROUP, tq, LANES), F32), pltpu.VMEM((SB_GROUP, tq, tq), F32),
                        pltpu.VMEM((SB_GROUP, tq, tq), F32), pltpu.VMEM((tq, SB_GROUP * tq), BF16)],
        compiler_params=_params(("parallel", "parallel", "arbitrary")),
        name="sb_attention",
    )(carry_done, proj3, proj3, proj3)


def _lru_kernel(x_ref, y_ref, cw_ref, cb_ref, wa_ref, ba_ref, wi_ref, bi_ref, lam_ref, o_ref,
                xbuf_ref, h_ref, *, ts):
    si = pl.program_id(1)

    @pl.when(si == 0)
    def _():
        xbuf_ref[pl.ds(0, HALO), :] = jnp.zeros((HALO, LRU_WIDTH), F32)
        h_ref[...] = jnp.zeros_like(h_ref)

    @pl.when(si > 0)
    def _():
        xbuf_ref[pl.ds(0, HALO), :] = xbuf_ref[pl.ds(ts, HALO), :]

    xbuf_ref[pl.ds(HALO, ts), :] = x_ref[0].astype(F32)
    xc = cb_ref[...] + jnp.zeros((ts, LRU_WIDTH), F32)
    for j in range(CONV_WIDTH):
        xc = xc + cw_ref[pl.ds(j, 1), :] * xbuf_ref[pl.ds(HALO - (CONV_WIDTH - 1) + j, ts), :]

    xcb = xc.astype(BF16)
    r = jax.nn.sigmoid(jnp.dot(xcb, wa_ref[...], preferred_element_type=F32) + ba_ref[...])
    gi = jax.nn.sigmoid(jnp.dot(xcb, wi_ref[...], preferred_element_type=F32) + bi_ref[...])
    nlam = -lam_ref[...]
    sp_nlam = jnp.maximum(nlam, 0.0) + jnp.log(1.0 + jnp.exp(-jnp.abs(nlam)))
    log_a = (-LRU_C * r) * sp_nlam
    a = jnp.exp(log_a)
    th = jnp.tanh(log_a)
    b = jnp.sqrt(-2.0 * th / (1.0 - th)) * (gi * xc)

    rows = lax.broadcasted_iota(jnp.int32, (ts, LRU_WIDTH), 0)
    d = 1
    while d < ts:
        keep = rows >= d
        b = jnp.where(keep, a * pltpu.roll(b, d, 0), 0.0) + b
        a = jnp.where(keep, a * pltpu.roll(a, d, 0), a)
        d *= 2
    h = b + a * h_ref[pl.ds(0, 1), :]
    h_ref[...] = jnp.broadcast_to(h[ts - 1:ts, :], h_ref.shape)
    o_ref[0] = (h * jax.nn.gelu(y_ref[0].astype(F32))).astype(o_ref.dtype)


def _rglru(proj3, conv_w, conv_b, wa_bd, b_a, wi_bd, b_i, lam, ts):
    b, s, _ = proj3.shape
    vec = lambda: pl.BlockSpec((1, LRU_WIDTH), lambda bi, si: (0, 0))
    mat = lambda: pl.BlockSpec((LRU_WIDTH, LRU_WIDTH), lambda bi, si: (0, 0))
    return pl.pallas_call(
        functools.partial(_lru_kernel, ts=ts),
        grid=(b, s // ts),
        in_specs=[
            pl.BlockSpec((1, ts, LRU_WIDTH), lambda bi, si: (bi, si, COL_XL)),
            pl.BlockSpec((1, ts, LRU_WIDTH), lambda bi, si: (bi, si, COL_YL)),
            pl.BlockSpec((CONV_WIDTH, LRU_WIDTH), lambda bi, si: (0, 0)),
            vec(), mat(), vec(), mat(), vec(), vec(),
        ],
        out_specs=pl.BlockSpec((1, ts, LRU_WIDTH), lambda bi, si: (bi, si, 0)),
        out_shape=jax.ShapeDtypeStruct((b, s, LRU_WIDTH), BF16),
        scratch_shapes=[pltpu.VMEM((ts + HALO, LRU_WIDTH), F32), pltpu.VMEM((HALO, LRU_WIDTH), F32)],
        compiler_params=_params(("parallel", "arbitrary")),
        name="rglru",
    )(proj3, proj3, conv_w, conv_b, wa_bd, b_a, wi_bd, b_i, lam)


def _mem_kv_kernel(m_ref, g_ref, w_ref, gk_ref, k_ref, v_ref):
    mn = _rms(m_ref[0], g_ref[...]).astype(BF16)
    kv = jnp.dot(mn, w_ref[...], preferred_element_type=F32)
    k = kv[:, :X_WIDTH]
    ms = jnp.dot((k * k).astype(BF16), _group_mean_matrix(X_WIDTH, X_HEAD_DIM), preferred_element_type=F32)
    k_ref[0] = (k * lax.rsqrt(ms + EPS) * gk_ref[...]).astype(BF16)
    v_ref[0] = kv[:, X_WIDTH:].astype(BF16)


def _mem_kv(mem, g_mem, w_kv, gk_cols):
    b, m, d = mem.shape
    return pl.pallas_call(
        _mem_kv_kernel,
        grid=(b,),
        in_specs=[
            pl.BlockSpec((1, m, d), lambda bi: (bi, 0, 0)),
            pl.BlockSpec((1, d), lambda bi: (0, 0)),
            pl.BlockSpec((d, 2 * X_WIDTH), lambda bi: (0, 0)),
            pl.BlockSpec((1, X_WIDTH), lambda bi: (0, 0)),
        ],
        out_specs=[pl.BlockSpec((1, m, X_WIDTH), lambda bi: (bi, 0, 0))] * 2,
        out_shape=[jax.ShapeDtypeStruct((b, m, X_WIDTH), BF16)] * 2,
        compiler_params=_params(("parallel",)),
        name="mem_kv",
    )(mem, g_mem, w_kv, gk_cols)


def _split_bf16(v):
    hi = v.astype(BF16)
    return hi, (v - hi.astype(F32)).astype(BF16)


def _router(logits):
    lane = lax.broadcasted_iota(jnp.int32, logits.shape, 1).astype(F32)
    ninf = -jnp.inf
    far = float(LANES)
    is_group = (lane >= N_EXPERTS) & (lane < N_EXPERTS + N_GROUPS)
    gl = jnp.where(is_group, logits, ninf)
    gmax = jnp.max(gl, axis=-1, keepdims=True)
    gidx = jnp.min(jnp.where(gl == gmax, lane, far), axis=-1, keepdims=True) - N_EXPERTS
    g_prob = 1.0 / jnp.sum(jnp.exp(gl - gmax), axis=-1, keepdims=True)
    first = gidx * EXPERTS_PER_GROUP
    el = jnp.where((lane >= first) & (lane < first + EXPERTS_PER_GROUP), logits, ninf)
    m1 = jnp.max(el, axis=-1, keepdims=True)
    i1 = jnp.min(jnp.where(el == m1, lane, far), axis=-1, keepdims=True)
    el2 = jnp.where(lane == i1, ninf, el)
    m2 = jnp.max(el2, axis=-1, keepdims=True)
    i2 = jnp.min(jnp.where(el2 == m2, lane, far), axis=-1, keepdims=True)
    e2 = jnp.exp(m2 - m1)
    w1 = 1.0 / (1.0 + e2)
    w2 = e2 / (1.0 + e2)
    return g_prob * (jnp.where(lane == i1, w1, 0.0) + jnp.where(lane == i2, w2, 0.0))


def _merge_kernel(qx_ref, g0_ref, g1_ref, g2_ref, osb_ref, olru_ref, kx_ref, vx_ref, x_ref,
                  wb_ref, wo_ref, gf_ref, wr_ref, br_ref, x1_ref, hn_ref, comb_ref):
    qx = qx_ref[0]
    heads = []
    for h in range(X_HEADS):
        sl = slice(h * X_HEAD_DIM, (h + 1) * X_HEAD_DIM)
        s = lax.dot_general(qx[:, sl], kx_ref[0][:, sl], (((1,), (1,)), ((), ())),
                            preferred_element_type=F32)
        p = jnp.exp(s - jnp.max(s, axis=-1, keepdims=True))
        p = p / jnp.sum(p, axis=-1, keepdims=True)
        heads.append(jnp.dot(p.astype(BF16), vx_ref[0][:, sl], preferred_element_type=F32))
    o_x = jnp.concatenate(heads, axis=1).astype(BF16)

    merged = g0_ref[0].astype(F32) * jnp.dot(osb_ref[0], wb_ref[0], preferred_element_type=F32)
    merged += g1_ref[0].astype(F32) * jnp.dot(olru_ref[0], wb_ref[1], preferred_element_type=F32)
    merged += g2_ref[0].astype(F32) * jnp.dot(o_x, wb_ref[2], preferred_element_type=F32)
    x1 = x_ref[0] + jnp.dot(merged.astype(BF16), wo_ref[...], preferred_element_type=F32)
    x1_ref[0] = x1

    hn = _rms(x1, gf_ref[...])
    hn_ref[0] = hn.astype(BF16)
    h_hi, h_lo = _split_bf16(hn)
    w_hi, w_lo = _split_bf16(wr_ref[...])
    logits = (jnp.dot(h_hi, w_hi, preferred_element_type=F32)
              + jnp.dot(h_hi, w_lo, preferred_element_type=F32)
              + jnp.dot(h_lo, w_hi, preferred_element_type=F32)) + br_ref[...]
    comb_ref[0] = _router(logits)


def _merge(proj3, o_sb, o_lru, kx, vx, x, w_branch, w_out, g_ffn, w_router, b_router, tm):
    b, s, d = x.shape
    m = kx.shape[1]
    gate_blk = lambda n: pl.BlockSpec((1, tm, d), lambda bi, si, n=n: (bi, si, COL_GATE * PROJ_TILE // d + n))
    tok512 = lambda: pl.BlockSpec((1, tm, SB_WIDTH), lambda bi, si: (bi, si, 0))
    const2 = lambda shape: pl.BlockSpec(shape, lambda bi, si: (0, 0))
    return pl.pallas_call(
        _merge_kernel,
        grid=(b, s // tm),
        in_specs=[
            pl.BlockSpec((1, tm, X_WIDTH), lambda bi, si: (bi, si, COL_QX)),
            gate_blk(0), gate_blk(1), gate_blk(2),
            tok512(), tok512(),
            pl.BlockSpec((1, m, X_WIDTH), lambda bi, si: (bi, 0, 0)),
            pl.BlockSpec((1, m, X_WIDTH), lambda bi, si: (bi, 0, 0)),
            pl.BlockSpec((1, tm, d), lambda bi, si: (bi, si, 0)),
            pl.BlockSpec((N_BRANCH, SB_WIDTH, d), lambda bi, si: (0, 0, 0)),
            const2((d, d)), const2((1, d)), const2((d, LANES)), const2((1, LANES)),
        ],
        out_specs=[
            pl.BlockSpec((1, tm, d), lambda bi, si: (bi, si, 0)),
            pl.BlockSpec((1, tm, d), lambda bi, si: (bi, si, 0)),
            pl.BlockSpec((1, tm, LANES), lambda bi, si: (bi, si, 0)),
        ],
        out_shape=[
            jax.ShapeDtypeStruct((b, s, d), F32),
            jax.ShapeDtypeStruct((b, s, d), BF16),
            jax.ShapeDtypeStruct((b, s, LANES), F32),
        ],
        compiler_params=_params(("parallel", "parallel")),
        name="merge_router",
    )(proj3, proj3, proj3, proj3, o_sb, o_lru, kx, vx, x, w_branch, w_out, g_ffn, w_router, b_router)


def _moe_kernel(hn_ref, comb_ref, x1_ref, wgu_ref, wd_ref, o_ref, acc_ref):
    e = pl.program_id(1)

    @pl.when(e == 0)
    def _():
        acc_ref[...] = jnp.zeros_like(acc_ref)

    gu = jnp.dot(hn_ref[...], wgu_ref[0], preferred_element_type=F32)
    gate, up = gu[:, :EXPERT_FF], gu[:, EXPERT_FF:]
    comb = comb_ref[...]
    lane = lax.broadcasted_iota(jnp.int32, comb.shape, 1)
    c = jnp.sum(jnp.where(lane == e, comb, 0.0), axis=-1, keepdims=True)
    act = jax.nn.silu(gate) * up * c
    acc_ref[...] += jnp.dot(act.astype(BF16), wd_ref[0], preferred_element_type=F32)

    @pl.when(e == pl.num_programs(1) - 1)
    def _():
        o_ref[...] = x1_ref[...] + acc_ref[...]


def _moe(hn2, comb2, x1_2, w_gu, w_down, tm):
    t, d = hn2.shape
    return pl.pallas_call(
        _moe_kernel,
        grid=(t // tm, N_EXPERTS),
        in_specs=[
            pl.BlockSpec((tm, d), lambda i, e: (i, 0)),
            pl.BlockSpec((tm, LANES), lambda i, e: (i, 0)),
            pl.BlockSpec((tm, d), lambda i, e: (i, 0)),
            pl.BlockSpec((1, d, 2 * EXPERT_FF), lambda i, e: (e, 0, 0)),
            pl.BlockSpec((1, EXPERT_FF, d), lambda i, e: (e, 0, 0)),
        ],
        out_specs=pl.BlockSpec((tm, d), lambda i, e: (i, 0)),
        out_shape=jax.ShapeDtypeStruct((t, d), F32),
        scratch_shapes=[pltpu.VMEM((tm, d), F32)],
        compiler_params=_params(("parallel", "arbitrary")),
        name="moe",
    )(hn2, comb2, x1_2, w_gu, w_down)


def _block_diag(w):
    n, bd, _ = w.shape
    eye = jnp.eye(n, dtype=w.dtype)
    return jnp.einsum("nij,nm->nimj", w, eye).reshape(n * bd, n * bd)


def _tile(n, pref):
    while n % pref:
        pref //= 2
    return pref


def _layer(x, mem, g_mix, w_in, g_q_sb, g_k_sb, conv_w, conv_b, lru_w_a, lru_b_a, lru_w_i, lru_b_i,
           lru_lambda, g_mem, w_mem_kv, g_q_x, g_k_x, w_branch, w_out, g_ffn, w_group, b_group,
           w_expert, b_expert, w_gate, w_up, w_down):
    b, s, d = x.shape
    t = b * s
    row = lambda v: v.reshape(1, -1).astype(F32)

    ones = jnp.ones((PROJ_TILE,), F32)
    col_gain = jnp.concatenate([
        jnp.tile(g_q_sb, SB_WIDTH // SB_HEAD_DIM) * (SB_HEAD_DIM ** -0.5 * LOG2E),
        jnp.tile(g_k_sb, SB_WIDTH // SB_HEAD_DIM),
        ones, ones, ones,
        jnp.tile(g_q_x, X_HEADS) * X_HEAD_DIM ** -0.5,
    ] + [ones] * 6).reshape(1, -1)
    w_router = jnp.zeros((d, LANES), F32).at[:, :N_EXPERTS].set(w_expert)
    w_router = w_router.at[:, N_EXPERTS:N_EXPERTS + N_GROUPS].set(w_group)
    b_router = jnp.zeros((1, LANES), F32).at[0, :N_EXPERTS].set(b_expert)
    b_router = b_router.at[0, N_EXPERTS:N_EXPERTS + N_GROUPS].set(b_group)
    w_gu = jnp.concatenate([w_gate, w_up], axis=-1).reshape(N_EXPERTS, d, 2 * EXPERT_FF).astype(BF16)
    w_dn = w_down.reshape(N_EXPERTS, EXPERT_FF, d).astype(BF16)

    proj = _in_proj(x.reshape(t, d), row(g_mix), w_in.astype(BF16), col_gain, _tile(t, 1024))
    proj3 = proj.reshape(b, s, -1)
    q_gain = jnp.abs(col_gain[0, COL_Q * PROJ_TILE:(COL_Q + 1) * PROJ_TILE])
    z_max = 1.05 * SB_HEAD_DIM * jnp.max(q_gain) * jnp.max(jnp.abs(g_k_sb))
    carry_done = (z_max + BF16_ZERO_EXP).reshape(1, 1).astype(F32)
    o_sb = _sb_attention(proj3, carry_done, _tile(s, 256))
    o_lru = _rglru(proj3, conv_w, row(conv_b), _block_diag(lru_w_a).astype(BF16), row(lru_b_a),
                   _block_diag(lru_w_i).astype(BF16), row(lru_b_i), row(lru_lambda), _tile(s, 256))
    kx, vx = _mem_kv(mem, row(g_mem), w_mem_kv.astype(BF16), row(jnp.tile(g_k_x, X_HEADS)))
    x1, hn, comb = _merge(proj3, o_sb, o_lru, kx, vx, x, w_branch.astype(BF16), w_out.astype(BF16),
                          row(g_ffn), w_router, b_router, _tile(s, 512))
    out = _moe(hn.reshape(t, d), comb.reshape(t, LANES), x1.reshape(t, d), w_gu, w_dn, _tile(t, 1024))
    return out.reshape(b, s, d)


def kernel(x, mem, g_mix, w_in, g_q_sb, g_k_sb, conv_w, conv_b, lru_w_a, lru_b_a, lru_w_i, lru_b_i, lru_lambda, g_mem, w_mem_kv, g_q_x, g_k_x, w_branch, w_out, g_ffn, w_group, b_group, w_expert, b_expert, w_gate, w_up, w_down):
    params = (g_mix, w_in, g_q_sb, g_k_sb, conv_w, conv_b, lru_w_a, lru_b_a, lru_w_i, lru_b_i, lru_lambda,
              g_mem, w_mem_kv, g_q_x, g_k_x, w_branch, w_out, g_ffn, w_group, b_group, w_expert, b_expert,
              w_gate, w_up, w_down)
    for layer in range(g_mix.shape[0]):
        x = _layer(x, mem, *[p[layer] for p in params])
    return x
```

```python
import functools

import jax
import jax.numpy as jnp
from jax import lax
from jax.experimental import pallas as pl
from jax.experimental.pallas import tpu as pltpu

F32 = jnp.float32
BF16 = jnp.bfloat16

EPS = 1e-6
SB_HEAD_DIM = 64
SB_WIDTH = 512
LRU_WIDTH = 512
LRU_BLOCKS = 8
CONV_WIDTH = 4
LRU_C = 8.0
X_HEADS = 4
X_WIDTH = 512
X_HEAD_DIM = 128
N_BRANCH = 3
N_GROUPS = 4
EXPERTS_PER_GROUP = 8
N_EXPERTS = N_GROUPS * EXPERTS_PER_GROUP
EXPERT_FF = 256

LOG2E = 1.4426950408889634
SB_GROUP = 4
SB_GROUP_WIDTH = SB_GROUP * SB_HEAD_DIM
BF16_ZERO_EXP = 160.0

MOE_CHUNK = 4
MOE_SUB = 256
MOE_PAD_LOG2 = 4
MOE_PAD = 1 << MOE_PAD_LOG2
MOE_GID_LANE = N_EXPERTS
MOE_POS_LANE = N_EXPERTS + 1
SUBLANES = 8

LANES = 128
HALO = 8
VMEM_LIMIT = 56 * 1024 * 1024

COL_Q, COL_K, COL_V, COL_XL, COL_YL, COL_QX, COL_GATE = 0, 1, 2, 3, 4, 5, 6
PROJ_TILE = 512
KIND_HEAD64, KIND_HEAD128, KIND_PLAIN, KIND_SIGMOID = 0, 1, 2, 3
PROJ_KINDS = (KIND_HEAD64, KIND_HEAD64, KIND_PLAIN, KIND_PLAIN, KIND_PLAIN, KIND_HEAD128) + (KIND_SIGMOID,) * 6


def _params(sem):
    return pltpu.CompilerParams(dimension_semantics=sem, vmem_limit_bytes=VMEM_LIMIT)


def _rms(xf, g):
    return xf * lax.rsqrt(jnp.mean(xf * xf, axis=-1, keepdims=True) + EPS) * g


def _group_mean_matrix(n, group):
    shift = group.bit_length() - 1
    r = lax.shift_right_logical(lax.broadcasted_iota(jnp.int32, (n, n), 0), shift)
    c = lax.shift_right_logical(lax.broadcasted_iota(jnp.int32, (n, n), 1), shift)
    return jnp.where(r == c, 1.0 / group, 0.0).astype(BF16)


def _proj_kernel(x_ref, g_ref, w_ref, cg_ref, o_ref, hn_ref):
    j = pl.program_id(1)

    @pl.when(j == 0)
    def _():
        hn_ref[...] = _rms(x_ref[...], g_ref[...]).astype(BF16)

    acc = jnp.dot(hn_ref[...], w_ref[...], preferred_element_type=F32)

    def head_norm(group):
        ms = jnp.dot((acc * acc).astype(BF16), _group_mean_matrix(PROJ_TILE, group),
                     preferred_element_type=F32)
        return acc * lax.rsqrt(ms + EPS) * cg_ref[...]

    for kind in sorted(set(PROJ_KINDS)):
        cols = [c for c, k in enumerate(PROJ_KINDS) if k == kind]
        cond = functools.reduce(jnp.logical_or, [j == c for c in cols])

        @pl.when(cond)
        def _(kind=kind):
            if kind == KIND_HEAD64:
                y = head_norm(SB_HEAD_DIM)
            elif kind == KIND_HEAD128:
                y = head_norm(X_HEAD_DIM)
            elif kind == KIND_SIGMOID:
                y = jax.nn.sigmoid(acc)
            else:
                y = acc
            o_ref[...] = y.astype(o_ref.dtype)


def _in_proj(x2, g_mix, w_cat, col_gain, tm):
    t, d = x2.shape
    ncol = w_cat.shape[1] // PROJ_TILE
    return pl.pallas_call(
        _proj_kernel,
        grid=(t // tm, ncol),
        in_specs=[
            pl.BlockSpec((tm, d), lambda i, j: (i, 0)),
            pl.BlockSpec((1, d), lambda i, j: (0, 0)),
            pl.BlockSpec((d, PROJ_TILE), lambda i, j: (0, j)),
            pl.BlockSpec((1, PROJ_TILE), lambda i, j: (0, j)),
        ],
        out_specs=pl.BlockSpec((tm, PROJ_TILE), lambda i, j: (i, j)),
        out_shape=jax.ShapeDtypeStruct((t, w_cat.shape[1]), BF16),
        scratch_shapes=[pltpu.VMEM((tm, d), BF16)],
        compiler_params=_params(("parallel", "arbitrary")),
        name="in_proj",
    )(x2, g_mix, w_cat, col_gain)


def _sb_attn_kernel(done_ref, q_ref, k_ref, v_ref, o_ref, vexp_ref, acc_ref, carry_ref, cprev_ref, z_ref, d_ref, w_ref,
                    *, tq, nkb):
    qi = pl.program_id(2)
    lane = lax.broadcasted_iota(jnp.int32, (tq, SB_GROUP_WIDTH), 1)
    head_lanes = [(lane >= h * SB_HEAD_DIM) & (lane < (h + 1) * SB_HEAD_DIM) for h in range(SB_GROUP)]
    keep_head = lambda m, a: jnp.where(m, a.astype(F32), 0.0).astype(BF16)

    @pl.when(qi == 0)
    def _():
        def fill(j, c):
            v4 = v_ref[0, pl.ds(pl.multiple_of(j * tq, tq), tq), :]
            for h in range(SB_GROUP):
                vexp_ref[j, pl.ds(h * tq, tq), :] = keep_head(head_lanes[h], v4)
            return c
        lax.fori_loop(0, nkb, fill, 0)

    q4 = q_ref[0]
    q_h = [keep_head(m, q4) for m in head_lanes]
    row = lax.broadcasted_iota(jnp.int32, (tq, tq), 0)
    col = lax.broadcasted_iota(jnp.int32, (tq, tq), 1)
    later_or_self = jnp.where(row >= col, 1.0, 0.0).astype(BF16)
    causal = col < row
    heads = range(SB_GROUP)
    sign_bit = jnp.uint32(0x80000000)

    def keys(kb):
        return k_ref[0, pl.ds(pl.multiple_of(kb * tq, tq), tq), :]

    def scores(h, k4):
        z_ref[h] = lax.dot_general(q_h[h], k4, (((1,), (1,)), ((), ())), preferred_element_type=F32)

    def sums(h, diag):
        z = z_ref[h]
        neg_abs = lax.bitcast_convert_type(lax.bitcast_convert_type(z, jnp.uint32) | sign_bit, F32)
        sp = jnp.maximum(z, 0.0) + jnp.log(1.0 + jnp.exp2(neg_abs)) * LOG2E
        if diag:
            sp = jnp.where(causal, sp, 0.0)
        hi = sp.astype(BF16)
        lo = (sp - hi.astype(F32)).astype(BF16)
        d = z - (jnp.dot(hi, later_or_self, preferred_element_type=F32)
                 + jnp.dot(lo, later_or_self, preferred_element_type=F32))
        d_ref[h] = jnp.where(causal, d, -jnp.inf) if diag else d
        carry = carry_ref[h]
        cprev_ref[h] = carry
        carry_ref[h] = carry + jnp.sum(sp, axis=-1, keepdims=True)

    def weights(h):
        carry_t = jnp.concatenate([cprev_ref[h]] * (tq // LANES), axis=1)
        w_ref[:, pl.ds(h * tq, tq)] = jnp.exp2(d_ref[h] - carry_t).astype(BF16)

    def values(kb):
        acc_ref[...] += jnp.dot(w_ref[...], vexp_ref[kb], preferred_element_type=F32)

    acc_ref[...] = jnp.zeros_like(acc_ref)
    carry_ref[...] = jnp.zeros_like(carry_ref)
    w_ref[...] = jnp.zeros_like(w_ref)
    k_diag = keys(qi)
    for h in heads:
        scores(h, k_diag)
    k_next = keys(jnp.maximum(qi - 1, 0))
    for h in heads:
        sums(h, True)
        scores(h, k_next)

    def stick_left():
        least = functools.reduce(jnp.minimum, [carry_ref[h] for h in heads])
        return jnp.min(least) < done_ref[0, 0]

    def body(c):
        j, _ = c
        values(jnp.minimum(qi - j + 2, nkb - 1))
        for h in heads:
            weights(h)
        k_next = keys(jnp.maximum(qi - j - 1, 0))
        for h in heads:
            sums(h, False)
            scores(h, k_next)
        return j + 1, stick_left()

    j_end, _ = lax.while_loop(lambda c: (c[0] <= qi) & c[1], body, (jnp.int32(1), stick_left()))
    values(jnp.minimum(qi - j_end + 2, nkb - 1))
    for h in heads:
        weights(h)
    values(qi - j_end + 1)
    o_ref[0] = acc_ref[...].astype(o_ref.dtype)


def _sb_attention(proj3, carry_done, tq):
    b, s, _ = proj3.shape
    w = SB_GROUP_WIDTH
    groups = SB_WIDTH // w
    qoff, koff, voff = (COL_Q * PROJ_TILE // w, COL_K * PROJ_TILE // w, COL_V * PROJ_TILE // w)
    return pl.pallas_call(
        functools.partial(_sb_attn_kernel, tq=tq, nkb=s // tq),
        grid=(b, groups, s // tq),
        in_specs=[
            pl.BlockSpec(memory_space=pltpu.SMEM),
            pl.BlockSpec((1, tq, w), lambda bi, p, qi: (bi, qi, qoff + p)),
            pl.BlockSpec((1, s, w), lambda bi, p, qi: (bi, 0, koff + p)),
            pl.BlockSpec((1, s, w), lambda bi, p, qi: (bi, 0, voff + p)),
        ],
        out_specs=pl.BlockSpec((1, tq, w), lambda bi, p, qi: (bi, qi, p)),
        out_shape=jax.ShapeDtypeStruct((b, s, SB_WIDTH), BF16),
        scratch_shapes=[pltpu.VMEM((s // tq, SB_GROUP * tq, w), BF16),
                        pltpu.VMEM((tq, w), F32), pltpu.VMEM((SB_GROUP, tq, LANES), F32),
                        pltpu.VMEM((SB_GROUP, tq, LANES), F32), pltpu.VMEM((SB_GROUP, tq, tq), F32),
                        pltpu.VMEM((SB_GROUP, tq, tq), F32), pltpu.VMEM((tq, SB_GROUP * tq), BF16)],
        compiler_params=_params(("parallel", "parallel", "arbitrary")),
        name="sb_attention",
    )(carry_done, proj3, proj3, proj3)


def _lru_kernel(x_ref, y_ref, cw_ref, cb_ref, wa_ref, ba_ref, wi_ref, bi_ref, lam_ref, o_ref,
                xbuf_ref, h_ref, *, ts):
    si = pl.program_id(1)

    @pl.when(si == 0)
    def _():
        xbuf_ref[pl.ds(0, HALO), :] = jnp.zeros((HALO, LRU_WIDTH), F32)
        h_ref[...] = jnp.zeros_like(h_ref)

    @pl.when(si > 0)
    def _():
        xbuf_ref[pl.ds(0, HALO), :] = xbuf_ref[pl.ds(ts, HALO), :]

    xbuf_ref[pl.ds(HALO, ts), :] = x_ref[0].astype(F32)
    xc = cb_ref[...] + jnp.zeros((ts, LRU_WIDTH), F32)
    for j in range(CONV_WIDTH):
        xc = xc + cw_ref[pl.ds(j, 1), :] * xbuf_ref[pl.ds(HALO - (CONV_WIDTH - 1) + j, ts), :]

    xcb = xc.astype(BF16)
    r = jax.nn.sigmoid(jnp.dot(xcb, wa_ref[...], preferred_element_type=F32) + ba_ref[...])
    gi = jax.nn.sigmoid(jnp.dot(xcb, wi_ref[...], preferred_element_type=F32) + bi_ref[...])
    nlam = -lam_ref[...]
    sp_nlam = jnp.maximum(nlam, 0.0) + jnp.log(1.0 + jnp.exp(-jnp.abs(nlam)))
    log_a = (-LRU_C * r) * sp_nlam
    a = jnp.exp(log_a)
    th = jnp.tanh(log_a)
    b = jnp.sqrt(-2.0 * th / (1.0 - th)) * (gi * xc)

    rows = lax.broadcasted_iota(jnp.int32, (ts, LRU_WIDTH), 0)
    d = 1
    while d < ts:
        keep = rows >= d
        b = jnp.where(keep, a * pltpu.roll(b, d, 0), 0.0) + b
        a = jnp.where(keep, a * pltpu.roll(a, d, 0), a)
        d *= 2
    h = b + a * h_ref[pl.ds(0, 1), :]
    h_ref[...] = jnp.broadcast_to(h[ts - 1:ts, :], h_ref.shape)
    o_ref[0] = (h * jax.nn.gelu(y_ref[0].astype(F32))).astype(o_ref.dtype)


def _rglru(proj3, conv_w, conv_b, wa_bd, b_a, wi_bd, b_i, lam, ts):
    b, s, _ = proj3.shape
    vec = lambda: pl.BlockSpec((1, LRU_WIDTH), lambda bi, si: (0, 0))
    mat = lambda: pl.BlockSpec((LRU_WIDTH, LRU_WIDTH), lambda bi, si: (0, 0))
    return pl.pallas_call(
        functools.partial(_lru_kernel, ts=ts),
        grid=(b, s // ts),
        in_specs=[
            pl.BlockSpec((1, ts, LRU_WIDTH), lambda bi, si: (bi, si, COL_XL)),
            pl.BlockSpec((1, ts, LRU_WIDTH), lambda bi, si: (bi, si, COL_YL)),
            pl.BlockSpec((CONV_WIDTH, LRU_WIDTH), lambda bi, si: (0, 0)),
            vec(), mat(), vec(), mat(), vec(), vec(),
        ],
        out_specs=pl.BlockSpec((1, ts, LRU_WIDTH), lambda bi, si: (bi, si, 0)),
        out_shape=jax.ShapeDtypeStruct((b, s, LRU_WIDTH), BF16),
        scratch_shapes=[pltpu.VMEM((ts + HALO, LRU_WIDTH), F32), pltpu.VMEM((HALO, LRU_WIDTH), F32)],
        compiler_params=_params(("parallel", "arbitrary")),
        name="rglru",
    )(proj3, proj3, conv_w, conv_b, wa_bd, b_a, wi_bd, b_i, lam)


def _mem_kv_kernel(m_ref, g_ref, w_ref, gk_ref, k_ref, v_ref):
    mn = _rms(m_ref[0], g_ref[...]).astype(BF16)
    kv = jnp.dot(mn, w_ref[...], preferred_element_type=F32)
    k = kv[:, :X_WIDTH]
    ms = jnp.dot((k * k).astype(BF16), _group_mean_matrix(X_WIDTH, X_HEAD_DIM), preferred_element_type=F32)
    k_ref[0] = (k * lax.rsqrt(ms + EPS) * gk_ref[...]).astype(BF16)
    v_ref[0] = kv[:, X_WIDTH:].astype(BF16)


def _mem_kv(mem, g_mem, w_kv, gk_cols):
    b, m, d = mem.shape
    return pl.pallas_call(
        _mem_kv_kernel,
        grid=(b,),
        in_specs=[
            pl.BlockSpec((1, m, d), lambda bi: (bi, 0, 0)),
            pl.BlockSpec((1, d), lambda bi: (0, 0)),
            pl.BlockSpec((d, 2 * X_WIDTH), lambda bi: (0, 0)),
            pl.BlockSpec((1, X_WIDTH), lambda bi: (0, 0)),
        ],
        out_specs=[pl.BlockSpec((1, m, X_WIDTH), lambda bi: (bi, 0, 0))] * 2,
        out_shape=[jax.ShapeDtypeStruct((b, m, X_WIDTH), BF16)] * 2,
        compiler_params=_params(("parallel",)),
        name="mem_kv",
    )(mem, g_mem, w_kv, gk_cols)


def _split_bf16(v):
    hi = v.astype(BF16)
    return hi, (v - hi.astype(F32)).astype(BF16)


def _router(logits):
    lane = lax.broadcasted_iota(jnp.int32, logits.shape, 1).astype(F32)
    ninf = -jnp.inf
    far = float(LANES)
    is_group = (lane >= N_EXPERTS) & (lane < N_EXPERTS + N_GROUPS)
    gl = jnp.where(is_group, logits, ninf)
    gmax = jnp.max(gl, axis=-1, keepdims=True)
    gidx = jnp.min(jnp.where(gl == gmax, lane, far), axis=-1, keepdims=True) - N_EXPERTS
    g_prob = 1.0 / jnp.sum(jnp.exp(gl - gmax), axis=-1, keepdims=True)
    first = gidx * EXPERTS_PER_GROUP
    el = jnp.where((lane >= first) & (lane < first + EXPERTS_PER_GROUP), logits, ninf)
    m1 = jnp.max(el, axis=-1, keepdims=True)
    i1 = jnp.min(jnp.where(el == m1, lane, far), axis=-1, keepdims=True)
    el2 = jnp.where(lane == i1, ninf, el)
    m2 = jnp.max(el2, axis=-1, keepdims=True)
    i2 = jnp.min(jnp.where(el2 == m2, lane, far), axis=-1, keepdims=True)
    e2 = jnp.exp(m2 - m1)
    w1 = 1.0 / (1.0 + e2)
    w2 = e2 / (1.0 + e2)
    combine = g_prob * (jnp.where(lane == i1, w1, 0.0) + jnp.where(lane == i2, w2, 0.0))
    return combine + jnp.where(lane == MOE_GID_LANE, gidx, 0.0)


def _merge_kernel(qx_ref, g0_ref, g1_ref, g2_ref, osb_ref, olru_ref, kx_ref, vx_ref, x_ref,
                  wb_ref, wo_ref, gf_ref, wr_ref, br_ref, x1_ref, hn_ref, comb_ref):
    qx = qx_ref[0]
    heads = []
    for h in range(X_HEADS):
        sl = slice(h * X_HEAD_DIM, (h + 1) * X_HEAD_DIM)
        s = lax.dot_general(qx[:, sl], kx_ref[0][:, sl], (((1,), (1,)), ((), ())),
                            preferred_element_type=F32)
        p = jnp.exp(s - jnp.max(s, axis=-1, keepdims=True))
        p = p / jnp.sum(p, axis=-1, keepdims=True)
        heads.append(jnp.dot(p.astype(BF16), vx_ref[0][:, sl], preferred_element_type=F32))
    o_x = jnp.concatenate(heads, axis=1).astype(BF16)

    merged = g0_ref[0].astype(F32) * jnp.dot(osb_ref[0], wb_ref[0], preferred_element_type=F32)
    merged += g1_ref[0].astype(F32) * jnp.dot(olru_ref[0], wb_ref[1], preferred_element_type=F32)
    merged += g2_ref[0].astype(F32) * jnp.dot(o_x, wb_ref[2], preferred_element_type=F32)
    x1 = x_ref[0] + jnp.dot(merged.astype(BF16), wo_ref[...], preferred_element_type=F32)
    x1_ref[0] = x1

    hn = _rms(x1, gf_ref[...])
    hn_ref[0] = hn.astype(BF16)
    h_hi, h_lo = _split_bf16(hn)
    w_hi, w_lo = _split_bf16(wr_ref[...])
    logits = (jnp.dot(h_hi, w_hi, preferred_element_type=F32)
              + jnp.dot(h_hi, w_lo, preferred_element_type=F32)
              + jnp.dot(h_lo, w_hi, preferred_element_type=F32)) + br_ref[...]
    comb_ref[0] = _router(logits)


def _merge(proj3, o_sb, o_lru, kx, vx, x, w_branch, w_out, g_ffn, w_router, b_router, tm):
    b, s, d = x.shape
    m = kx.shape[1]
    gate_blk = lambda n: pl.BlockSpec((1, tm, d), lambda bi, si, n=n: (bi, si, COL_GATE * PROJ_TILE // d + n))
    tok512 = lambda: pl.BlockSpec((1, tm, SB_WIDTH), lambda bi, si: (bi, si, 0))
    const2 = lambda shape: pl.BlockSpec(shape, lambda bi, si: (0, 0))
    return pl.pallas_call(
        _merge_kernel,
        grid=(b, s // tm),
        in_specs=[
            pl.BlockSpec((1, tm, X_WIDTH), lambda bi, si: (bi, si, COL_QX)),
            gate_blk(0), gate_blk(1), gate_blk(2),
            tok512(), tok512(),
            pl.BlockSpec((1, m, X_WIDTH), lambda bi, si: (bi, 0, 0)),
            pl.BlockSpec((1, m, X_WIDTH), lambda bi, si: (bi, 0, 0)),
            pl.BlockSpec((1, tm, d), lambda bi, si: (bi, si, 0)),
            pl.BlockSpec((N_BRANCH, SB_WIDTH, d), lambda bi, si: (0, 0, 0)),
            const2((d, d)), const2((1, d)), const2((d, LANES)), const2((1, LANES)),
        ],
        out_specs=[
            pl.BlockSpec((1, tm, d), lambda bi, si: (bi, si, 0)),
            pl.BlockSpec((1, tm, d), lambda bi, si: (bi, si, 0)),
            pl.BlockSpec((1, tm, LANES), lambda bi, si: (bi, si, 0)),
        ],
        out_shape=[
            jax.ShapeDtypeStruct((b, s, d), F32),
            jax.ShapeDtypeStruct((b, s, d), BF16),
            jax.ShapeDtypeStruct((b, s, LANES), F32),
        ],
        compiler_params=_params(("parallel", "parallel")),
        name="merge_router",
    )(proj3, proj3, proj3, proj3, o_sb, o_lru, kx, vx, x, w_branch, w_out, g_ffn, w_router, b_router)


def _moe_sort(hn_ref, comb_ref, xs_ref, cs_ref, ys_ref, pt_ref, seg_ref, *, tt, rows):
    comb = comb_ref[...]
    comb_t = comb.T
    gid_row = comb_t[MOE_GID_LANE:MOE_GID_LANE + 1, :]
    sub = lax.broadcasted_iota(jnp.int32, (SUBLANES, tt), 0).astype(F32)
    member = jnp.where(sub == gid_row, 1.0, 0.0)
    r = lax.broadcasted_iota(jnp.int32, (MOE_SUB, MOE_SUB), 0)
    c = lax.broadcasted_iota(jnp.int32, (MOE_SUB, MOE_SUB), 1)
    upto = jnp.where(r <= c, 1.0, 0.0).astype(BF16)
    before = jnp.zeros((SUBLANES, 1), F32)
    ranks = []
    for blk in range(tt // MOE_SUB):
        m = member[:, blk * MOE_SUB:(blk + 1) * MOE_SUB]
        incl = jnp.dot(m.astype(BF16), upto, preferred_element_type=F32) + before
        ranks.append(jnp.sum(m * (incl - m), axis=0, keepdims=True))
        before = before + jnp.sum(m, axis=1, keepdims=True)
    pos_row = jnp.concatenate(ranks, axis=1)
    start = jnp.int32(0)
    for g in range(N_GROUPS):
        count = jnp.sum(member[g:g + 1, :]).astype(jnp.int32)
        padded = lax.shift_left(lax.shift_right_logical(count + (MOE_PAD - 1), MOE_PAD_LOG2), MOE_PAD_LOG2)
        seg_ref[g] = start
        seg_ref[N_GROUPS + g] = padded
        pos_row = pos_row + member[g:g + 1, :] * start.astype(F32)
        start = start + padded

    hi, lo = _split_bf16(comb)
    hn = hn_ref[...]
    for blk in range(rows // LANES):
        rr = (lax.broadcasted_iota(jnp.int32, (LANES, tt), 0) + blk * LANES).astype(F32)
        p = jnp.where(rr == pos_row, 1.0, 0.0).astype(BF16)
        sl = pl.ds(blk * LANES, LANES)
        xs_ref[sl, :] = jnp.dot(p, hn, preferred_element_type=F32).astype(BF16)
        cs_ref[sl, :] = jnp.dot(p, hi, preferred_element_type=F32) + jnp.dot(p, lo, preferred_element_type=F32)
    tail = pl.ds(rows, MOE_SUB)
    xs_ref[tail, :] = jnp.zeros((MOE_SUB, xs_ref.shape[1]), BF16)
    cs_ref[tail, :] = jnp.zeros((MOE_SUB, LANES), F32)
    ys_ref[...] = jnp.zeros_like(ys_ref)

    sub_t = lax.broadcasted_iota(jnp.int32, (LANES, tt), 0)
    pos_col = jnp.where(sub_t == MOE_POS_LANE, pos_row, comb_t).T[:, MOE_POS_LANE:MOE_POS_LANE + 1]
    for blk in range(tt // LANES):
        cc = lax.broadcasted_iota(jnp.int32, (LANES, rows), 1).astype(F32)
        sl = pl.ds(blk * LANES, LANES)
        pt_ref[sl, :] = jnp.where(cc == pos_col[blk * LANES:(blk + 1) * LANES, :], 1.0, 0.0).astype(BF16)


def _moe_kernel(hn_ref, comb_ref, x1_ref, wgu_ref, wd_ref, o_ref, xs_ref, cs_ref, ys_ref, pt_ref, seg_ref,
                *, tt, rows):
    c = pl.program_id(1)

    @pl.when(c == 0)
    def _():
        _moe_sort(hn_ref, comb_ref, xs_ref, cs_ref, ys_ref, pt_ref, seg_ref, tt=tt, rows=rows)

    group = lax.shift_right_logical(c, (EXPERTS_PER_GROUP // MOE_CHUNK).bit_length() - 1)
    start = seg_ref[group]
    n_sub = lax.shift_right_logical(seg_ref[N_GROUPS + group] + (MOE_SUB - 1), MOE_SUB.bit_length() - 1)
    lane = lax.broadcasted_iota(jnp.int32, (MOE_SUB, LANES), 1)

    def sub_tile(s, carry):
        sl = pl.ds(pl.multiple_of(start + s * MOE_SUB, MOE_PAD), MOE_SUB)
        x = xs_ref[sl, :]
        cw = cs_ref[sl, :]
        acc = ys_ref[sl, :]
        for k in range(MOE_CHUNK):
            gu = jnp.dot(x, wgu_ref[k], preferred_element_type=F32)
            gate, up = gu[:, :EXPERT_FF], gu[:, EXPERT_FF:]
            weight = jnp.sum(jnp.where(lane == c * MOE_CHUNK + k, cw, 0.0), axis=-1, keepdims=True)
            act = jax.nn.silu(gate) * up * weight
            acc = acc + jnp.dot(act.astype(BF16), wd_ref[k], preferred_element_type=F32)
        ys_ref[sl, :] = acc
        return carry

    lax.fori_loop(0, n_sub, sub_tile, 0)

    @pl.when(c == pl.num_programs(1) - 1)
    def _():
        y = ys_ref[pl.ds(0, rows), :].astype(BF16)
        o_ref[...] = x1_ref[...] + jnp.dot(pt_ref[...], y, preferred_element_type=F32)


def _moe(hn2, comb2, x1_2, w_gu, w_down, tt):
    t, d = hn2.shape
    rows = -(-(tt + N_GROUPS * MOE_PAD) // LANES) * LANES
    return pl.pallas_call(
        functools.partial(_moe_kernel, tt=tt, rows=rows),
        grid=(t // tt, N_EXPERTS // MOE_CHUNK),
        in_specs=[
            pl.BlockSpec((tt, d), lambda i, c: (i, 0)),
            pl.BlockSpec((tt, LANES), lambda i, c: (i, 0)),
            pl.BlockSpec((tt, d), lambda i, c: (i, 0)),
            pl.BlockSpec((MOE_CHUNK, d, 2 * EXPERT_FF), lambda i, c: (c, 0, 0)),
            pl.BlockSpec((MOE_CHUNK, EXPERT_FF, d), lambda i, c: (c, 0, 0)),
        ],
        out_specs=pl.BlockSpec((tt, d), lambda i, c: (i, 0)),
        out_shape=jax.ShapeDtypeStruct((t, d), F32),
        scratch_shapes=[pltpu.VMEM((rows + MOE_SUB, d), BF16), pltpu.VMEM((rows + MOE_SUB, LANES), F32),
                        pltpu.VMEM((rows + MOE_SUB, d), F32), pltpu.VMEM((tt, rows), BF16),
                        pltpu.SMEM((2 * N_GROUPS,), jnp.int32)],
        compiler_params=_params(("parallel", "arbitrary")),
        name="moe",
    )(hn2, comb2, x1_2, w_gu, w_down)


def _block_diag(w):
    n, bd, _ = w.shape
    eye = jnp.eye(n, dtype=w.dtype)
    return jnp.einsum("nij,nm->nimj", w, eye).reshape(n * bd, n * bd)


def _tile(n, pref):
    while n % pref:
        pref //= 2
    return pref


def _layer(x, mem, g_mix, w_in, g_q_sb, g_k_sb, conv_w, conv_b, lru_w_a, lru_b_a, lru_w_i, lru_b_i,
           lru_lambda, g_mem, w_mem_kv, g_q_x, g_k_x, w_branch, w_out, g_ffn, w_group, b_group,
           w_expert, b_expert, w_gate, w_up, w_down):
    b, s, d = x.shape
    t = b * s
    row = lambda v: v.reshape(1, -1).astype(F32)

    ones = jnp.ones((PROJ_TILE,), F32)
    col_gain = jnp.concatenate([
        jnp.tile(g_q_sb, SB_WIDTH // SB_HEAD_DIM) * (SB_HEAD_DIM ** -0.5 * LOG2E),
        jnp.tile(g_k_sb, SB_WIDTH // SB_HEAD_DIM),
        ones, ones, ones,
        jnp.tile(g_q_x, X_HEADS) * X_HEAD_DIM ** -0.5,
    ] + [ones] * 6).reshape(1, -1)
    w_router = jnp.zeros((d, LANES), F32).at[:, :N_EXPERTS].set(w_expert)
    w_router = w_router.at[:, N_EXPERTS:N_EXPERTS + N_GROUPS].set(w_group)
    b_router = jnp.zeros((1, LANES), F32).at[0, :N_EXPERTS].set(b_expert)
    b_router = b_router.at[0, N_EXPERTS:N_EXPERTS + N_GROUPS].set(b_group)
    w_gu = jnp.concatenate([w_gate, w_up], axis=-1).reshape(N_EXPERTS, d, 2 * EXPERT_FF).astype(BF16)
    w_dn = w_down.reshape(N_EXPERTS, EXPERT_FF, d).astype(BF16)

    proj = _in_proj(x.reshape(t, d), row(g_mix), w_in.astype(BF16), col_gain, _tile(t, 1024))
    proj3 = proj.reshape(b, s, -1)
    q_gain = jnp.abs(col_gain[0, COL_Q * PROJ_TILE:(COL_Q + 1) * PROJ_TILE])
    z_max = 1.05 * SB_HEAD_DIM * jnp.max(q_gain) * jnp.max(jnp.abs(g_k_sb))
    carry_done = (z_max + BF16_ZERO_EXP).reshape(1, 1).astype(F32)
    o_sb = _sb_attention(proj3, carry_done, _tile(s, 256))
    o_lru = _rglru(proj3, conv_w, row(conv_b), _block_diag(lru_w_a).astype(BF16), row(lru_b_a),
                   _block_diag(lru_w_i).astype(BF16), row(lru_b_i), row(lru_lambda), _tile(s, 256))
    kx, vx = _mem_kv(mem, row(g_mem), w_mem_kv.astype(BF16), row(jnp.tile(g_k_x, X_HEADS)))
    x1, hn, comb = _merge(proj3, o_sb, o_lru, kx, vx, x, w_branch.astype(BF16), w_out.astype(BF16),
                          row(g_ffn), w_router, b_router, _tile(s, 512))
    out = _moe(hn.reshape(t, d), comb.reshape(t, LANES), x1.reshape(t, d), w_gu, w_dn, _tile(t, 1024))
    return out.reshape(b, s, d)


def kernel(x, mem, g_mix, w_in, g_q_sb, g_k_sb, conv_w, conv_b, lru_w_a, lru_b_a, lru_w_i, lru_b_i, lru_lambda, g_mem, w_mem_kv, g_q_x, g_k_x, w_branch, w_out, g_ffn, w_group, b_group, w_expert, b_expert, w_gate, w_up, w_down):
    params = (g_mix, w_in, g_q_sb, g_k_sb, conv_w, conv_b, lru_w_a, lru_b_a, lru_w_i, lru_b_i, lru_lambda,
              g_mem, w_mem_kv, g_q_x, g_k_x, w_branch, w_out, g_ffn, w_group, b_group, w_expert, b_expert,
              w_gate, w_up, w_down)
    for layer in range(g_mix.shape[0]):
        x = _layer(x, mem, *[p[layer] for p in params])
    return x
```

```python
import functools

import jax
import jax.numpy as jnp
from jax import lax
from jax.experimental import pallas as pl
from jax.experimental.pallas import tpu as pltpu

F32 = jnp.float32
BF16 = jnp.bfloat16

EPS = 1e-6
SB_HEAD_DIM = 64
SB_WIDTH = 512
LRU_WIDTH = 512
LRU_BLOCKS = 8
CONV_WIDTH = 4
LRU_C = 8.0
X_HEADS = 4
X_WIDTH = 512
X_HEAD_DIM = 128
N_BRANCH = 3
N_GROUPS = 4
EXPERTS_PER_GROUP = 8
N_EXPERTS = N_GROUPS * EXPERTS_PER_GROUP
EXPERT_FF = 256

LOG2E = 1.4426950408889634
SB_GROUP = 4
SB_GROUP_WIDTH = SB_GROUP * SB_HEAD_DIM
BF16_ZERO_EXP = 160.0

MOE_CHUNK = 4
MXU_TILE = 256
MOE_PAD_LOG2 = 4
MOE_PAD = 1 << MOE_PAD_LOG2
MOE_GID_LANE = N_EXPERTS
MOE_POS_LANE = N_EXPERTS + 1
SUBLANES = 8

LANES = 128
BF16_ROWS = 16
VMEM_LIMIT = 56 * 1024 * 1024

COL_Q, COL_K, COL_V, COL_XL, COL_YL, COL_QX, COL_GATE = 0, 1, 2, 3, 4, 5, 6
PROJ_TILE = 512
KIND_HEAD64, KIND_HEAD128, KIND_PLAIN, KIND_SIGMOID = 0, 1, 2, 3
PROJ_KINDS = (KIND_HEAD64, KIND_HEAD64, KIND_PLAIN, KIND_PLAIN, KIND_PLAIN, KIND_HEAD128) + (KIND_SIGMOID,) * 6


def _params(sem):
    return pltpu.CompilerParams(dimension_semantics=sem, vmem_limit_bytes=VMEM_LIMIT)


def _rms(xf, g):
    return xf * lax.rsqrt(jnp.mean(xf * xf, axis=-1, keepdims=True) + EPS) * g


def _group_mean_matrix(n, group):
    shift = group.bit_length() - 1
    r = lax.shift_right_logical(lax.broadcasted_iota(jnp.int32, (n, n), 0), shift)
    c = lax.shift_right_logical(lax.broadcasted_iota(jnp.int32, (n, n), 1), shift)
    return jnp.where(r == c, 1.0 / group, 0.0).astype(BF16)


def _proj_kernel(x_ref, g_ref, w_ref, cg_ref, o_ref):
    hn = _rms(x_ref[...], g_ref[...]).astype(BF16)
    group_mean = {KIND_HEAD64: _group_mean_matrix(PROJ_TILE, SB_HEAD_DIM),
                  KIND_HEAD128: _group_mean_matrix(PROJ_TILE, X_HEAD_DIM)}
    for j, kind in enumerate(PROJ_KINDS):
        cols = pl.ds(j * PROJ_TILE, PROJ_TILE)
        acc = jnp.dot(hn, w_ref[:, cols], preferred_element_type=F32)
        if kind in group_mean:
            ms = jnp.dot((acc * acc).astype(BF16), group_mean[kind], preferred_element_type=F32)
            acc = acc * lax.rsqrt(ms + EPS) * cg_ref[:, cols]
        elif kind == KIND_SIGMOID:
            acc = jax.nn.sigmoid(acc)
        o_ref[:, cols] = acc.astype(o_ref.dtype)


def _in_proj(x2, g_mix, w_cat, col_gain, tm):
    t, d = x2.shape
    ncols = w_cat.shape[1]
    resident = lambda shape: pl.BlockSpec(shape, lambda i: (0, 0), pipeline_mode=pl.Buffered(1))
    return pl.pallas_call(
        _proj_kernel,
        grid=(t // tm,),
        in_specs=[
            pl.BlockSpec((tm, d), lambda i: (i, 0)),
            resident((1, d)),
            resident((d, ncols)),
            resident((1, ncols)),
        ],
        out_specs=pl.BlockSpec((tm, ncols), lambda i: (i, 0)),
        out_shape=jax.ShapeDtypeStruct((t, ncols), BF16),
        compiler_params=_params(("parallel",)),
        name="in_proj",
    )(x2, g_mix, w_cat, col_gain)


def _sb_attn_kernel(done_ref, q_ref, k_ref, v_ref, o_ref, vexp_ref, acc_ref, carry_ref, cprev_ref, z_ref, d_ref, w_ref,
                    *, tq, nkb):
    qi = pl.program_id(2)
    lane = lax.broadcasted_iota(jnp.int32, (tq, SB_GROUP_WIDTH), 1)
    head_lanes = [(lane >= h * SB_HEAD_DIM) & (lane < (h + 1) * SB_HEAD_DIM) for h in range(SB_GROUP)]
    keep_head = lambda m, a: jnp.where(m, a.astype(F32), 0.0).astype(BF16)

    @pl.when(qi == 0)
    def _():
        def fill(j, c):
            v4 = v_ref[0, pl.ds(pl.multiple_of(j * tq, tq), tq), :]
            for h in range(SB_GROUP):
                vexp_ref[j, pl.ds(h * tq, tq), :] = keep_head(head_lanes[h], v4)
            return c
        lax.fori_loop(0, nkb, fill, 0)

    q4 = q_ref[0]
    q_h = [keep_head(m, q4) for m in head_lanes]
    row = lax.broadcasted_iota(jnp.int32, (tq, tq), 0)
    col = lax.broadcasted_iota(jnp.int32, (tq, tq), 1)
    later_or_self = jnp.where(row >= col, 1.0, 0.0).astype(BF16)
    causal = col < row
    heads = range(SB_GROUP)
    sign_bit = jnp.uint32(0x80000000)

    def keys(kb):
        return k_ref[0, pl.ds(pl.multiple_of(kb * tq, tq), tq), :]

    def scores(h, k4):
        z_ref[h] = lax.dot_general(q_h[h], k4, (((1,), (1,)), ((), ())), preferred_element_type=F32)

    def sums(h, diag):
        z = z_ref[h]
        neg_abs = lax.bitcast_convert_type(lax.bitcast_convert_type(z, jnp.uint32) | sign_bit, F32)
        sp = jnp.maximum(z, 0.0) + jnp.log(1.0 + jnp.exp2(neg_abs)) * LOG2E
        if diag:
            sp = jnp.where(causal, sp, 0.0)
        hi = sp.astype(BF16)
        lo = (sp - hi.astype(F32)).astype(BF16)
        d = z - (jnp.dot(hi, later_or_self, preferred_element_type=F32)
                 + jnp.dot(lo, later_or_self, preferred_element_type=F32))
        d_ref[h] = jnp.where(causal, d, -jnp.inf) if diag else d
        carry = carry_ref[h]
        cprev_ref[h] = carry
        carry_ref[h] = carry + jnp.sum(sp, axis=-1, keepdims=True)

    def weights(h):
        carry_t = jnp.concatenate([cprev_ref[h]] * (tq // LANES), axis=1)
        w_ref[:, pl.ds(h * tq, tq)] = jnp.exp2(d_ref[h] - carry_t).astype(BF16)

    def values(kb):
        acc_ref[...] += jnp.dot(w_ref[...], vexp_ref[kb], preferred_element_type=F32)

    acc_ref[...] = jnp.zeros_like(acc_ref)
    carry_ref[...] = jnp.zeros_like(carry_ref)
    w_ref[...] = jnp.zeros_like(w_ref)
    k_diag = keys(qi)
    for h in heads:
        scores(h, k_diag)
    k_next = keys(jnp.maximum(qi - 1, 0))
    for h in heads:
        sums(h, True)
        scores(h, k_next)

    def stick_left():
        least = functools.reduce(jnp.minimum, [carry_ref[h] for h in heads])
        return jnp.min(least) < done_ref[0, 0]

    def body(c):
        j, _ = c
        values(jnp.minimum(qi - j + 2, nkb - 1))
        for h in heads:
            weights(h)
        k_next = keys(jnp.maximum(qi - j - 1, 0))
        for h in heads:
            sums(h, False)
            scores(h, k_next)
        return j + 1, stick_left()

    j_end, _ = lax.while_loop(lambda c: (c[0] <= qi) & c[1], body, (jnp.int32(1), stick_left()))
    values(jnp.minimum(qi - j_end + 2, nkb - 1))
    for h in heads:
        weights(h)
    values(qi - j_end + 1)
    o_ref[0] = acc_ref[...].astype(o_ref.dtype)


def _sb_attention(proj3, carry_done, tq):
    b, s, _ = proj3.shape
    w = SB_GROUP_WIDTH
    groups = SB_WIDTH // w
    qoff, koff, voff = (COL_Q * PROJ_TILE // w, COL_K * PROJ_TILE // w, COL_V * PROJ_TILE // w)
    return pl.pallas_call(
        functools.partial(_sb_attn_kernel, tq=tq, nkb=s // tq),
        grid=(b, groups, s // tq),
        in_specs=[
            pl.BlockSpec(memory_space=pltpu.SMEM),
            pl.BlockSpec((1, tq, w), lambda bi, p, qi: (bi, qi, qoff + p)),
            pl.BlockSpec((1, s, w), lambda bi, p, qi: (bi, 0, koff + p)),
            pl.BlockSpec((1, s, w), lambda bi, p, qi: (bi, 0, voff + p)),
        ],
        out_specs=pl.BlockSpec((1, tq, w), lambda bi, p, qi: (bi, qi, p)),
        out_shape=jax.ShapeDtypeStruct((b, s, SB_WIDTH), BF16),
        scratch_shapes=[pltpu.VMEM((s // tq, SB_GROUP * tq, w), BF16),
                        pltpu.VMEM((tq, w), F32), pltpu.VMEM((SB_GROUP, tq, LANES), F32),
                        pltpu.VMEM((SB_GROUP, tq, LANES), F32), pltpu.VMEM((SB_GROUP, tq, tq), F32),
                        pltpu.VMEM((SB_GROUP, tq, tq), F32), pltpu.VMEM((tq, SB_GROUP * tq), BF16)],
        compiler_params=_params(("parallel", "parallel", "arbitrary")),
        name="sb_attention",
    )(carry_done, proj3, proj3, proj3)


def _lru_kernel(x_ref, y_ref, cw_ref, cb_ref, wa_ref, ba_ref, wi_ref, bi_ref, lam_ref, o_ref,
                shift_ref, tail_ref, h_ref, *, ts, nb):
    si = pl.program_id(1)

    @pl.when(si == 0)
    def _():
        tail_ref[...] = jnp.zeros_like(tail_ref)
        h_ref[...] = jnp.zeros_like(h_ref)
        t_idx = lax.broadcasted_iota(jnp.int32, (ts, ts), 0)
        c_idx = lax.broadcasted_iota(jnp.int32, (ts, ts), 1)
        for d in range(1, CONV_WIDTH):
            shift_ref[d - 1] = jnp.where(c_idx == t_idx - d, 1.0, 0.0).astype(BF16)

    first_rows = lax.broadcasted_iota(jnp.int32, (SUBLANES, LRU_WIDTH), 0)
    nlam = -lam_ref[...]
    sp_nlam = jnp.maximum(nlam, 0.0) + jnp.log(1.0 + jnp.exp(-jnp.abs(nlam)))

    def conv_and_gates(s):
        x_tile = x_ref[s]
        x_f32 = x_tile.astype(F32)
        before = tail_ref[s]
        xc = cb_ref[...] + cw_ref[pl.ds(CONV_WIDTH - 1, 1), :] * x_f32
        for d in range(1, CONV_WIDTH):
            delayed = jnp.dot(shift_ref[d - 1], x_tile, preferred_element_type=F32)
            head = delayed[:SUBLANES] + jnp.where(first_rows < d, pltpu.roll(before, d, 0), 0.0)
            delayed = jnp.concatenate([head, delayed[SUBLANES:]], axis=0)
            xc = xc + cw_ref[pl.ds(CONV_WIDTH - 1 - d, 1), :] * delayed
        tail_ref[s] = x_f32[ts - SUBLANES:]
        xcb = xc.astype(BF16)
        r = jax.nn.sigmoid(jnp.dot(xcb, wa_ref[...], preferred_element_type=F32) + ba_ref[...])
        gi = jax.nn.sigmoid(jnp.dot(xcb, wi_ref[...], preferred_element_type=F32) + bi_ref[...])
        return xc, r, gi

    def recurrence(s, xc, r, gi):
        log_a = (-LRU_C * r) * sp_nlam
        a = jnp.exp(log_a)
        th = jnp.tanh(log_a)
        n = -2.0 * th
        coef = jnp.where(n > 0.0, n * lax.rsqrt(n * (1.0 - th)), 0.0)
        b = coef * (gi * xc)
        a = a.reshape(ts // SUBLANES, SUBLANES, LRU_WIDTH)
        b = b.reshape(ts // SUBLANES, SUBLANES, LRU_WIDTH)
        row_in_vreg = lax.broadcasted_iota(jnp.int32, a.shape, 1)
        d = 1
        while d < SUBLANES:
            keep = row_in_vreg >= d
            b = jnp.where(keep, a * pltpu.roll(b, d, 1), 0.0) + b
            a = jnp.where(keep, a * pltpu.roll(a, d, 1), a)
            d *= 2
        state = h_ref[s, pl.ds(0, 1), :]
        pieces = []
        for i in range(ts // SUBLANES):
            pieces.append(b[i] + a[i] * state)
            state = pieces[-1][SUBLANES - 1:SUBLANES, :]
        h = jnp.concatenate(pieces, axis=0)
        h_ref[s] = jnp.broadcast_to(state, (SUBLANES, LRU_WIDTH))
        o_ref[s] = (h * jax.nn.gelu(y_ref[s].astype(F32))).astype(o_ref.dtype)

    fronts = [conv_and_gates(s) for s in range(nb)]
    for s in range(nb):
        recurrence(s, *fronts[s])


def _rglru(proj3, conv_w, conv_b, wa_bd, b_a, wi_bd, b_i, lam, ts):
    b, s, _ = proj3.shape
    vec = lambda: pl.BlockSpec((1, LRU_WIDTH), lambda bi, si: (0, 0))
    mat = lambda: pl.BlockSpec((LRU_WIDTH, LRU_WIDTH), lambda bi, si: (0, 0))
    nb = 2 if b % 2 == 0 else 1
    return pl.pallas_call(
        functools.partial(_lru_kernel, ts=ts, nb=nb),
        grid=(b // nb, s // ts),
        in_specs=[
            pl.BlockSpec((nb, ts, LRU_WIDTH), lambda bi, si: (bi, si, COL_XL)),
            pl.BlockSpec((nb, ts, LRU_WIDTH), lambda bi, si: (bi, si, COL_YL)),
            pl.BlockSpec((CONV_WIDTH, LRU_WIDTH), lambda bi, si: (0, 0)),
            vec(), mat(), vec(), mat(), vec(), vec(),
        ],
        out_specs=pl.BlockSpec((nb, ts, LRU_WIDTH), lambda bi, si: (bi, si, 0)),
        out_shape=jax.ShapeDtypeStruct((b, s, LRU_WIDTH), BF16),
        scratch_shapes=[pltpu.VMEM((CONV_WIDTH - 1, ts, ts), BF16),
                        pltpu.VMEM((nb, SUBLANES, LRU_WIDTH), F32),
                        pltpu.VMEM((nb, SUBLANES, LRU_WIDTH), F32)],
        compiler_params=_params(("parallel", "arbitrary")),
        name="rglru",
    )(proj3, proj3, conv_w, conv_b, wa_bd, b_a, wi_bd, b_i, lam)


def _mem_kv_kernel(m_ref, g_ref, w_ref, gk_ref, k_ref, v_ref):
    mn = _rms(m_ref[0], g_ref[...]).astype(BF16)
    kv = jnp.dot(mn, w_ref[...], preferred_element_type=F32)
    k = kv[:, :X_WIDTH]
    ms = jnp.dot((k * k).astype(BF16), _group_mean_matrix(X_WIDTH, X_HEAD_DIM), preferred_element_type=F32)
    k_ref[0] = (k * lax.rsqrt(ms + EPS) * gk_ref[...]).astype(BF16)
    v_ref[0] = kv[:, X_WIDTH:].astype(BF16)


def _mem_kv(mem, g_mem, w_kv, gk_cols):
    b, m, d = mem.shape
    return pl.pallas_call(
        _mem_kv_kernel,
        grid=(b,),
        in_specs=[
            pl.BlockSpec((1, m, d), lambda bi: (bi, 0, 0)),
            pl.BlockSpec((1, d), lambda bi: (0, 0)),
            pl.BlockSpec((d, 2 * X_WIDTH), lambda bi: (0, 0)),
            pl.BlockSpec((1, X_WIDTH), lambda bi: (0, 0)),
        ],
        out_specs=[pl.BlockSpec((1, m, X_WIDTH), lambda bi: (bi, 0, 0))] * 2,
        out_shape=[jax.ShapeDtypeStruct((b, m, X_WIDTH), BF16)] * 2,
        compiler_params=_params(("parallel",)),
        name="mem_kv",
    )(mem, g_mem, w_kv, gk_cols)


def _split_bf16(v):
    hi = v.astype(BF16)
    return hi, (v - hi.astype(F32)).astype(BF16)


def _router(logits):
    lane = lax.broadcasted_iota(jnp.int32, logits.shape, 1).astype(F32)
    ninf = -jnp.inf
    far = float(LANES)
    is_group = (lane >= N_EXPERTS) & (lane < N_EXPERTS + N_GROUPS)
    gl = jnp.where(is_group, logits, ninf)
    gmax = jnp.max(gl, axis=-1, keepdims=True)
    gidx = jnp.min(jnp.where(gl == gmax, lane, far), axis=-1, keepdims=True) - N_EXPERTS
    g_prob = 1.0 / jnp.sum(jnp.exp(gl - gmax), axis=-1, keepdims=True)
    first = gidx * EXPERTS_PER_GROUP
    el = jnp.where((lane >= first) & (lane < first + EXPERTS_PER_GROUP), logits, ninf)
    m1 = jnp.max(el, axis=-1, keepdims=True)
    i1 = jnp.min(jnp.where(el == m1, lane, far), axis=-1, keepdims=True)
    el2 = jnp.where(lane == i1, ninf, el)
    m2 = jnp.max(el2, axis=-1, keepdims=True)
    i2 = jnp.min(jnp.where(el2 == m2, lane, far), axis=-1, keepdims=True)
    e2 = jnp.exp(m2 - m1)
    w1 = 1.0 / (1.0 + e2)
    w2 = e2 / (1.0 + e2)
    combine = g_prob * (jnp.where(lane == i1, w1, 0.0) + jnp.where(lane == i2, w2, 0.0))
    return combine + jnp.where(lane == MOE_GID_LANE, gidx, 0.0)


def _merge_kernel(qx_ref, g0_ref, g1_ref, g2_ref, osb_ref, olru_ref, kx_ref, vx_ref, x_ref,
                  wb_ref, wo_ref, gf_ref, wr_ref, br_ref, x1_ref, hn_ref, comb_ref):
    qx = qx_ref[0]
    heads = []
    for h in range(X_HEADS):
        sl = slice(h * X_HEAD_DIM, (h + 1) * X_HEAD_DIM)
        s = lax.dot_general(qx[:, sl], kx_ref[0][:, sl], (((1,), (1,)), ((), ())),
                            preferred_element_type=F32)
        p = jnp.exp(s - jnp.max(s, axis=-1, keepdims=True))
        p = p / jnp.sum(p, axis=-1, keepdims=True)
        heads.append(jnp.dot(p.astype(BF16), vx_ref[0][:, sl], preferred_element_type=F32))
    o_x = jnp.concatenate(heads, axis=1).astype(BF16)

    merged = g0_ref[0].astype(F32) * jnp.dot(osb_ref[0], wb_ref[0], preferred_element_type=F32)
    merged += g1_ref[0].astype(F32) * jnp.dot(olru_ref[0], wb_ref[1], preferred_element_type=F32)
    merged += g2_ref[0].astype(F32) * jnp.dot(o_x, wb_ref[2], preferred_element_type=F32)
    x1 = x_ref[0] + jnp.dot(merged.astype(BF16), wo_ref[...], preferred_element_type=F32)
    x1_ref[0] = x1

    hn = _rms(x1, gf_ref[...])
    hn_ref[0] = hn.astype(BF16)
    h_hi, h_lo = _split_bf16(hn)
    w_hi, w_lo = _split_bf16(wr_ref[...])
    logits = (jnp.dot(h_hi, w_hi, preferred_element_type=F32)
              + jnp.dot(h_hi, w_lo, preferred_element_type=F32)
              + jnp.dot(h_lo, w_hi, preferred_element_type=F32)) + br_ref[...]
    comb_ref[0] = _router(logits)


def _merge(proj3, o_sb, o_lru, kx, vx, x, w_branch, w_out, g_ffn, w_router, b_router, tm):
    b, s, d = x.shape
    m = kx.shape[1]
    gate_blk = lambda n: pl.BlockSpec((1, tm, d), lambda bi, si, n=n: (bi, si, COL_GATE * PROJ_TILE // d + n))
    tok512 = lambda: pl.BlockSpec((1, tm, SB_WIDTH), lambda bi, si: (bi, si, 0))
    const2 = lambda shape: pl.BlockSpec(shape, lambda bi, si: (0, 0))
    return pl.pallas_call(
        _merge_kernel,
        grid=(b, s // tm),
        in_specs=[
            pl.BlockSpec((1, tm, X_WIDTH), lambda bi, si: (bi, si, COL_QX)),
            gate_blk(0), gate_blk(1), gate_blk(2),
            tok512(), tok512(),
            pl.BlockSpec((1, m, X_WIDTH), lambda bi, si: (bi, 0, 0)),
            pl.BlockSpec((1, m, X_WIDTH), lambda bi, si: (bi, 0, 0)),
            pl.BlockSpec((1, tm, d), lambda bi, si: (bi, si, 0)),
            pl.BlockSpec((N_BRANCH, SB_WIDTH, d), lambda bi, si: (0, 0, 0)),
            const2((d, d)), const2((1, d)), const2((d, LANES)), const2((1, LANES)),
        ],
        out_specs=[
            pl.BlockSpec((1, tm, d), lambda bi, si: (bi, si, 0)),
            pl.BlockSpec((1, tm, d), lambda bi, si: (bi, si, 0)),
            pl.BlockSpec((1, tm, LANES), lambda bi, si: (bi, si, 0)),
        ],
        out_shape=[
            jax.ShapeDtypeStruct((b, s, d), F32),
            jax.ShapeDtypeStruct((b, s, d), BF16),
            jax.ShapeDtypeStruct((b, s, LANES), F32),
        ],
        compiler_params=_params(("parallel", "parallel")),
        name="merge_router",
    )(proj3, proj3, proj3, proj3, o_sb, o_lru, kx, vx, x, w_branch, w_out, g_ffn, w_router, b_router)


def _moe_sort(hn_ref, comb_ref, xs_ref, cs_ref, ys_ref, pt_ref, seg_ref, *, tt, rows, sub):
    comb = comb_ref[...]
    comb_t = comb.T
    gid_row = comb_t[MOE_GID_LANE:MOE_GID_LANE + 1, :]
    group_of_row = lax.broadcasted_iota(jnp.int32, (SUBLANES, tt), 0).astype(F32)
    member = jnp.where(group_of_row == gid_row, 1.0, 0.0)
    r = lax.broadcasted_iota(jnp.int32, (MXU_TILE, MXU_TILE), 0)
    c = lax.broadcasted_iota(jnp.int32, (MXU_TILE, MXU_TILE), 1)
    upto = jnp.where(r <= c, 1.0, 0.0).astype(BF16)
    before = jnp.zeros((SUBLANES, 1), F32)
    ranks = []
    for blk in range(tt // MXU_TILE):
        m = member[:, blk * MXU_TILE:(blk + 1) * MXU_TILE]
        incl = jnp.dot(m.astype(BF16), upto, preferred_element_type=F32) + before
        ranks.append(jnp.sum(m * (incl - m), axis=0, keepdims=True))
        before = before + jnp.sum(m, axis=1, keepdims=True)
    pos_row = jnp.concatenate(ranks, axis=1)
    start = jnp.int32(0)
    for g in range(N_GROUPS):
        count = jnp.sum(member[g:g + 1, :]).astype(jnp.int32)
        padded = lax.shift_left(lax.shift_right_logical(count + (MOE_PAD - 1), MOE_PAD_LOG2), MOE_PAD_LOG2)
        seg_ref[g] = start
        seg_ref[N_GROUPS + g] = padded
        pos_row = pos_row + member[g:g + 1, :] * start.astype(F32)
        start = start + padded

    hi_lo = jnp.concatenate(_split_bf16(comb), axis=1)
    hn = hn_ref[...]
    for blk in range(rows // LANES):
        rr = (lax.broadcasted_iota(jnp.int32, (LANES, tt), 0) + blk * LANES).astype(F32)
        p = jnp.where(rr == pos_row, 1.0, 0.0).astype(BF16)
        sl = pl.ds(blk * LANES, LANES)
        xs_ref[sl, :] = jnp.dot(p, hn, preferred_element_type=F32).astype(BF16)
        hl = jnp.dot(p, hi_lo, preferred_element_type=F32)
        cs_ref[sl, :] = hl[:, :LANES] + hl[:, LANES:]
    tail = pl.ds(rows, sub)
    xs_ref[tail, :] = jnp.zeros((sub, xs_ref.shape[1]), BF16)
    cs_ref[tail, :] = jnp.zeros((sub, LANES), F32)
    ys_ref[...] = jnp.zeros_like(ys_ref)

    sub_t = lax.broadcasted_iota(jnp.int32, (LANES, tt), 0)
    pos_col = jnp.where(sub_t == MOE_POS_LANE, pos_row, comb_t).T[:, MOE_POS_LANE:MOE_POS_LANE + 1]
    for blk in range(tt // LANES):
        cc = lax.broadcasted_iota(jnp.int32, (LANES, rows), 1).astype(F32)
        sl = pl.ds(blk * LANES, LANES)
        pt_ref[sl, :] = jnp.where(cc == pos_col[blk * LANES:(blk + 1) * LANES, :], 1.0, 0.0).astype(BF16)


def _moe_kernel(hn_ref, comb_ref, x1_ref, wgu_ref, wd_ref, o_ref, xs_ref, cs_ref, ys_ref, pt_ref, seg_ref,
                *, tt, rows, sub):
    c = pl.program_id(1)

    @pl.when(c == 0)
    def _():
        _moe_sort(hn_ref, comb_ref, xs_ref, cs_ref, ys_ref, pt_ref, seg_ref, tt=tt, rows=rows, sub=sub)

    group = lax.shift_right_logical(c, (EXPERTS_PER_GROUP // MOE_CHUNK).bit_length() - 1)
    start = seg_ref[group]
    padded = seg_ref[N_GROUPS + group]
    n_sub = sum((padded > k * sub).astype(jnp.int32) for k in range(-(-rows // sub)))
    lane = lax.broadcasted_iota(jnp.int32, (sub, LANES), 1)

    def sub_tile(s, carry):
        sl = pl.ds(pl.multiple_of(start + s * sub, MOE_PAD), sub)
        x = xs_ref[sl, :]
        cw = cs_ref[sl, :]
        acc = ys_ref[sl, :]
        for k in range(MOE_CHUNK):
            gu = jnp.dot(x, wgu_ref[k], preferred_element_type=F32)
            gate, up = gu[:, :EXPERT_FF], gu[:, EXPERT_FF:]
            weight = jnp.sum(jnp.where(lane == c * MOE_CHUNK + k, cw, 0.0), axis=-1, keepdims=True)
            act = jax.nn.silu(gate) * up * weight
            acc = acc + jnp.dot(act.astype(BF16), wd_ref[k], preferred_element_type=F32)
        ys_ref[sl, :] = acc
        return carry

    lax.fori_loop(0, n_sub, sub_tile, 0)

    @pl.when(c == pl.num_programs(1) - 1)
    def _():
        y = ys_ref[pl.ds(0, rows), :].astype(BF16)
        o_ref[...] = x1_ref[...] + jnp.dot(pt_ref[...], y, preferred_element_type=F32)


def _moe(hn2, comb2, x1_2, w_gu, w_down, tt):
    t, d = hn2.shape
    rows = -(-(tt + N_GROUPS * MOE_PAD) // LANES) * LANES
    spread = 3.5 * (tt * (N_GROUPS - 1)) ** 0.5 / N_GROUPS
    sub = min(rows, -(-int(tt / N_GROUPS + spread) // MOE_PAD) * MOE_PAD)
    return pl.pallas_call(
        functools.partial(_moe_kernel, tt=tt, rows=rows, sub=sub),
        grid=(t // tt, N_EXPERTS // MOE_CHUNK),
        in_specs=[
            pl.BlockSpec((tt, d), lambda i, c: (i, 0)),
            pl.BlockSpec((tt, LANES), lambda i, c: (i, 0)),
            pl.BlockSpec((tt, d), lambda i, c: (i, 0)),
            pl.BlockSpec((MOE_CHUNK, d, 2 * EXPERT_FF), lambda i, c: (c, 0, 0)),
            pl.BlockSpec((MOE_CHUNK, EXPERT_FF, d), lambda i, c: (c, 0, 0)),
        ],
        out_specs=pl.BlockSpec((tt, d), lambda i, c: (i, 0)),
        out_shape=jax.ShapeDtypeStruct((t, d), F32),
        scratch_shapes=[pltpu.VMEM((rows + sub, d), BF16), pltpu.VMEM((rows + sub, LANES), F32),
                        pltpu.VMEM((rows + sub, d), F32), pltpu.VMEM((tt, rows), BF16),
                        pltpu.SMEM((2 * N_GROUPS,), jnp.int32)],
        compiler_params=_params(("parallel", "arbitrary")),
        name="moe",
    )(hn2, comb2, x1_2, w_gu, w_down)


def _block_diag(w):
    n, bd, _ = w.shape
    eye = jnp.eye(n, dtype=w.dtype)
    return jnp.einsum("nij,nm->nimj", w, eye).reshape(n * bd, n * bd)


def _tile(n, pref):
    while n % pref:
        pref //= 2
    return pref


def _layer(x, mem, g_mix, w_in, g_q_sb, g_k_sb, conv_w, conv_b, lru_w_a, lru_b_a, lru_w_i, lru_b_i,
           lru_lambda, g_mem, w_mem_kv, g_q_x, g_k_x, w_branch, w_out, g_ffn, w_group, b_group,
           w_expert, b_expert, w_gate, w_up, w_down):
    b, s, d = x.shape
    t = b * s
    row = lambda v: v.reshape(1, -1).astype(F32)

    ones = jnp.ones((PROJ_TILE,), F32)
    col_gain = jnp.concatenate([
        jnp.tile(g_q_sb, SB_WIDTH // SB_HEAD_DIM) * (SB_HEAD_DIM ** -0.5 * LOG2E),
        jnp.tile(g_k_sb, SB_WIDTH // SB_HEAD_DIM),
        ones, ones, ones,
        jnp.tile(g_q_x, X_HEADS) * X_HEAD_DIM ** -0.5,
    ] + [ones] * 6).reshape(1, -1)
    w_router = jnp.zeros((d, LANES), F32).at[:, :N_EXPERTS].set(w_expert)
    w_router = w_router.at[:, N_EXPERTS:N_EXPERTS + N_GROUPS].set(w_group)
    b_router = jnp.zeros((1, LANES), F32).at[0, :N_EXPERTS].set(b_expert)
    b_router = b_router.at[0, N_EXPERTS:N_EXPERTS + N_GROUPS].set(b_group)
    w_gu = jnp.concatenate([w_gate, w_up], axis=-1).reshape(N_EXPERTS, d, 2 * EXPERT_FF).astype(BF16)
    w_dn = w_down.reshape(N_EXPERTS, EXPERT_FF, d).astype(BF16)

    proj = _in_proj(x.reshape(t, d), row(g_mix), w_in.astype(BF16), col_gain, _tile(t, 512))
    proj3 = proj.reshape(b, s, -1)
    q_gain = jnp.abs(col_gain[0, COL_Q * PROJ_TILE:(COL_Q + 1) * PROJ_TILE])
    z_max = 1.05 * SB_HEAD_DIM * jnp.max(q_gain) * jnp.max(jnp.abs(g_k_sb))
    carry_done = (z_max + BF16_ZERO_EXP).reshape(1, 1).astype(F32)
    o_sb = _sb_attention(proj3, carry_done, _tile(s, 256))
    o_lru = _rglru(proj3, conv_w, row(conv_b), _block_diag(lru_w_a).astype(BF16), row(lru_b_a),
                   _block_diag(lru_w_i).astype(BF16), row(lru_b_i), row(lru_lambda), _tile(s, 256))
    kx, vx = _mem_kv(mem, row(g_mem), w_mem_kv.astype(BF16), row(jnp.tile(g_k_x, X_HEADS)))
    x1, hn, comb = _merge(proj3, o_sb, o_lru, kx, vx, x, w_branch.astype(BF16), w_out.astype(BF16),
                          row(g_ffn), w_router, b_router, _tile(s, 512))
    out = _moe(hn.reshape(t, d), comb.reshape(t, LANES), x1.reshape(t, d), w_gu, w_dn, _tile(t, 1024))
    return out.reshape(b, s, d)


def kernel(x, mem, g_mix, w_in, g_q_sb, g_k_sb, conv_w, conv_b, lru_w_a, lru_b_a, lru_w_i, lru_b_i, lru_lambda, g_mem, w_mem_kv, g_q_x, g_k_x, w_branch, w_out, g_ffn, w_group, b_group, w_expert, b_expert, w_gate, w_up, w_down):
    params = (g_mix, w_in, g_q_sb, g_k_sb, conv_w, conv_b, lru_w_a, lru_b_a, lru_w_i, lru_b_i, lru_lambda,
              g_mem, w_mem_kv, g_q_x, g_k_x, w_branch, w_out, g_ffn, w_group, b_group, w_expert, b_expert,
              w_gate, w_up, w_down)
    for layer in range(g_mix.shape[0]):
        x = _layer(x, mem, *[p[layer] for p in params])
    return x
```

```python
import functools

import jax
import jax.numpy as jnp
from jax import lax
from jax.experimental import pallas as pl
from jax.experimental.pallas import tpu as pltpu

F32 = jnp.float32
BF16 = jnp.bfloat16

EPS = 1e-6
SB_HEAD_DIM = 64
SB_WIDTH = 512
LRU_WIDTH = 512
LRU_BLOCKS = 8
CONV_WIDTH = 4
LRU_C = 8.0
X_HEADS = 4
X_WIDTH = 512
X_HEAD_DIM = 128
N_BRANCH = 3
N_GROUPS = 4
EXPERTS_PER_GROUP = 8
N_EXPERTS = N_GROUPS * EXPERTS_PER_GROUP
EXPERT_FF = 256

LOG2E = 1.4426950408889634
SB_GROUP = 4
SB_GROUP_WIDTH = SB_GROUP * SB_HEAD_DIM
BF16_ZERO_EXP = 160.0

MOE_CHUNK = 4
MXU_TILE = 256
MOE_PAD_LOG2 = 4
MOE_PAD = 1 << MOE_PAD_LOG2
MOE_GID_LANE = N_EXPERTS
MOE_POS_LANE = N_EXPERTS + 1
SUBLANES = 8

LANES = 128
BF16_ROWS = 16
VMEM_LIMIT = 56 * 1024 * 1024

COL_Q, COL_K, COL_V, COL_XL, COL_YL, COL_QX, COL_GATE = 0, 1, 2, 3, 4, 5, 6
PROJ_TILE = 512
KIND_HEAD64, KIND_HEAD128, KIND_PLAIN, KIND_SIGMOID = 0, 1, 2, 3
PROJ_KINDS = (KIND_HEAD64, KIND_HEAD64, KIND_PLAIN, KIND_PLAIN, KIND_PLAIN, KIND_HEAD128) + (KIND_SIGMOID,) * 6


def _params(sem):
    return pltpu.CompilerParams(dimension_semantics=sem, vmem_limit_bytes=VMEM_LIMIT)


def _rms(xf, g):
    return xf * lax.rsqrt(jnp.mean(xf * xf, axis=-1, keepdims=True) + EPS) * g


def _group_mean_matrix(n, group):
    shift = group.bit_length() - 1
    r = lax.shift_right_logical(lax.broadcasted_iota(jnp.int32, (n, n), 0), shift)
    c = lax.shift_right_logical(lax.broadcasted_iota(jnp.int32, (n, n), 1), shift)
    return jnp.where(r == c, 1.0 / group, 0.0).astype(BF16)


def _proj_kernel(x_ref, g_ref, w_ref, cg_ref, o_ref):
    hn = _rms(x_ref[...], g_ref[...]).astype(BF16)
    group_mean = {KIND_HEAD64: _group_mean_matrix(PROJ_TILE, SB_HEAD_DIM),
                  KIND_HEAD128: _group_mean_matrix(PROJ_TILE, X_HEAD_DIM)}
    for j, kind in enumerate(PROJ_KINDS):
        cols = pl.ds(j * PROJ_TILE, PROJ_TILE)
        acc = jnp.dot(hn, w_ref[:, cols], preferred_element_type=F32)
        if kind in group_mean:
            ms = jnp.dot((acc * acc).astype(BF16), group_mean[kind], preferred_element_type=F32)
            acc = acc * lax.rsqrt(ms + EPS) * cg_ref[:, cols]
        elif kind == KIND_SIGMOID:
            acc = jax.nn.sigmoid(acc)
        o_ref[:, cols] = acc.astype(o_ref.dtype)


def _in_proj(x2, g_mix, w_cat, col_gain, tm):
    t, d = x2.shape
    ncols = w_cat.shape[1]
    resident = lambda shape: pl.BlockSpec(shape, lambda i: (0, 0), pipeline_mode=pl.Buffered(1))
    return pl.pallas_call(
        _proj_kernel,
        grid=(t // tm,),
        in_specs=[
            pl.BlockSpec((tm, d), lambda i: (i, 0)),
            resident((1, d)),
            resident((d, ncols)),
            resident((1, ncols)),
        ],
        out_specs=pl.BlockSpec((tm, ncols), lambda i: (i, 0)),
        out_shape=jax.ShapeDtypeStruct((t, ncols), BF16),
        compiler_params=_params(("parallel",)),
        name="in_proj",
    )(x2, g_mix, w_cat, col_gain)


def _sb_attn_kernel(done_ref, q_ref, k_ref, v_ref, o_ref, vexp_ref, acc_ref, carry_ref, cprev_ref, z_ref, d_ref, w_ref,
                    *, tq, nkb):
    qi = pl.program_id(2)
    lane = lax.broadcasted_iota(jnp.int32, (tq, SB_GROUP_WIDTH), 1)
    head_lanes = [(lane >= h * SB_HEAD_DIM) & (lane < (h + 1) * SB_HEAD_DIM) for h in range(SB_GROUP)]
    keep_head = lambda m, a: jnp.where(m, a.astype(F32), 0.0).astype(BF16)

    @pl.when(qi == 0)
    def _():
        def fill(j, c):
            v4 = v_ref[0, pl.ds(pl.multiple_of(j * tq, tq), tq), :]
            for h in range(SB_GROUP):
                vexp_ref[j, pl.ds(h * tq, tq), :] = keep_head(head_lanes[h], v4)
            return c
        lax.fori_loop(0, nkb, fill, 0)

    q4 = q_ref[0]
    q_h = [keep_head(m, q4) for m in head_lanes]
    row = lax.broadcasted_iota(jnp.int32, (tq, tq), 0)
    col = lax.broadcasted_iota(jnp.int32, (tq, tq), 1)
    later_or_self = jnp.where(row >= col, 1.0, 0.0).astype(BF16)
    causal = col < row
    heads = range(SB_GROUP)
    sign_bit = jnp.uint32(0x80000000)

    def keys(kb):
        return k_ref[0, pl.ds(pl.multiple_of(kb * tq, tq), tq), :]

    def score(h, k4):
        return lax.dot_general(q_h[h], k4, (((1,), (1,)), ((), ())), preferred_element_type=F32)

    def softplus2(z):
        neg_abs = lax.bitcast_convert_type(lax.bitcast_convert_type(z, jnp.uint32) | sign_bit, F32)
        return jnp.maximum(z, 0.0) + jnp.log(1.0 + jnp.exp2(neg_abs)) * LOG2E

    def later_sums(sp):
        return jnp.dot(sp.astype(BF16), later_or_self, preferred_element_type=F32)

    has_left = qi > 0
    left = jnp.maximum(qi - 1, 0)
    k_pair = (keys(qi), keys(left))
    n_chain = 2 * SB_GROUP
    z_pair, sp_pair, totals, d_pair, w_pair = {}, {}, {}, {}, {}

    def pair_scores(c):
        z_pair[c] = score(c % SB_GROUP, k_pair[c // SB_GROUP])

    def pair_softplus(c):
        sp = softplus2(z_pair[c])
        sp_pair[c] = jnp.where(causal, sp, 0.0) if c < SB_GROUP else sp
        totals[c] = jnp.sum(sp_pair[c], axis=-1, keepdims=True)

    def pair_sums(c):
        d = z_pair[c] - later_sums(sp_pair[c])
        if c < SB_GROUP:
            d_pair[c] = jnp.where(causal, d, -jnp.inf)
        else:
            d_pair[c] = jnp.where(has_left, d - totals[c - SB_GROUP], -jnp.inf)

    def pair_weights(c):
        w_pair[c] = jnp.exp2(d_pair[c]).astype(BF16)

    def pair_values(block, kb):
        w = jnp.concatenate([w_pair[block * SB_GROUP + h] for h in heads], axis=1)
        return jnp.dot(w, vexp_ref[kb], preferred_element_type=F32)

    stages = (pair_scores, pair_softplus, pair_sums, pair_weights)
    for step in range(n_chain + len(stages) - 1):
        for s, stage in enumerate(stages):
            if 0 <= step - s < n_chain:
                stage(step - s)
        if step - (len(stages) - 1) == SB_GROUP - 1:
            acc_diag = pair_values(0, qi)
    acc_ref[...] = acc_diag + pair_values(1, left)
    for h in heads:
        both = totals[h] + jnp.where(has_left, totals[SB_GROUP + h], 0.0)
        carry_ref[h] = jnp.broadcast_to(both, (tq, LANES))

    def stick_left():
        least = functools.reduce(jnp.minimum, [carry_ref[h] for h in heads])
        return jnp.min(least) < done_ref[0, 0]

    def scores(h, k4):
        z_ref[h] = score(h, k4)

    def sums(h):
        z = z_ref[h]
        sp = softplus2(z)
        d_ref[h] = z - later_sums(sp)
        carry = carry_ref[h]
        cprev_ref[h] = carry
        carry_ref[h] = carry + jnp.sum(sp, axis=-1, keepdims=True)

    def weights(h):
        carry_t = jnp.concatenate([cprev_ref[h]] * (tq // LANES), axis=1)
        w_ref[:, pl.ds(h * tq, tq)] = jnp.exp2(d_ref[h] - carry_t).astype(BF16)

    def values(kb):
        acc_ref[...] += jnp.dot(w_ref[...], vexp_ref[kb], preferred_element_type=F32)

    @pl.when((qi >= 2) & stick_left())
    def _():
        first = qi - 2
        w_ref[...] = jnp.zeros_like(w_ref)
        k_first = keys(first)
        for h in heads:
            scores(h, k_first)
        k_next = keys(jnp.maximum(first - 1, 0))
        for h in heads:
            sums(h)
            scores(h, k_next)

        def body(c):
            j, _ = c
            values(jnp.minimum(first - j + 2, nkb - 1))
            for h in heads:
                weights(h)
            k_next = keys(jnp.maximum(first - j - 1, 0))
            for h in heads:
                sums(h)
                scores(h, k_next)
            return j + 1, stick_left()

        j_end, _ = lax.while_loop(lambda c: (c[0] <= first) & c[1], body, (jnp.int32(1), stick_left()))
        values(jnp.minimum(first - j_end + 2, nkb - 1))
        for h in heads:
            weights(h)
        values(first - j_end + 1)

    o_ref[0] = acc_ref[...].astype(o_ref.dtype)


def _sb_attention(proj3, carry_done, tq):
    b, s, _ = proj3.shape
    w = SB_GROUP_WIDTH
    groups = SB_WIDTH // w
    qoff, koff, voff = (COL_Q * PROJ_TILE // w, COL_K * PROJ_TILE // w, COL_V * PROJ_TILE // w)
    return pl.pallas_call(
        functools.partial(_sb_attn_kernel, tq=tq, nkb=s // tq),
        grid=(b, groups, s // tq),
        in_specs=[
            pl.BlockSpec(memory_space=pltpu.SMEM),
            pl.BlockSpec((1, tq, w), lambda bi, p, qi: (bi, qi, qoff + p)),
            pl.BlockSpec((1, s, w), lambda bi, p, qi: (bi, 0, koff + p)),
            pl.BlockSpec((1, s, w), lambda bi, p, qi: (bi, 0, voff + p)),
        ],
        out_specs=pl.BlockSpec((1, tq, w), lambda bi, p, qi: (bi, qi, p)),
        out_shape=jax.ShapeDtypeStruct((b, s, SB_WIDTH), BF16),
        scratch_shapes=[pltpu.VMEM((s // tq, SB_GROUP * tq, w), BF16),
                        pltpu.VMEM((tq, w), F32), pltpu.VMEM((SB_GROUP, tq, LANES), F32),
                        pltpu.VMEM((SB_GROUP, tq, LANES), F32), pltpu.VMEM((SB_GROUP, tq, tq), F32),
                        pltpu.VMEM((SB_GROUP, tq, tq), F32), pltpu.VMEM((tq, SB_GROUP * tq), BF16)],
        compiler_params=_params(("parallel", "parallel", "arbitrary")),
        name="sb_attention",
    )(carry_done, proj3, proj3, proj3)


def _lru_kernel(x_ref, y_ref, cw_ref, cb_ref, wa_ref, ba_ref, wi_ref, bi_ref, lam_ref, o_ref,
                shift_ref, tail_ref, h_ref, *, ts, nb):
    si = pl.program_id(1)

    @pl.when(si == 0)
    def _():
        tail_ref[...] = jnp.zeros_like(tail_ref)
        h_ref[...] = jnp.zeros_like(h_ref)
        t_idx = lax.broadcasted_iota(jnp.int32, (ts, ts), 0)
        c_idx = lax.broadcasted_iota(jnp.int32, (ts, ts), 1)
        for d in range(1, CONV_WIDTH):
            shift_ref[d - 1] = jnp.where(c_idx == t_idx - d, 1.0, 0.0).astype(BF16)

    first_rows = lax.broadcasted_iota(jnp.int32, (SUBLANES, LRU_WIDTH), 0)
    nlam = -lam_ref[...]
    sp_nlam = jnp.maximum(nlam, 0.0) + jnp.log(1.0 + jnp.exp(-jnp.abs(nlam)))

    def conv_and_gates(s):
        x_tile = x_ref[s]
        x_f32 = x_tile.astype(F32)
        before = tail_ref[s]
        xc = cb_ref[...] + cw_ref[pl.ds(CONV_WIDTH - 1, 1), :] * x_f32
        for d in range(1, CONV_WIDTH):
            delayed = jnp.dot(shift_ref[d - 1], x_tile, preferred_element_type=F32)
            head = delayed[:SUBLANES] + jnp.where(first_rows < d, pltpu.roll(before, d, 0), 0.0)
            delayed = jnp.concatenate([head, delayed[SUBLANES:]], axis=0)
            xc = xc + cw_ref[pl.ds(CONV_WIDTH - 1 - d, 1), :] * delayed
        tail_ref[s] = x_f32[ts - SUBLANES:]
        xcb = xc.astype(BF16)
        r = jax.nn.sigmoid(jnp.dot(xcb, wa_ref[...], preferred_element_type=F32) + ba_ref[...])
        gi = jax.nn.sigmoid(jnp.dot(xcb, wi_ref[...], preferred_element_type=F32) + bi_ref[...])
        return xc, r, gi

    def recurrence(s, xc, r, gi):
        log_a = (-LRU_C * r) * sp_nlam
        a = jnp.exp(log_a)
        th = jnp.tanh(log_a)
        n = -2.0 * th
        coef = jnp.where(n > 0.0, n * lax.rsqrt(n * (1.0 - th)), 0.0)
        b = coef * (gi * xc)
        a = a.reshape(ts // SUBLANES, SUBLANES, LRU_WIDTH)
        b = b.reshape(ts // SUBLANES, SUBLANES, LRU_WIDTH)
        row_in_vreg = lax.broadcasted_iota(jnp.int32, a.shape, 1)
        d = 1
        while d < SUBLANES:
            keep = row_in_vreg >= d
            b = jnp.where(keep, a * pltpu.roll(b, d, 1), 0.0) + b
            a = jnp.where(keep, a * pltpu.roll(a, d, 1), a)
            d *= 2
        state = h_ref[s, pl.ds(0, 1), :]
        pieces = []
        for i in range(ts // SUBLANES):
            pieces.append(b[i] + a[i] * state)
            state = pieces[-1][SUBLANES - 1:SUBLANES, :]
        h = jnp.concatenate(pieces, axis=0)
        h_ref[s] = jnp.broadcast_to(state, (SUBLANES, LRU_WIDTH))
        o_ref[s] = (h * jax.nn.gelu(y_ref[s].astype(F32))).astype(o_ref.dtype)

    fronts = [conv_and_gates(s) for s in range(nb)]
    for s in range(nb):
        recurrence(s, *fronts[s])


def _rglru(proj3, conv_w, conv_b, wa_bd, b_a, wi_bd, b_i, lam, ts):
    b, s, _ = proj3.shape
    vec = lambda: pl.BlockSpec((1, LRU_WIDTH), lambda bi, si: (0, 0))
    mat = lambda: pl.BlockSpec((LRU_WIDTH, LRU_WIDTH), lambda bi, si: (0, 0))
    nb = 2 if b % 2 == 0 else 1
    return pl.pallas_call(
        functools.partial(_lru_kernel, ts=ts, nb=nb),
        grid=(b // nb, s // ts),
        in_specs=[
            pl.BlockSpec((nb, ts, LRU_WIDTH), lambda bi, si: (bi, si, COL_XL)),
            pl.BlockSpec((nb, ts, LRU_WIDTH), lambda bi, si: (bi, si, COL_YL)),
            pl.BlockSpec((CONV_WIDTH, LRU_WIDTH), lambda bi, si: (0, 0)),
            vec(), mat(), vec(), mat(), vec(), vec(),
        ],
        out_specs=pl.BlockSpec((nb, ts, LRU_WIDTH), lambda bi, si: (bi, si, 0)),
        out_shape=jax.ShapeDtypeStruct((b, s, LRU_WIDTH), BF16),
        scratch_shapes=[pltpu.VMEM((CONV_WIDTH - 1, ts, ts), BF16),
                        pltpu.VMEM((nb, SUBLANES, LRU_WIDTH), F32),
                        pltpu.VMEM((nb, SUBLANES, LRU_WIDTH), F32)],
        compiler_params=_params(("parallel", "arbitrary")),
        name="rglru",
    )(proj3, proj3, conv_w, conv_b, wa_bd, b_a, wi_bd, b_i, lam)


def _mem_kv_kernel(m_ref, g_ref, w_ref, gk_ref, k_ref, v_ref):
    mn = _rms(m_ref[0], g_ref[...]).astype(BF16)
    kv = jnp.dot(mn, w_ref[...], preferred_element_type=F32)
    k = kv[:, :X_WIDTH]
    ms = jnp.dot((k * k).astype(BF16), _group_mean_matrix(X_WIDTH, X_HEAD_DIM), preferred_element_type=F32)
    k_ref[0] = (k * lax.rsqrt(ms + EPS) * gk_ref[...]).astype(BF16)
    v_ref[0] = kv[:, X_WIDTH:].astype(BF16)


def _mem_kv(mem, g_mem, w_kv, gk_cols):
    b, m, d = mem.shape
    return pl.pallas_call(
        _mem_kv_kernel,
        grid=(b,),
        in_specs=[
            pl.BlockSpec((1, m, d), lambda bi: (bi, 0, 0)),
            pl.BlockSpec((1, d), lambda bi: (0, 0)),
            pl.BlockSpec((d, 2 * X_WIDTH), lambda bi: (0, 0)),
            pl.BlockSpec((1, X_WIDTH), lambda bi: (0, 0)),
        ],
        out_specs=[pl.BlockSpec((1, m, X_WIDTH), lambda bi: (bi, 0, 0))] * 2,
        out_shape=[jax.ShapeDtypeStruct((b, m, X_WIDTH), BF16)] * 2,
        compiler_params=_params(("parallel",)),
        name="mem_kv",
    )(mem, g_mem, w_kv, gk_cols)


def _split_bf16(v):
    hi = v.astype(BF16)
    return hi, (v - hi.astype(F32)).astype(BF16)


def _router(logits):
    lane = lax.broadcasted_iota(jnp.int32, logits.shape, 1).astype(F32)
    ninf = -jnp.inf
    far = float(LANES)
    is_group = (lane >= N_EXPERTS) & (lane < N_EXPERTS + N_GROUPS)
    gl = jnp.where(is_group, logits, ninf)
    gmax = jnp.max(gl, axis=-1, keepdims=True)
    gidx = jnp.min(jnp.where(gl == gmax, lane, far), axis=-1, keepdims=True) - N_EXPERTS
    g_prob = 1.0 / jnp.sum(jnp.exp(gl - gmax), axis=-1, keepdims=True)
    first = gidx * EXPERTS_PER_GROUP
    el = jnp.where((lane >= first) & (lane < first + EXPERTS_PER_GROUP), logits, ninf)
    m1 = jnp.max(el, axis=-1, keepdims=True)
    i1 = jnp.min(jnp.where(el == m1, lane, far), axis=-1, keepdims=True)
    el2 = jnp.where(lane == i1, ninf, el)
    m2 = jnp.max(el2, axis=-1, keepdims=True)
    i2 = jnp.min(jnp.where(el2 == m2, lane, far), axis=-1, keepdims=True)
    e2 = jnp.exp(m2 - m1)
    w1 = 1.0 / (1.0 + e2)
    w2 = e2 / (1.0 + e2)
    combine = g_prob * (jnp.where(lane == i1, w1, 0.0) + jnp.where(lane == i2, w2, 0.0))
    return combine + jnp.where(lane == MOE_GID_LANE, gidx, 0.0)


def _merge_kernel(qx_ref, g0_ref, g1_ref, g2_ref, osb_ref, olru_ref, kx_ref, vx_ref, x_ref,
                  wb_ref, wo_ref, gf_ref, wr_ref, br_ref, x1_ref, hn_ref, comb_ref):
    qx = qx_ref[0]
    heads = []
    for h in range(X_HEADS):
        sl = slice(h * X_HEAD_DIM, (h + 1) * X_HEAD_DIM)
        s = lax.dot_general(qx[:, sl], kx_ref[0][:, sl], (((1,), (1,)), ((), ())),
                            preferred_element_type=F32)
        p = jnp.exp(s - jnp.max(s, axis=-1, keepdims=True))
        p = p / jnp.sum(p, axis=-1, keepdims=True)
        heads.append(jnp.dot(p.astype(BF16), vx_ref[0][:, sl], preferred_element_type=F32))
    o_x = jnp.concatenate(heads, axis=1).astype(BF16)

    merged = g0_ref[0].astype(F32) * jnp.dot(osb_ref[0], wb_ref[0], preferred_element_type=F32)
    merged += g1_ref[0].astype(F32) * jnp.dot(olru_ref[0], wb_ref[1], preferred_element_type=F32)
    merged += g2_ref[0].astype(F32) * jnp.dot(o_x, wb_ref[2], preferred_element_type=F32)
    x1 = x_ref[0] + jnp.dot(merged.astype(BF16), wo_ref[...], preferred_element_type=F32)
    x1_ref[0] = x1

    hn = _rms(x1, gf_ref[...])
    hn_ref[0] = hn.astype(BF16)
    h_hi, h_lo = _split_bf16(hn)
    w_hi, w_lo = _split_bf16(wr_ref[...])
    both = jnp.dot(h_hi, jnp.concatenate([w_hi, w_lo], axis=1), preferred_element_type=F32)
    logits = both[:, :LANES] + both[:, LANES:] + jnp.dot(h_lo, w_hi, preferred_element_type=F32) + br_ref[...]
    comb_ref[0] = _router(logits)


def _merge(proj3, o_sb, o_lru, kx, vx, x, w_branch, w_out, g_ffn, w_router, b_router, tm):
    b, s, d = x.shape
    m = kx.shape[1]
    gate_blk = lambda n: pl.BlockSpec((1, tm, d), lambda bi, si, n=n: (bi, si, COL_GATE * PROJ_TILE // d + n))
    tok512 = lambda: pl.BlockSpec((1, tm, SB_WIDTH), lambda bi, si: (bi, si, 0))
    const2 = lambda shape: pl.BlockSpec(shape, lambda bi, si: (0, 0))
    return pl.pallas_call(
        _merge_kernel,
        grid=(b, s // tm),
        in_specs=[
            pl.BlockSpec((1, tm, X_WIDTH), lambda bi, si: (bi, si, COL_QX)),
            gate_blk(0), gate_blk(1), gate_blk(2),
            tok512(), tok512(),
            pl.BlockSpec((1, m, X_WIDTH), lambda bi, si: (bi, 0, 0)),
            pl.BlockSpec((1, m, X_WIDTH), lambda bi, si: (bi, 0, 0)),
            pl.BlockSpec((1, tm, d), lambda bi, si: (bi, si, 0)),
            pl.BlockSpec((N_BRANCH, SB_WIDTH, d), lambda bi, si: (0, 0, 0)),
            const2((d, d)), const2((1, d)), const2((d, LANES)), const2((1, LANES)),
        ],
        out_specs=[
            pl.BlockSpec((1, tm, d), lambda bi, si: (bi, si, 0)),
            pl.BlockSpec((1, tm, d), lambda bi, si: (bi, si, 0)),
            pl.BlockSpec((1, tm, LANES), lambda bi, si: (bi, si, 0)),
        ],
        out_shape=[
            jax.ShapeDtypeStruct((b, s, d), F32),
            jax.ShapeDtypeStruct((b, s, d), BF16),
            jax.ShapeDtypeStruct((b, s, LANES), F32),
        ],
        compiler_params=_params(("parallel", "parallel")),
        name="merge_router",
    )(proj3, proj3, proj3, proj3, o_sb, o_lru, kx, vx, x, w_branch, w_out, g_ffn, w_router, b_router)


def _moe_sort(hn_ref, comb_ref, xs_ref, cs_ref, ys_ref, pt_ref, seg_ref, *, tt, rows, sub):
    comb = comb_ref[...]
    comb_t = comb.T
    gid_row = comb_t[MOE_GID_LANE:MOE_GID_LANE + 1, :]
    group_of_row = lax.broadcasted_iota(jnp.int32, (SUBLANES, tt), 0).astype(F32)
    member = jnp.where(group_of_row == gid_row, 1.0, 0.0)
    r = lax.broadcasted_iota(jnp.int32, (MXU_TILE, MXU_TILE), 0)
    c = lax.broadcasted_iota(jnp.int32, (MXU_TILE, MXU_TILE), 1)
    upto = jnp.where(r <= c, 1.0, 0.0).astype(BF16)
    before = jnp.zeros((SUBLANES, 1), F32)
    ranks = []
    for blk in range(tt // MXU_TILE):
        m = member[:, blk * MXU_TILE:(blk + 1) * MXU_TILE]
        incl = jnp.dot(m.astype(BF16), upto, preferred_element_type=F32) + before
        ranks.append(jnp.sum(m * (incl - m), axis=0, keepdims=True))
        before = before + jnp.sum(m, axis=1, keepdims=True)
    pos_row = jnp.concatenate(ranks, axis=1)
    start = jnp.int32(0)
    for g in range(N_GROUPS):
        count = jnp.sum(member[g:g + 1, :]).astype(jnp.int32)
        padded = lax.shift_left(lax.shift_right_logical(count + (MOE_PAD - 1), MOE_PAD_LOG2), MOE_PAD_LOG2)
        seg_ref[g] = start
        seg_ref[N_GROUPS + g] = padded
        pos_row = pos_row + member[g:g + 1, :] * start.astype(F32)
        start = start + padded

    hi_lo = jnp.concatenate(_split_bf16(comb), axis=1)
    hn = hn_ref[...]
    for blk in range(rows // LANES):
        rr = (lax.broadcasted_iota(jnp.int32, (LANES, tt), 0) + blk * LANES).astype(F32)
        p = jnp.where(rr == pos_row, 1.0, 0.0).astype(BF16)
        sl = pl.ds(blk * LANES, LANES)
        xs_ref[sl, :] = jnp.dot(p, hn, preferred_element_type=F32).astype(BF16)
        hl = jnp.dot(p, hi_lo, preferred_element_type=F32)
        cs_ref[sl, :] = hl[:, :LANES] + hl[:, LANES:]
    tail = pl.ds(rows, sub)
    xs_ref[tail, :] = jnp.zeros((sub, xs_ref.shape[1]), BF16)
    cs_ref[tail, :] = jnp.zeros((sub, LANES), F32)
    ys_ref[...] = jnp.zeros_like(ys_ref)

    sub_t = lax.broadcasted_iota(jnp.int32, (LANES, tt), 0)
    pos_col = jnp.where(sub_t == MOE_POS_LANE, pos_row, comb_t).T[:, MOE_POS_LANE:MOE_POS_LANE + 1]
    for blk in range(tt // LANES):
        cc = lax.broadcasted_iota(jnp.int32, (LANES, rows), 1).astype(F32)
        sl = pl.ds(blk * LANES, LANES)
        pt_ref[sl, :] = jnp.where(cc == pos_col[blk * LANES:(blk + 1) * LANES, :], 1.0, 0.0).astype(BF16)


def _moe_kernel(hn_ref, comb_ref, x1_ref, wgu_ref, wd_ref, o_ref, xs_ref, cs_ref, ys_ref, pt_ref, seg_ref,
                *, tt, rows, sub):
    c = pl.program_id(1)

    @pl.when(c == 0)
    def _():
        _moe_sort(hn_ref, comb_ref, xs_ref, cs_ref, ys_ref, pt_ref, seg_ref, tt=tt, rows=rows, sub=sub)

    group = lax.shift_right_logical(c, (EXPERTS_PER_GROUP // MOE_CHUNK).bit_length() - 1)
    start = seg_ref[group]
    padded = seg_ref[N_GROUPS + group]
    n_sub = sum((padded > k * sub).astype(jnp.int32) for k in range(-(-rows // sub)))
    lane = lax.broadcasted_iota(jnp.int32, (sub, LANES), 1)

    def sub_tile(s, carry):
        sl = pl.ds(pl.multiple_of(start + s * sub, MOE_PAD), sub)
        x = xs_ref[sl, :]
        cw = cs_ref[sl, :]
        acc = ys_ref[sl, :]
        for k in range(MOE_CHUNK):
            gu = jnp.dot(x, wgu_ref[k], preferred_element_type=F32)
            gate, up = gu[:, :EXPERT_FF], gu[:, EXPERT_FF:]
            weight = jnp.sum(jnp.where(lane == c * MOE_CHUNK + k, cw, 0.0), axis=-1, keepdims=True)
            act = jax.nn.silu(gate) * up * weight
            acc = acc + jnp.dot(act.astype(BF16), wd_ref[k], preferred_element_type=F32)
        ys_ref[sl, :] = acc
        return carry

    lax.fori_loop(0, n_sub, sub_tile, 0)

    @pl.when(c == pl.num_programs(1) - 1)
    def _():
        y = ys_ref[pl.ds(0, rows), :].astype(BF16)
        o_ref[...] = x1_ref[...] + jnp.dot(pt_ref[...], y, preferred_element_type=F32)


def _moe(hn2, comb2, x1_2, w_gu, w_down, tt):
    t, d = hn2.shape
    rows = -(-(tt + N_GROUPS * MOE_PAD) // LANES) * LANES
    spread = 3.5 * (tt * (N_GROUPS - 1)) ** 0.5 / N_GROUPS
    sub = min(rows, -(-int(tt / N_GROUPS + spread) // MOE_PAD) * MOE_PAD)
    return pl.pallas_call(
        functools.partial(_moe_kernel, tt=tt, rows=rows, sub=sub),
        grid=(t // tt, N_EXPERTS // MOE_CHUNK),
        in_specs=[
            pl.BlockSpec((tt, d), lambda i, c: (i, 0)),
            pl.BlockSpec((tt, LANES), lambda i, c: (i, 0)),
            pl.BlockSpec((tt, d), lambda i, c: (i, 0)),
            pl.BlockSpec((MOE_CHUNK, d, 2 * EXPERT_FF), lambda i, c: (c, 0, 0)),
            pl.BlockSpec((MOE_CHUNK, EXPERT_FF, d), lambda i, c: (c, 0, 0)),
        ],
        out_specs=pl.BlockSpec((tt, d), lambda i, c: (i, 0)),
        out_shape=jax.ShapeDtypeStruct((t, d), F32),
        scratch_shapes=[pltpu.VMEM((rows + sub, d), BF16), pltpu.VMEM((rows + sub, LANES), F32),
                        pltpu.VMEM((rows + sub, d), F32), pltpu.VMEM((tt, rows), BF16),
                        pltpu.SMEM((2 * N_GROUPS,), jnp.int32)],
        compiler_params=_params(("parallel", "arbitrary")),
        name="moe",
    )(hn2, comb2, x1_2, w_gu, w_down)


def _block_diag(w):
    n, bd, _ = w.shape
    eye = jnp.eye(n, dtype=w.dtype)
    return jnp.einsum("nij,nm->nimj", w, eye).reshape(n * bd, n * bd)


def _tile(n, pref):
    while n % pref:
        pref //= 2
    return pref


def _layer(x, mem, g_mix, w_in, g_q_sb, g_k_sb, conv_w, conv_b, lru_w_a, lru_b_a, lru_w_i, lru_b_i,
           lru_lambda, g_mem, w_mem_kv, g_q_x, g_k_x, w_branch, w_out, g_ffn, w_group, b_group,
           w_expert, b_expert, w_gate, w_up, w_down):
    b, s, d = x.shape
    t = b * s
    row = lambda v: v.reshape(1, -1).astype(F32)

    ones = jnp.ones((PROJ_TILE,), F32)
    col_gain = jnp.concatenate([
        jnp.tile(g_q_sb, SB_WIDTH // SB_HEAD_DIM) * (SB_HEAD_DIM ** -0.5 * LOG2E),
        jnp.tile(g_k_sb, SB_WIDTH // SB_HEAD_DIM),
        ones, ones, ones,
        jnp.tile(g_q_x, X_HEADS) * X_HEAD_DIM ** -0.5,
    ] + [ones] * 6).reshape(1, -1)
    w_router = jnp.zeros((d, LANES), F32).at[:, :N_EXPERTS].set(w_expert)
    w_router = w_router.at[:, N_EXPERTS:N_EXPERTS + N_GROUPS].set(w_group)
    b_router = jnp.zeros((1, LANES), F32).at[0, :N_EXPERTS].set(b_expert)
    b_router = b_router.at[0, N_EXPERTS:N_EXPERTS + N_GROUPS].set(b_group)
    w_gu = jnp.concatenate([w_gate, w_up], axis=-1).reshape(N_EXPERTS, d, 2 * EXPERT_FF).astype(BF16)
    w_dn = w_down.reshape(N_EXPERTS, EXPERT_FF, d).astype(BF16)

    proj = _in_proj(x.reshape(t, d), row(g_mix), w_in.astype(BF16), col_gain, _tile(t, 512))
    proj3 = proj.reshape(b, s, -1)
    q_gain = jnp.abs(col_gain[0, COL_Q * PROJ_TILE:(COL_Q + 1) * PROJ_TILE])
    z_max = 1.05 * SB_HEAD_DIM * jnp.max(q_gain) * jnp.max(jnp.abs(g_k_sb))
    carry_done = (z_max + BF16_ZERO_EXP).reshape(1, 1).astype(F32)
    o_sb = _sb_attention(proj3, carry_done, _tile(s, 256))
    o_lru = _rglru(proj3, conv_w, row(conv_b), _block_diag(lru_w_a).astype(BF16), row(lru_b_a),
                   _block_diag(lru_w_i).astype(BF16), row(lru_b_i), row(lru_lambda), _tile(s, 256))
    kx, vx = _mem_kv(mem, row(g_mem), w_mem_kv.astype(BF16), row(jnp.tile(g_k_x, X_HEADS)))
    x1, hn, comb = _merge(proj3, o_sb, o_lru, kx, vx, x, w_branch.astype(BF16), w_out.astype(BF16),
                          row(g_ffn), w_router, b_router, _tile(s, 512))
    out = _moe(hn.reshape(t, d), comb.reshape(t, LANES), x1.reshape(t, d), w_gu, w_dn, _tile(t, 1024))
    return out.reshape(b, s, d)


def kernel(x, mem, g_mix, w_in, g_q_sb, g_k_sb, conv_w, conv_b, lru_w_a, lru_b_a, lru_w_i, lru_b_i, lru_lambda, g_mem, w_mem_kv, g_q_x, g_k_x, w_branch, w_out, g_ffn, w_group, b_group, w_expert, b_expert, w_gate, w_up, w_down):
    params = (g_mix, w_in, g_q_sb, g_k_sb, conv_w, conv_b, lru_w_a, lru_b_a, lru_w_i, lru_b_i, lru_lambda,
              g_mem, w_mem_kv, g_q_x, g_k_x, w_branch, w_out, g_ffn, w_group, b_group, w_expert, b_expert,
              w_gate, w_up, w_down)
    for layer in range(g_mix.shape[0]):
        x = _layer(x, mem, *[p[layer] for p in params])
    return x
```

```python
import functools

import jax
import jax.numpy as jnp
from jax import lax
from jax.experimental import pallas as pl
from jax.experimental.pallas import tpu as pltpu

F32 = jnp.float32
BF16 = jnp.bfloat16

EPS = 1e-6
SB_HEAD_DIM = 64
SB_WIDTH = 512
LRU_WIDTH = 512
LRU_BLOCKS = 8
CONV_WIDTH = 4
LRU_C = 8.0
X_HEADS = 4
X_WIDTH = 512
X_HEAD_DIM = 128
N_BRANCH = 3
N_GROUPS = 4
EXPERTS_PER_GROUP = 8
N_EXPERTS = N_GROUPS * EXPERTS_PER_GROUP
EXPERT_FF = 256

LOG2E = 1.4426950408889634
SB_GROUP = 4
SB_GROUP_WIDTH = SB_GROUP * SB_HEAD_DIM
BF16_ZERO_EXP = 160.0

MOE_CHUNK = 4
MXU_TILE = 256
MOE_PAD_LOG2 = 4
MOE_PAD = 1 << MOE_PAD_LOG2
MOE_GID_LANE = N_EXPERTS
MOE_POS_LANE = N_EXPERTS + 1
SUBLANES = 8

LANES = 128
VMEM_LIMIT = 56 * 1024 * 1024

PROJ_TILE = 512
KIND_HEAD64, KIND_HEAD128, KIND_PLAIN, KIND_SIGMOID, KIND_LRU = 0, 1, 2, 3, 4
PROJ_KINDS = (KIND_HEAD64, KIND_HEAD64, KIND_PLAIN, KIND_LRU, KIND_LRU, KIND_HEAD128) + (KIND_SIGMOID,) * 6
W_COL_XL, W_COL_YL = 3, 4
PROJ_OUT_COL = {j: sum(k != KIND_LRU for k in PROJ_KINDS[:j]) for j, kind in enumerate(PROJ_KINDS) if kind != KIND_LRU}
COL_Q, COL_K, COL_V, COL_QX, COL_GATE = (PROJ_OUT_COL[j] for j in (0, 1, 2, 5, 6))


def _params(sem):
    return pltpu.CompilerParams(dimension_semantics=sem, vmem_limit_bytes=VMEM_LIMIT)


def _rms(xf, g):
    return xf * lax.rsqrt(jnp.mean(xf * xf, axis=-1, keepdims=True) + EPS) * g


def _group_mean_matrix(n, group):
    shift = group.bit_length() - 1
    r = lax.shift_right_logical(lax.broadcasted_iota(jnp.int32, (n, n), 0), shift)
    c = lax.shift_right_logical(lax.broadcasted_iota(jnp.int32, (n, n), 1), shift)
    return jnp.where(r == c, 1.0 / group, 0.0).astype(BF16)


def _proj_kernel(x_ref, g_ref, w_ref, cg_ref, cw_ref, cb_ref, wa_ref, ba_ref, wi_ref, bi_ref, lam_ref,
                 o_ref, olru_ref, xbuf_ref, h_ref, *, tm, tiles_per_seq):
    seq_tile = pl.program_id(0) % tiles_per_seq

    @pl.when(seq_tile == 0)
    def _():
        xbuf_ref[pl.ds(0, SUBLANES), :] = jnp.zeros((SUBLANES, LRU_WIDTH), F32)
        h_ref[...] = jnp.zeros_like(h_ref)

    @pl.when(seq_tile > 0)
    def _():
        xbuf_ref[pl.ds(0, SUBLANES), :] = xbuf_ref[pl.ds(tm, SUBLANES), :]

    hn = _rms(x_ref[...], g_ref[...]).astype(BF16)
    group_mean = {KIND_HEAD64: _group_mean_matrix(PROJ_TILE, SB_HEAD_DIM),
                  KIND_HEAD128: _group_mean_matrix(PROJ_TILE, X_HEAD_DIM)}

    def column_tile(j):
        return jnp.dot(hn, w_ref[:, pl.ds(j * PROJ_TILE, PROJ_TILE)], preferred_element_type=F32)

    def stored_tile(j):
        kind, acc = PROJ_KINDS[j], column_tile(j)
        if kind in group_mean:
            ms = jnp.dot((acc * acc).astype(BF16), group_mean[kind], preferred_element_type=F32)
            acc = acc * lax.rsqrt(ms + EPS) * cg_ref[:, pl.ds(j * PROJ_TILE, PROJ_TILE)]
        elif kind == KIND_SIGMOID:
            acc = jax.nn.sigmoid(acc)
        o_ref[:, pl.ds(PROJ_OUT_COL[j] * PROJ_TILE, PROJ_TILE)] = acc.astype(o_ref.dtype)

    stored = [j for j, kind in enumerate(PROJ_KINDS) if kind != KIND_LRU]
    x_lru, y_lru = column_tile(W_COL_XL), column_tile(W_COL_YL)
    for j in stored[:3]:
        stored_tile(j)

    xbuf_ref[pl.ds(SUBLANES, tm), :] = x_lru
    xc = cb_ref[...] + jnp.zeros((tm, LRU_WIDTH), F32)
    for j in range(CONV_WIDTH):
        xc = xc + cw_ref[pl.ds(j, 1), :] * xbuf_ref[pl.ds(SUBLANES - (CONV_WIDTH - 1) + j, tm), :]
    xcb = xc.astype(BF16)
    r = jax.nn.sigmoid(jnp.dot(xcb, wa_ref[...], preferred_element_type=F32) + ba_ref[...])
    gi = jax.nn.sigmoid(jnp.dot(xcb, wi_ref[...], preferred_element_type=F32) + bi_ref[...])
    for j in stored[3:6]:
        stored_tile(j)

    nlam = -lam_ref[...]
    sp_nlam = jnp.maximum(nlam, 0.0) + jnp.log(1.0 + jnp.exp(-jnp.abs(nlam)))
    state = h_ref[pl.ds(0, 1), :]
    rest = stored[6:]
    chunk = tm // len(rest)
    for c, j in enumerate(rest):
        rows = slice(c * chunk, (c + 1) * chunk)
        log_a = (-LRU_C * r[rows]) * sp_nlam
        a = jnp.exp(log_a)
        th = jnp.tanh(log_a)
        n = -2.0 * th
        coef = jnp.where(n > 0.0, n * lax.rsqrt(n * (1.0 - th)), 0.0)
        b = coef * (gi[rows] * xc[rows])
        a = a.reshape(chunk // SUBLANES, SUBLANES, LRU_WIDTH)
        b = b.reshape(chunk // SUBLANES, SUBLANES, LRU_WIDTH)
        row_in_vreg = lax.broadcasted_iota(jnp.int32, a.shape, 1)
        d = 1
        while d < SUBLANES:
            keep = row_in_vreg >= d
            b = jnp.where(keep, a * pltpu.roll(b, d, 1), 0.0) + b
            a = jnp.where(keep, a * pltpu.roll(a, d, 1), a)
            d *= 2
        pieces = []
        for i in range(chunk // SUBLANES):
            pieces.append(b[i] + a[i] * state)
            state = pieces[-1][SUBLANES - 1:SUBLANES, :]
        h = jnp.concatenate(pieces, axis=0)
        olru_ref[pl.ds(c * chunk, chunk), :] = (h * jax.nn.gelu(y_lru[rows])).astype(olru_ref.dtype)
        stored_tile(j)
    h_ref[...] = jnp.broadcast_to(state, h_ref.shape)


def _in_proj(x2, g_mix, w_cat, col_gain, conv_w, conv_b, wa_bd, b_a, wi_bd, b_i, lam, tm, seq_len):
    t, d = x2.shape
    ncols = len([k for k in PROJ_KINDS if k != KIND_LRU]) * PROJ_TILE
    resident = lambda shape: pl.BlockSpec(shape, lambda i: (0, 0), pipeline_mode=pl.Buffered(1))
    vec = lambda: resident((1, LRU_WIDTH))
    mat = lambda: resident((LRU_WIDTH, LRU_WIDTH))
    return pl.pallas_call(
        functools.partial(_proj_kernel, tm=tm, tiles_per_seq=seq_len // tm),
        grid=(t // tm,),
        in_specs=[
            pl.BlockSpec((tm, d), lambda i: (i, 0)),
            resident((1, d)),
            resident((d, w_cat.shape[1])),
            resident((1, w_cat.shape[1])),
            resident((CONV_WIDTH, LRU_WIDTH)), vec(), mat(), vec(), mat(), vec(), vec(),
        ],
        out_specs=[pl.BlockSpec((tm, ncols), lambda i: (i, 0)),
                   pl.BlockSpec((tm, LRU_WIDTH), lambda i: (i, 0))],
        out_shape=[jax.ShapeDtypeStruct((t, ncols), BF16), jax.ShapeDtypeStruct((t, LRU_WIDTH), BF16)],
        scratch_shapes=[pltpu.VMEM((tm + SUBLANES, LRU_WIDTH), F32), pltpu.VMEM((SUBLANES, LRU_WIDTH), F32)],
        compiler_params=_params(("arbitrary",)),
        name="in_proj_rglru",
    )(x2, g_mix, w_cat, col_gain, conv_w, conv_b, wa_bd, b_a, wi_bd, b_i, lam)


def _sb_attn_kernel(done_ref, q_ref, k_ref, v_ref, o_ref, vexp_ref, acc_ref, carry_ref, cprev_ref, z_ref, d_ref, w_ref,
                    *, tq, nkb):
    qi = pl.program_id(2)
    lane = lax.broadcasted_iota(jnp.int32, (tq, SB_GROUP_WIDTH), 1)
    head_lanes = [(lane >= h * SB_HEAD_DIM) & (lane < (h + 1) * SB_HEAD_DIM) for h in range(SB_GROUP)]
    keep_head = lambda m, a: jnp.where(m, a.astype(F32), 0.0).astype(BF16)

    @pl.when(qi == 0)
    def _():
        def fill(j, c):
            v4 = v_ref[0, pl.ds(pl.multiple_of(j * tq, tq), tq), :]
            for h in range(SB_GROUP):
                vexp_ref[j, pl.ds(h * tq, tq), :] = keep_head(head_lanes[h], v4)
            return c
        lax.fori_loop(0, nkb, fill, 0)

    q4 = q_ref[0]
    q_h = [keep_head(m, q4) for m in head_lanes]
    row = lax.broadcasted_iota(jnp.int32, (tq, tq), 0)
    col = lax.broadcasted_iota(jnp.int32, (tq, tq), 1)
    later_or_self = jnp.where(row >= col, 1.0, 0.0).astype(BF16)
    causal = col < row
    heads = range(SB_GROUP)
    sign_bit = jnp.uint32(0x80000000)

    def keys(kb):
        return k_ref[0, pl.ds(pl.multiple_of(kb * tq, tq), tq), :]

    def score(h, k4):
        return lax.dot_general(q_h[h], k4, (((1,), (1,)), ((), ())), preferred_element_type=F32)

    def softplus2(z):
        neg_abs = lax.bitcast_convert_type(lax.bitcast_convert_type(z, jnp.uint32) | sign_bit, F32)
        return jnp.maximum(z, 0.0) + jnp.log(1.0 + jnp.exp2(neg_abs)) * LOG2E

    def later_sums(sp):
        return jnp.dot(sp.astype(BF16), later_or_self, preferred_element_type=F32)

    has_left = qi > 0
    left = jnp.maximum(qi - 1, 0)
    k_pair = (keys(qi), keys(left))
    n_chain = 2 * SB_GROUP
    z_pair, sp_pair, totals, d_pair, w_pair = {}, {}, {}, {}, {}

    def pair_scores(c):
        z_pair[c] = score(c % SB_GROUP, k_pair[c // SB_GROUP])

    def pair_softplus(c):
        sp = softplus2(z_pair[c])
        sp_pair[c] = jnp.where(causal, sp, 0.0) if c < SB_GROUP else sp
        totals[c] = jnp.sum(sp_pair[c], axis=-1, keepdims=True)

    def pair_sums(c):
        d = z_pair[c] - later_sums(sp_pair[c])
        if c < SB_GROUP:
            d_pair[c] = jnp.where(causal, d, -jnp.inf)
        else:
            d_pair[c] = jnp.where(has_left, d - totals[c - SB_GROUP], -jnp.inf)

    def pair_weights(c):
        w_pair[c] = jnp.exp2(d_pair[c]).astype(BF16)

    def pair_values(block, kb):
        w = jnp.concatenate([w_pair[block * SB_GROUP + h] for h in heads], axis=1)
        return jnp.dot(w, vexp_ref[kb], preferred_element_type=F32)

    stages = (pair_scores, pair_softplus, pair_sums, pair_weights)
    for step in range(n_chain + len(stages) - 1):
        for s, stage in enumerate(stages):
            if 0 <= step - s < n_chain:
                stage(step - s)
        if step - (len(stages) - 1) == SB_GROUP - 1:
            acc_diag = pair_values(0, qi)
    acc_ref[...] = acc_diag + pair_values(1, left)
    for h in heads:
        both = totals[h] + jnp.where(has_left, totals[SB_GROUP + h], 0.0)
        carry_ref[h] = jnp.broadcast_to(both, (tq, LANES))

    def stick_left():
        least = functools.reduce(jnp.minimum, [carry_ref[h] for h in heads])
        return jnp.min(least) < done_ref[0, 0]

    def scores(h, k4):
        z_ref[h] = score(h, k4)

    def sums(h):
        z = z_ref[h]
        sp = softplus2(z)
        d_ref[h] = z - later_sums(sp)
        carry = carry_ref[h]
        cprev_ref[h] = carry
        carry_ref[h] = carry + jnp.sum(sp, axis=-1, keepdims=True)

    def weights(h):
        carry_t = jnp.concatenate([cprev_ref[h]] * (tq // LANES), axis=1)
        w_ref[:, pl.ds(h * tq, tq)] = jnp.exp2(d_ref[h] - carry_t).astype(BF16)

    def values(kb):
        acc_ref[...] += jnp.dot(w_ref[...], vexp_ref[kb], preferred_element_type=F32)

    @pl.when((qi >= 2) & stick_left())
    def _():
        first = qi - 2
        w_ref[...] = jnp.zeros_like(w_ref)
        k_first = keys(first)
        for h in heads:
            scores(h, k_first)
        k_next = keys(jnp.maximum(first - 1, 0))
        for h in heads:
            sums(h)
            scores(h, k_next)

        def body(c):
            j, _ = c
            values(jnp.minimum(first - j + 2, nkb - 1))
            for h in heads:
                weights(h)
            k_next = keys(jnp.maximum(first - j - 1, 0))
            for h in heads:
                sums(h)
                scores(h, k_next)
            return j + 1, stick_left()

        j_end, _ = lax.while_loop(lambda c: (c[0] <= first) & c[1], body, (jnp.int32(1), stick_left()))
        values(jnp.minimum(first - j_end + 2, nkb - 1))
        for h in heads:
            weights(h)
        values(first - j_end + 1)

    o_ref[0] = acc_ref[...].astype(o_ref.dtype)


def _sb_attention(proj3, carry_done, tq):
    b, s, _ = proj3.shape
    w = SB_GROUP_WIDTH
    groups = SB_WIDTH // w
    qoff, koff, voff = (COL_Q * PROJ_TILE // w, COL_K * PROJ_TILE // w, COL_V * PROJ_TILE // w)
    return pl.pallas_call(
        functools.partial(_sb_attn_kernel, tq=tq, nkb=s // tq),
        grid=(b, groups, s // tq),
        in_specs=[
            pl.BlockSpec(memory_space=pltpu.SMEM),
            pl.BlockSpec((1, tq, w), lambda bi, p, qi: (bi, qi, qoff + p)),
            pl.BlockSpec((1, s, w), lambda bi, p, qi: (bi, 0, koff + p)),
            pl.BlockSpec((1, s, w), lambda bi, p, qi: (bi, 0, voff + p)),
        ],
        out_specs=pl.BlockSpec((1, tq, w), lambda bi, p, qi: (bi, qi, p)),
        out_shape=jax.ShapeDtypeStruct((b, s, SB_WIDTH), BF16),
        scratch_shapes=[pltpu.VMEM((s // tq, SB_GROUP * tq, w), BF16),
                        pltpu.VMEM((tq, w), F32), pltpu.VMEM((SB_GROUP, tq, LANES), F32),
                        pltpu.VMEM((SB_GROUP, tq, LANES), F32), pltpu.VMEM((SB_GROUP, tq, tq), F32),
                        pltpu.VMEM((SB_GROUP, tq, tq), F32), pltpu.VMEM((tq, SB_GROUP * tq), BF16)],
        compiler_params=_params(("parallel", "parallel", "arbitrary")),
        name="sb_attention",
    )(carry_done, proj3, proj3, proj3)


def _mem_kv_kernel(m_ref, g_ref, w_ref, gk_ref, k_ref, v_ref):
    mn = _rms(m_ref[0], g_ref[...]).astype(BF16)
    kv = jnp.dot(mn, w_ref[...], preferred_element_type=F32)
    k = kv[:, :X_WIDTH]
    ms = jnp.dot((k * k).astype(BF16), _group_mean_matrix(X_WIDTH, X_HEAD_DIM), preferred_element_type=F32)
    k_ref[0] = (k * lax.rsqrt(ms + EPS) * gk_ref[...]).astype(BF16)
    v_ref[0] = kv[:, X_WIDTH:].astype(BF16)


def _mem_kv(mem, g_mem, w_kv, gk_cols):
    b, m, d = mem.shape
    return pl.pallas_call(
        _mem_kv_kernel,
        grid=(b,),
        in_specs=[
            pl.BlockSpec((1, m, d), lambda bi: (bi, 0, 0)),
            pl.BlockSpec((1, d), lambda bi: (0, 0)),
            pl.BlockSpec((d, 2 * X_WIDTH), lambda bi: (0, 0)),
            pl.BlockSpec((1, X_WIDTH), lambda bi: (0, 0)),
        ],
        out_specs=[pl.BlockSpec((1, m, X_WIDTH), lambda bi: (bi, 0, 0))] * 2,
        out_shape=[jax.ShapeDtypeStruct((b, m, X_WIDTH), BF16)] * 2,
        compiler_params=_params(("parallel",)),
        name="mem_kv",
    )(mem, g_mem, w_kv, gk_cols)


def _split_bf16(v):
    hi = v.astype(BF16)
    return hi, (v - hi.astype(F32)).astype(BF16)


def _router(logits):
    lane = lax.broadcasted_iota(jnp.int32, logits.shape, 1).astype(F32)
    ninf = -jnp.inf
    far = float(LANES)
    is_group = (lane >= N_EXPERTS) & (lane < N_EXPERTS + N_GROUPS)
    gl = jnp.where(is_group, logits, ninf)
    gmax = jnp.max(gl, axis=-1, keepdims=True)
    gidx = jnp.min(jnp.where(gl == gmax, lane, far), axis=-1, keepdims=True) - N_EXPERTS
    g_prob = 1.0 / jnp.sum(jnp.exp(gl - gmax), axis=-1, keepdims=True)
    first = gidx * EXPERTS_PER_GROUP
    el = jnp.where((lane >= first) & (lane < first + EXPERTS_PER_GROUP), logits, ninf)
    m1 = jnp.max(el, axis=-1, keepdims=True)
    i1 = jnp.min(jnp.where(el == m1, lane, far), axis=-1, keepdims=True)
    el2 = jnp.where(lane == i1, ninf, el)
    m2 = jnp.max(el2, axis=-1, keepdims=True)
    i2 = jnp.min(jnp.where(el2 == m2, lane, far), axis=-1, keepdims=True)
    e2 = jnp.exp(m2 - m1)
    w1 = 1.0 / (1.0 + e2)
    w2 = e2 / (1.0 + e2)
    combine = g_prob * (jnp.where(lane == i1, w1, 0.0) + jnp.where(lane == i2, w2, 0.0))
    return combine + jnp.where(lane == MOE_GID_LANE, gidx, 0.0)


def _merge_kernel(qx_ref, g0_ref, g1_ref, g2_ref, osb_ref, olru_ref, kx_ref, vx_ref, x_ref,
                  wb_ref, wo_ref, gf_ref, wr_ref, br_ref, x1_ref, hn_ref, comb_ref):
    qx = qx_ref[0]
    heads = []
    for h in range(X_HEADS):
        sl = slice(h * X_HEAD_DIM, (h + 1) * X_HEAD_DIM)
        s = lax.dot_general(qx[:, sl], kx_ref[0][:, sl], (((1,), (1,)), ((), ())),
                            preferred_element_type=F32)
        p = jnp.exp(s - jnp.max(s, axis=-1, keepdims=True))
        p = p / jnp.sum(p, axis=-1, keepdims=True)
        heads.append(jnp.dot(p.astype(BF16), vx_ref[0][:, sl], preferred_element_type=F32))
    o_x = jnp.concatenate(heads, axis=1).astype(BF16)

    merged = g0_ref[0].astype(F32) * jnp.dot(osb_ref[0], wb_ref[0], preferred_element_type=F32)
    merged += g1_ref[0].astype(F32) * jnp.dot(olru_ref[0], wb_ref[1], preferred_element_type=F32)
    merged += g2_ref[0].astype(F32) * jnp.dot(o_x, wb_ref[2], preferred_element_type=F32)
    x1 = x_ref[0] + jnp.dot(merged.astype(BF16), wo_ref[...], preferred_element_type=F32)
    x1_ref[0] = x1

    hn = _rms(x1, gf_ref[...])
    hn_ref[0] = hn.astype(BF16)
    h_hi, h_lo = _split_bf16(hn)
    w_hi, w_lo = _split_bf16(wr_ref[...])
    both = jnp.dot(h_hi, jnp.concatenate([w_hi, w_lo], axis=1), preferred_element_type=F32)
    logits = both[:, :LANES] + both[:, LANES:] + jnp.dot(h_lo, w_hi, preferred_element_type=F32) + br_ref[...]
    comb_ref[0] = _router(logits)


def _merge(proj3, o_sb, o_lru, kx, vx, x, w_branch, w_out, g_ffn, w_router, b_router, tm):
    b, s, d = x.shape
    m = kx.shape[1]
    gate_blk = lambda n: pl.BlockSpec((1, tm, d), lambda bi, si, n=n: (bi, si, COL_GATE * PROJ_TILE // d + n))
    tok512 = lambda: pl.BlockSpec((1, tm, SB_WIDTH), lambda bi, si: (bi, si, 0))
    const2 = lambda shape: pl.BlockSpec(shape, lambda bi, si: (0, 0))
    return pl.pallas_call(
        _merge_kernel,
        grid=(b, s // tm),
        in_specs=[
            pl.BlockSpec((1, tm, X_WIDTH), lambda bi, si: (bi, si, COL_QX)),
            gate_blk(0), gate_blk(1), gate_blk(2),
            tok512(), tok512(),
            pl.BlockSpec((1, m, X_WIDTH), lambda bi, si: (bi, 0, 0)),
            pl.BlockSpec((1, m, X_WIDTH), lambda bi, si: (bi, 0, 0)),
            pl.BlockSpec((1, tm, d), lambda bi, si: (bi, si, 0)),
            pl.BlockSpec((N_BRANCH, SB_WIDTH, d), lambda bi, si: (0, 0, 0)),
            const2((d, d)), const2((1, d)), const2((d, LANES)), const2((1, LANES)),
        ],
        out_specs=[
            pl.BlockSpec((1, tm, d), lambda bi, si: (bi, si, 0)),
            pl.BlockSpec((1, tm, d), lambda bi, si: (bi, si, 0)),
            pl.BlockSpec((1, tm, LANES), lambda bi, si: (bi, si, 0)),
        ],
        out_shape=[
            jax.ShapeDtypeStruct((b, s, d), F32),
            jax.ShapeDtypeStruct((b, s, d), BF16),
            jax.ShapeDtypeStruct((b, s, LANES), F32),
        ],
        compiler_params=_params(("parallel", "parallel")),
        name="merge_router",
    )(proj3, proj3, proj3, proj3, o_sb, o_lru, kx, vx, x, w_branch, w_out, g_ffn, w_router, b_router)


def _moe_sort(hn_ref, comb_ref, xs_ref, cs_ref, ys_ref, pt_ref, seg_ref, *, tt, rows, sub):
    comb = comb_ref[...]
    comb_t = comb.T
    gid_row = comb_t[MOE_GID_LANE:MOE_GID_LANE + 1, :]
    group_of_row = lax.broadcasted_iota(jnp.int32, (SUBLANES, tt), 0).astype(F32)
    member = jnp.where(group_of_row == gid_row, 1.0, 0.0)
    r = lax.broadcasted_iota(jnp.int32, (MXU_TILE, MXU_TILE), 0)
    c = lax.broadcasted_iota(jnp.int32, (MXU_TILE, MXU_TILE), 1)
    upto = jnp.where(r <= c, 1.0, 0.0).astype(BF16)
    before = jnp.zeros((SUBLANES, 1), F32)
    ranks = []
    for blk in range(tt // MXU_TILE):
        m = member[:, blk * MXU_TILE:(blk + 1) * MXU_TILE]
        incl = jnp.dot(m.astype(BF16), upto, preferred_element_type=F32) + before
        ranks.append(jnp.sum(m * (incl - m), axis=0, keepdims=True))
        before = before + jnp.sum(m, axis=1, keepdims=True)
    pos_row = jnp.concatenate(ranks, axis=1)
    start = jnp.int32(0)
    for g in range(N_GROUPS):
        count = jnp.sum(member[g:g + 1, :]).astype(jnp.int32)
        padded = lax.shift_left(lax.shift_right_logical(count + (MOE_PAD - 1), MOE_PAD_LOG2), MOE_PAD_LOG2)
        seg_ref[g] = start
        seg_ref[N_GROUPS + g] = padded
        pos_row = pos_row + member[g:g + 1, :] * start.astype(F32)
        start = start + padded

    hi_lo = jnp.concatenate(_split_bf16(comb), axis=1)
    hn = hn_ref[...]
    for blk in range(rows // LANES):
        rr = (lax.broadcasted_iota(jnp.int32, (LANES, tt), 0) + blk * LANES).astype(F32)
        p = jnp.where(rr == pos_row, 1.0, 0.0).astype(BF16)
        sl = pl.ds(blk * LANES, LANES)
        xs_ref[sl, :] = jnp.dot(p, hn, preferred_element_type=F32).astype(BF16)
        hl = jnp.dot(p, hi_lo, preferred_element_type=F32)
        cs_ref[sl, :] = hl[:, :LANES] + hl[:, LANES:]
    tail = pl.ds(rows, sub)
    xs_ref[tail, :] = jnp.zeros((sub, xs_ref.shape[1]), BF16)
    cs_ref[tail, :] = jnp.zeros((sub, LANES), F32)
    ys_ref[...] = jnp.zeros_like(ys_ref)

    sub_t = lax.broadcasted_iota(jnp.int32, (LANES, tt), 0)
    pos_col = jnp.where(sub_t == MOE_POS_LANE, pos_row, comb_t).T[:, MOE_POS_LANE:MOE_POS_LANE + 1]
    for blk in range(tt // LANES):
        cc = lax.broadcasted_iota(jnp.int32, (LANES, rows), 1).astype(F32)
        sl = pl.ds(blk * LANES, LANES)
        pt_ref[sl, :] = jnp.where(cc == pos_col[blk * LANES:(blk + 1) * LANES, :], 1.0, 0.0).astype(BF16)


def _moe_kernel(hn_ref, comb_ref, x1_ref, wgu_ref, wd_ref, o_ref, xs_ref, cs_ref, ys_ref, pt_ref, seg_ref,
                *, tt, rows, sub):
    c = pl.program_id(1)

    @pl.when(c == 0)
    def _():
        _moe_sort(hn_ref, comb_ref, xs_ref, cs_ref, ys_ref, pt_ref, seg_ref, tt=tt, rows=rows, sub=sub)

    group = lax.shift_right_logical(c, (EXPERTS_PER_GROUP // MOE_CHUNK).bit_length() - 1)
    start = seg_ref[group]
    padded = seg_ref[N_GROUPS + group]
    n_sub = sum((padded > k * sub).astype(jnp.int32) for k in range(-(-rows // sub)))
    lane = lax.broadcasted_iota(jnp.int32, (sub, LANES), 1)

    def sub_tile(s, carry):
        sl = pl.ds(pl.multiple_of(start + s * sub, MOE_PAD), sub)
        x = xs_ref[sl, :]
        cw = cs_ref[sl, :]
        acc = ys_ref[sl, :]
        for k in range(MOE_CHUNK):
            gu = jnp.dot(x, wgu_ref[k], preferred_element_type=F32)
            gate, up = gu[:, :EXPERT_FF], gu[:, EXPERT_FF:]
            weight = jnp.sum(jnp.where(lane == c * MOE_CHUNK + k, cw, 0.0), axis=-1, keepdims=True)
            act = jax.nn.silu(gate) * up * weight
            acc = acc + jnp.dot(act.astype(BF16), wd_ref[k], preferred_element_type=F32)
        ys_ref[sl, :] = acc
        return carry

    lax.fori_loop(0, n_sub, sub_tile, 0)

    @pl.when(c == pl.num_programs(1) - 1)
    def _():
        y = ys_ref[pl.ds(0, rows), :].astype(BF16)
        o_ref[...] = x1_ref[...] + jnp.dot(pt_ref[...], y, preferred_element_type=F32)


def _moe(hn2, comb2, x1_2, w_gu, w_down, tt):
    t, d = hn2.shape
    rows = -(-(tt + N_GROUPS * MOE_PAD) // LANES) * LANES
    spread = 3.5 * (tt * (N_GROUPS - 1)) ** 0.5 / N_GROUPS
    sub = min(rows, -(-int(tt / N_GROUPS + spread) // MOE_PAD) * MOE_PAD)
    return pl.pallas_call(
        functools.partial(_moe_kernel, tt=tt, rows=rows, sub=sub),
        grid=(t // tt, N_EXPERTS // MOE_CHUNK),
        in_specs=[
            pl.BlockSpec((tt, d), lambda i, c: (i, 0)),
            pl.BlockSpec((tt, LANES), lambda i, c: (i, 0)),
            pl.BlockSpec((tt, d), lambda i, c: (i, 0)),
            pl.BlockSpec((MOE_CHUNK, d, 2 * EXPERT_FF), lambda i, c: (c, 0, 0)),
            pl.BlockSpec((MOE_CHUNK, EXPERT_FF, d), lambda i, c: (c, 0, 0)),
        ],
        out_specs=pl.BlockSpec((tt, d), lambda i, c: (i, 0)),
        out_shape=jax.ShapeDtypeStruct((t, d), F32),
        scratch_shapes=[pltpu.VMEM((rows + sub, d), BF16), pltpu.VMEM((rows + sub, LANES), F32),
                        pltpu.VMEM((rows + sub, d), F32), pltpu.VMEM((tt, rows), BF16),
                        pltpu.SMEM((2 * N_GROUPS,), jnp.int32)],
        compiler_params=_params(("parallel", "arbitrary")),
        name="moe",
    )(hn2, comb2, x1_2, w_gu, w_down)


def _block_diag(w):
    n, bd, _ = w.shape
    eye = jnp.eye(n, dtype=w.dtype)
    return jnp.einsum("nij,nm->nimj", w, eye).reshape(n * bd, n * bd)


def _tile(n, pref):
    while n % pref:
        pref //= 2
    return pref


def _layer(x, mem, g_mix, w_in, g_q_sb, g_k_sb, conv_w, conv_b, lru_w_a, lru_b_a, lru_w_i, lru_b_i,
           lru_lambda, g_mem, w_mem_kv, g_q_x, g_k_x, w_branch, w_out, g_ffn, w_group, b_group,
           w_expert, b_expert, w_gate, w_up, w_down):
    b, s, d = x.shape
    t = b * s
    row = lambda v: v.reshape(1, -1).astype(F32)

    ones = jnp.ones((PROJ_TILE,), F32)
    col_gain = jnp.concatenate([
        jnp.tile(g_q_sb, SB_WIDTH // SB_HEAD_DIM) * (SB_HEAD_DIM ** -0.5 * LOG2E),
        jnp.tile(g_k_sb, SB_WIDTH // SB_HEAD_DIM),
        ones, ones, ones,
        jnp.tile(g_q_x, X_HEADS) * X_HEAD_DIM ** -0.5,
    ] + [ones] * 6).reshape(1, -1)
    w_router = jnp.zeros((d, LANES), F32).at[:, :N_EXPERTS].set(w_expert)
    w_router = w_router.at[:, N_EXPERTS:N_EXPERTS + N_GROUPS].set(w_group)
    b_router = jnp.zeros((1, LANES), F32).at[0, :N_EXPERTS].set(b_expert)
    b_router = b_router.at[0, N_EXPERTS:N_EXPERTS + N_GROUPS].set(b_group)
    w_gu = jnp.concatenate([w_gate, w_up], axis=-1).reshape(N_EXPERTS, d, 2 * EXPERT_FF).astype(BF16)
    w_dn = w_down.reshape(N_EXPERTS, EXPERT_FF, d).astype(BF16)

    proj, o_lru = _in_proj(x.reshape(t, d), row(g_mix), w_in.astype(BF16), col_gain, conv_w, row(conv_b),
                           _block_diag(lru_w_a).astype(BF16), row(lru_b_a), _block_diag(lru_w_i).astype(BF16),
                           row(lru_b_i), row(lru_lambda), _tile(s, 512), s)
    proj3 = proj.reshape(b, s, -1)
    o_lru = o_lru.reshape(b, s, LRU_WIDTH)
    q_gain = jnp.abs(col_gain[0, :PROJ_TILE])
    z_max = 1.05 * SB_HEAD_DIM * jnp.max(q_gain) * jnp.max(jnp.abs(g_k_sb))
    carry_done = (z_max + BF16_ZERO_EXP).reshape(1, 1).astype(F32)
    o_sb = _sb_attention(proj3, carry_done, _tile(s, 256))
    kx, vx = _mem_kv(mem, row(g_mem), w_mem_kv.astype(BF16), row(jnp.tile(g_k_x, X_HEADS)))
    x1, hn, comb = _merge(proj3, o_sb, o_lru, kx, vx, x, w_branch.astype(BF16), w_out.astype(BF16),
                          row(g_ffn), w_router, b_router, _tile(s, 512))
    out = _moe(hn.reshape(t, d), comb.reshape(t, LANES), x1.reshape(t, d), w_gu, w_dn, _tile(t, 1024))
    return out.reshape(b, s, d)


def kernel(x, mem, g_mix, w_in, g_q_sb, g_k_sb, conv_w, conv_b, lru_w_a, lru_b_a, lru_w_i, lru_b_i, lru_lambda, g_mem, w_mem_kv, g_q_x, g_k_x, w_branch, w_out, g_ffn, w_group, b_group, w_expert, b_expert, w_gate, w_up, w_down):
    params = (g_mix, w_in, g_q_sb, g_k_sb, conv_w, conv_b, lru_w_a, lru_b_a, lru_w_i, lru_b_i, lru_lambda,
              g_mem, w_mem_kv, g_q_x, g_k_x, w_branch, w_out, g_ffn, w_group, b_group, w_expert, b_expert,
              w_gate, w_up, w_down)
    for layer in range(g_mix.shape[0]):
        x = _layer(x, mem, *[p[layer] for p in params])
    return x
```

```python
import functools

import jax
import jax.numpy as jnp
from jax import lax
from jax.experimental import pallas as pl
from jax.experimental.pallas import tpu as pltpu

F32 = jnp.float32
BF16 = jnp.bfloat16

EPS = 1e-6
SB_HEAD_DIM = 64
SB_WIDTH = 512
LRU_WIDTH = 512
LRU_BLOCKS = 8
CONV_WIDTH = 4
LRU_C = 8.0
X_HEADS = 4
X_WIDTH = 512
X_HEAD_DIM = 128
N_BRANCH = 3
N_GROUPS = 4
EXPERTS_PER_GROUP = 8
N_EXPERTS = N_GROUPS * EXPERTS_PER_GROUP
EXPERT_FF = 256

LOG2E = 1.4426950408889634
SB_GROUP = 4
SB_GROUP_WIDTH = SB_GROUP * SB_HEAD_DIM
SB_QBLOCKS = 2
BF16_ZERO_EXP = 160.0

MOE_CHUNK = 4
MXU_TILE = 256
MOE_PAD_LOG2 = 4
MOE_PAD = 1 << MOE_PAD_LOG2
MOE_GID_LANE = N_EXPERTS
MOE_POS_LANE = N_EXPERTS + 1
SUBLANES = 8

LANES = 128
VMEM_LIMIT = 56 * 1024 * 1024

PROJ_TILE = 512
KIND_HEAD64, KIND_HEAD128, KIND_PLAIN, KIND_SIGMOID, KIND_LRU = 0, 1, 2, 3, 4
PROJ_KINDS = (KIND_HEAD64, KIND_HEAD64, KIND_PLAIN, KIND_LRU, KIND_LRU, KIND_HEAD128) + (KIND_SIGMOID,) * 6
W_COL_XL, W_COL_YL = 3, 4
PROJ_OUT_COL = {j: sum(k != KIND_LRU for k in PROJ_KINDS[:j]) for j, kind in enumerate(PROJ_KINDS) if kind != KIND_LRU}
COL_Q, COL_K, COL_V, COL_QX, COL_GATE = (PROJ_OUT_COL[j] for j in (0, 1, 2, 5, 6))


def _params(sem):
    return pltpu.CompilerParams(dimension_semantics=sem, vmem_limit_bytes=VMEM_LIMIT)


def _rms(xf, g):
    return xf * lax.rsqrt(jnp.mean(xf * xf, axis=-1, keepdims=True) + EPS) * g


def _sigmoid(v):
    return 0.5 * jnp.tanh(0.5 * v) + 0.5


def _group_mean_matrix(n, group):
    shift = group.bit_length() - 1
    r = lax.shift_right_logical(lax.broadcasted_iota(jnp.int32, (n, n), 0), shift)
    c = lax.shift_right_logical(lax.broadcasted_iota(jnp.int32, (n, n), 1), shift)
    return jnp.where(r == c, 1.0 / group, 0.0).astype(BF16)


def _proj_kernel(x_ref, g_ref, w_ref, cg_ref, cw_ref, cb_ref, wa_ref, ba_ref, wi_ref, bi_ref, lam_ref,
                 o_ref, olru_ref, xbuf_ref, h_ref, *, tm, tiles_per_seq):
    seq_tile = pl.program_id(0) % tiles_per_seq

    @pl.when(seq_tile == 0)
    def _():
        xbuf_ref[pl.ds(0, SUBLANES), :] = jnp.zeros((SUBLANES, LRU_WIDTH), F32)
        h_ref[...] = jnp.zeros_like(h_ref)

    @pl.when(seq_tile > 0)
    def _():
        xbuf_ref[pl.ds(0, SUBLANES), :] = xbuf_ref[pl.ds(tm, SUBLANES), :]

    hn = _rms(x_ref[...], g_ref[...]).astype(BF16)
    group_mean = {KIND_HEAD64: _group_mean_matrix(PROJ_TILE, SB_HEAD_DIM),
                  KIND_HEAD128: _group_mean_matrix(PROJ_TILE, X_HEAD_DIM)}

    def column_tile(j):
        return jnp.dot(hn, w_ref[:, pl.ds(j * PROJ_TILE, PROJ_TILE)], preferred_element_type=F32)

    def stored_tile(j):
        kind, acc = PROJ_KINDS[j], column_tile(j)
        if kind in group_mean:
            ms = jnp.dot((acc * acc).astype(BF16), group_mean[kind], preferred_element_type=F32)
            acc = acc * lax.rsqrt(ms + EPS) * cg_ref[:, pl.ds(j * PROJ_TILE, PROJ_TILE)]
        elif kind == KIND_SIGMOID:
            acc = _sigmoid(acc)
        o_ref[:, pl.ds(PROJ_OUT_COL[j] * PROJ_TILE, PROJ_TILE)] = acc.astype(o_ref.dtype)

    stored = [j for j, kind in enumerate(PROJ_KINDS) if kind != KIND_LRU]
    x_lru, y_lru = column_tile(W_COL_XL), column_tile(W_COL_YL)
    for j in stored[:3]:
        stored_tile(j)

    xbuf_ref[pl.ds(SUBLANES, tm), :] = x_lru
    xc = cb_ref[...] + jnp.zeros((tm, LRU_WIDTH), F32)
    for j in range(CONV_WIDTH):
        xc = xc + cw_ref[pl.ds(j, 1), :] * xbuf_ref[pl.ds(SUBLANES - (CONV_WIDTH - 1) + j, tm), :]
    xcb = xc.astype(BF16)
    r = _sigmoid(jnp.dot(xcb, wa_ref[...], preferred_element_type=F32) + ba_ref[...])
    gi = _sigmoid(jnp.dot(xcb, wi_ref[...], preferred_element_type=F32) + bi_ref[...])
    for j in stored[3:6]:
        stored_tile(j)

    nlam = -lam_ref[...]
    sp_nlam = jnp.maximum(nlam, 0.0) + jnp.log(1.0 + jnp.exp(-jnp.abs(nlam)))
    state = h_ref[pl.ds(0, 1), :]
    rest = stored[6:]
    chunk = tm // len(rest)
    for c, j in enumerate(rest):
        rows = slice(c * chunk, (c + 1) * chunk)
        log_a = (-LRU_C * r[rows]) * sp_nlam
        a = jnp.exp(log_a)
        th = jnp.tanh(log_a)
        n = -2.0 * th
        coef = jnp.where(n > 0.0, n * lax.rsqrt(n * (1.0 - th)), 0.0)
        b = coef * (gi[rows] * xc[rows])
        a = a.reshape(chunk // SUBLANES, SUBLANES, LRU_WIDTH)
        b = b.reshape(chunk // SUBLANES, SUBLANES, LRU_WIDTH)
        row_in_vreg = lax.broadcasted_iota(jnp.int32, a.shape, 1)
        d = 1
        while d < SUBLANES:
            keep = row_in_vreg >= d
            b = jnp.where(keep, a * pltpu.roll(b, d, 1), 0.0) + b
            a = jnp.where(keep, a * pltpu.roll(a, d, 1), a)
            d *= 2
        pieces = []
        for i in range(chunk // SUBLANES):
            pieces.append(b[i] + a[i] * state)
            state = pieces[-1][SUBLANES - 1:SUBLANES, :]
        h = jnp.concatenate(pieces, axis=0)
        olru_ref[pl.ds(c * chunk, chunk), :] = (h * jax.nn.gelu(y_lru[rows])).astype(olru_ref.dtype)
        stored_tile(j)
    h_ref[...] = jnp.broadcast_to(state, h_ref.shape)


def _in_proj(x2, g_mix, w_cat, col_gain, conv_w, conv_b, wa_bd, b_a, wi_bd, b_i, lam, tm, seq_len):
    t, d = x2.shape
    ncols = len([k for k in PROJ_KINDS if k != KIND_LRU]) * PROJ_TILE
    resident = lambda shape: pl.BlockSpec(shape, lambda i: (0, 0), pipeline_mode=pl.Buffered(1))
    vec = lambda: resident((1, LRU_WIDTH))
    mat = lambda: resident((LRU_WIDTH, LRU_WIDTH))
    return pl.pallas_call(
        functools.partial(_proj_kernel, tm=tm, tiles_per_seq=seq_len // tm),
        grid=(t // tm,),
        in_specs=[
            pl.BlockSpec((tm, d), lambda i: (i, 0)),
            resident((1, d)),
            resident((d, w_cat.shape[1])),
            resident((1, w_cat.shape[1])),
            resident((CONV_WIDTH, LRU_WIDTH)), vec(), mat(), vec(), mat(), vec(), vec(),
        ],
        out_specs=[pl.BlockSpec((tm, ncols), lambda i: (i, 0)),
                   pl.BlockSpec((tm, LRU_WIDTH), lambda i: (i, 0))],
        out_shape=[jax.ShapeDtypeStruct((t, ncols), BF16), jax.ShapeDtypeStruct((t, LRU_WIDTH), BF16)],
        scratch_shapes=[pltpu.VMEM((tm + SUBLANES, LRU_WIDTH), F32), pltpu.VMEM((SUBLANES, LRU_WIDTH), F32)],
        compiler_params=_params(("arbitrary",)),
        name="in_proj_rglru",
    )(x2, g_mix, w_cat, col_gain, conv_w, conv_b, wa_bd, b_a, wi_bd, b_i, lam)


def _sb_attn_kernel(done_ref, q_ref, k_ref, v_ref, o_ref, vexp_ref, acc_ref, carry_ref, cprev_ref, z_ref, d_ref, w_ref,
                    *, tq, nkb):
    step = pl.program_id(2)
    lane = lax.broadcasted_iota(jnp.int32, (tq, SB_GROUP_WIDTH), 1)
    head_lanes = [(lane >= h * SB_HEAD_DIM) & (lane < (h + 1) * SB_HEAD_DIM) for h in range(SB_GROUP)]
    keep_head = lambda m, a: jnp.where(m, a.astype(F32), 0.0).astype(BF16)

    @pl.when(step == 0)
    def _():
        def fill(j, c):
            v4 = v_ref[0, pl.ds(pl.multiple_of(j * tq, tq), tq), :]
            for h in range(SB_GROUP):
                vexp_ref[j, pl.ds(h * tq, tq), :] = keep_head(head_lanes[h], v4)
            return c
        lax.fori_loop(0, nkb, fill, 0)

    q_h = [[keep_head(m, q_ref[0, pl.ds(u * tq, tq), :]) for m in head_lanes] for u in range(SB_QBLOCKS)]
    qb = [step * SB_QBLOCKS + u for u in range(SB_QBLOCKS)]
    row = lax.broadcasted_iota(jnp.int32, (tq, tq), 0)
    col = lax.broadcasted_iota(jnp.int32, (tq, tq), 1)
    later_or_self = jnp.where(row >= col, 1.0, 0.0).astype(BF16)
    causal = col < row
    heads = range(SB_GROUP)
    sign_bit = jnp.uint32(0x80000000)

    def keys(kb):
        return k_ref[0, pl.ds(pl.multiple_of(kb * tq, tq), tq), :]

    def score(u, h, k4):
        return lax.dot_general(q_h[u][h], k4, (((1,), (1,)), ((), ())), preferred_element_type=F32)

    def softplus2(z):
        neg_abs = lax.bitcast_convert_type(lax.bitcast_convert_type(z, jnp.uint32) | sign_bit, F32)
        return jnp.maximum(z, 0.0) + jnp.log(1.0 + jnp.exp2(neg_abs)) * LOG2E

    def later_sums(sp):
        return jnp.dot(sp.astype(BF16), later_or_self, preferred_element_type=F32)

    has_left = [q > 0 for q in qb]
    left = [jnp.maximum(q - 1, 0) for q in qb]
    k_pair = [(keys(qb[u]), keys(left[u])) for u in range(SB_QBLOCKS)]
    per_q = 2 * SB_GROUP
    n_chain = SB_QBLOCKS * per_q
    z_pair, sp_pair, totals, d_pair, w_pair, acc_pair = {}, {}, {}, {}, {}, {}

    def pair_scores(c):
        u, i = divmod(c, per_q)
        z_pair[c] = score(u, i % SB_GROUP, k_pair[u][i // SB_GROUP])

    def pair_softplus(c):
        sp = softplus2(z_pair[c])
        sp_pair[c] = jnp.where(causal, sp, 0.0) if c % per_q < SB_GROUP else sp
        totals[c] = jnp.sum(sp_pair[c], axis=-1, keepdims=True)

    def pair_sums(c):
        d = z_pair[c] - later_sums(sp_pair[c])
        if c % per_q < SB_GROUP:
            d_pair[c] = jnp.where(causal, d, -jnp.inf)
        else:
            d_pair[c] = d - jnp.where(has_left[c // per_q], totals[c - SB_GROUP], jnp.inf)

    def pair_weights(c):
        w_pair[c] = jnp.exp2(d_pair[c]).astype(BF16)
        if c % SB_GROUP == SB_GROUP - 1:
            u, i = divmod(c, per_q)
            w = jnp.concatenate([w_pair[c - SB_GROUP + 1 + h] for h in heads], axis=1)
            kb = qb[u] if i < SB_GROUP else left[u]
            acc_pair[c] = jnp.dot(w, vexp_ref[kb], preferred_element_type=F32)

    stages = (pair_scores, pair_softplus, pair_sums, pair_weights)
    for t in range(n_chain + len(stages) - 1):
        for s, stage in enumerate(stages):
            if 0 <= t - s < n_chain:
                stage(t - s)
    for u in range(SB_QBLOCKS):
        base = u * per_q
        acc_ref[u] = acc_pair[base + SB_GROUP - 1] + acc_pair[base + per_q - 1]
        for h in heads:
            both = totals[base + h] + jnp.where(has_left[u], totals[base + SB_GROUP + h], 0.0)
            carry_ref[u, h] = jnp.broadcast_to(both, (tq, LANES))

    def stick_left(u):
        least = functools.reduce(jnp.minimum, [carry_ref[u, h] for h in heads])
        return jnp.min(least) < done_ref[0, 0]

    def sweep(u):
        def scores(h, k4):
            z_ref[h] = score(u, h, k4)

        def sums(h):
            z = z_ref[h]
            sp = softplus2(z)
            d_ref[h] = z - later_sums(sp)
            carry = carry_ref[u, h]
            cprev_ref[h] = carry
            carry_ref[u, h] = carry + jnp.sum(sp, axis=-1, keepdims=True)

        def weights(h):
            carry_t = jnp.concatenate([cprev_ref[h]] * (tq // LANES), axis=1)
            w_ref[:, pl.ds(h * tq, tq)] = jnp.exp2(d_ref[h] - carry_t).astype(BF16)

        def values(kb):
            acc_ref[u] += jnp.dot(w_ref[...], vexp_ref[kb], preferred_element_type=F32)

        first = qb[u] - 2
        w_ref[...] = jnp.zeros_like(w_ref)
        k_first = keys(first)
        for h in heads:
            scores(h, k_first)
        k_next = keys(jnp.maximum(first - 1, 0))
        for h in heads:
            sums(h)
            scores(h, k_next)

        def body(c):
            j, _ = c
            values(jnp.minimum(first - j + 2, nkb - 1))
            for h in heads:
                weights(h)
            k_next = keys(jnp.maximum(first - j - 1, 0))
            for h in heads:
                sums(h)
                scores(h, k_next)
            return j + 1, stick_left(u)

        j_end, _ = lax.while_loop(lambda c: (c[0] <= first) & c[1], body, (jnp.int32(1), stick_left(u)))
        values(jnp.minimum(first - j_end + 2, nkb - 1))
        for h in heads:
            weights(h)
        values(first - j_end + 1)

    for u in range(SB_QBLOCKS):
        pl.when((qb[u] >= 2) & stick_left(u))(functools.partial(sweep, u))
        o_ref[0, pl.ds(u * tq, tq), :] = acc_ref[u].astype(o_ref.dtype)


def _sb_attention(proj3, carry_done, tq):
    b, s, _ = proj3.shape
    w = SB_GROUP_WIDTH
    groups = SB_WIDTH // w
    qoff, koff, voff = (COL_Q * PROJ_TILE // w, COL_K * PROJ_TILE // w, COL_V * PROJ_TILE // w)
    tqs = SB_QBLOCKS * tq
    return pl.pallas_call(
        functools.partial(_sb_attn_kernel, tq=tq, nkb=s // tq),
        grid=(b, groups, s // tqs),
        in_specs=[
            pl.BlockSpec(memory_space=pltpu.SMEM),
            pl.BlockSpec((1, tqs, w), lambda bi, p, qi: (bi, qi, qoff + p)),
            pl.BlockSpec((1, s, w), lambda bi, p, qi: (bi, 0, koff + p)),
            pl.BlockSpec((1, s, w), lambda bi, p, qi: (bi, 0, voff + p)),
        ],
        out_specs=pl.BlockSpec((1, tqs, w), lambda bi, p, qi: (bi, qi, p)),
        out_shape=jax.ShapeDtypeStruct((b, s, SB_WIDTH), BF16),
        scratch_shapes=[pltpu.VMEM((s // tq, SB_GROUP * tq, w), BF16),
                        pltpu.VMEM((SB_QBLOCKS, tq, w), F32), pltpu.VMEM((SB_QBLOCKS, SB_GROUP, tq, LANES), F32),
                        pltpu.VMEM((SB_GROUP, tq, LANES), F32), pltpu.VMEM((SB_GROUP, tq, tq), F32),
                        pltpu.VMEM((SB_GROUP, tq, tq), F32), pltpu.VMEM((tq, SB_GROUP * tq), BF16)],
        compiler_params=_params(("parallel", "parallel", "arbitrary")),
        name="sb_attention",
    )(carry_done, proj3, proj3, proj3)


def _mem_kv_kernel(m_ref, g_ref, w_ref, gk_ref, k_ref, v_ref):
    mn = _rms(m_ref[0], g_ref[...]).astype(BF16)
    kv = jnp.dot(mn, w_ref[...], preferred_element_type=F32)
    k = kv[:, :X_WIDTH]
    ms = jnp.dot((k * k).astype(BF16), _group_mean_matrix(X_WIDTH, X_HEAD_DIM), preferred_element_type=F32)
    k_ref[0] = (k * lax.rsqrt(ms + EPS) * gk_ref[...]).astype(BF16)
    v_ref[0] = kv[:, X_WIDTH:].astype(BF16)


def _mem_kv(mem, g_mem, w_kv, gk_cols):
    b, m, d = mem.shape
    return pl.pallas_call(
        _mem_kv_kernel,
        grid=(b,),
        in_specs=[
            pl.BlockSpec((1, m, d), lambda bi: (bi, 0, 0)),
            pl.BlockSpec((1, d), lambda bi: (0, 0)),
            pl.BlockSpec((d, 2 * X_WIDTH), lambda bi: (0, 0)),
            pl.BlockSpec((1, X_WIDTH), lambda bi: (0, 0)),
        ],
        out_specs=[pl.BlockSpec((1, m, X_WIDTH), lambda bi: (bi, 0, 0))] * 2,
        out_shape=[jax.ShapeDtypeStruct((b, m, X_WIDTH), BF16)] * 2,
        compiler_params=_params(("parallel",)),
        name="mem_kv",
    )(mem, g_mem, w_kv, gk_cols)


def _split_bf16(v):
    hi = v.astype(BF16)
    return hi, (v - hi.astype(F32)).astype(BF16)


def _router(logits):
    lane = lax.broadcasted_iota(jnp.int32, logits.shape, 1).astype(F32)
    ninf = -jnp.inf
    far = float(LANES)
    is_group = (lane >= N_EXPERTS) & (lane < N_EXPERTS + N_GROUPS)
    gl = jnp.where(is_group, logits, ninf)
    gmax = jnp.max(gl, axis=-1, keepdims=True)
    gidx = jnp.min(jnp.where(gl == gmax, lane, far), axis=-1, keepdims=True) - N_EXPERTS
    g_prob = 1.0 / jnp.sum(jnp.exp(gl - gmax), axis=-1, keepdims=True)
    first = gidx * EXPERTS_PER_GROUP
    el = jnp.where((lane >= first) & (lane < first + EXPERTS_PER_GROUP), logits, ninf)
    m1 = jnp.max(el, axis=-1, keepdims=True)
    i1 = jnp.min(jnp.where(el == m1, lane, far), axis=-1, keepdims=True)
    el2 = jnp.where(lane == i1, ninf, el)
    m2 = jnp.max(el2, axis=-1, keepdims=True)
    i2 = jnp.min(jnp.where(el2 == m2, lane, far), axis=-1, keepdims=True)
    e2 = jnp.exp(m2 - m1)
    w1 = 1.0 / (1.0 + e2)
    w2 = e2 / (1.0 + e2)
    combine = g_prob * (jnp.where(lane == i1, w1, 0.0) + jnp.where(lane == i2, w2, 0.0))
    return combine + jnp.where(lane == MOE_GID_LANE, gidx, 0.0)


def _merge_kernel(qx_ref, g0_ref, g1_ref, g2_ref, osb_ref, olru_ref, kx_ref, vx_ref, x_ref,
                  wb_ref, wo_ref, gf_ref, wr_ref, br_ref, x1_ref, hn_ref, comb_ref):
    qx = qx_ref[0]
    heads = []
    for h in range(X_HEADS):
        sl = slice(h * X_HEAD_DIM, (h + 1) * X_HEAD_DIM)
        s = lax.dot_general(qx[:, sl], kx_ref[0][:, sl], (((1,), (1,)), ((), ())),
                            preferred_element_type=F32)
        p = jnp.exp(s - jnp.max(s, axis=-1, keepdims=True))
        p = p / jnp.sum(p, axis=-1, keepdims=True)
        heads.append(jnp.dot(p.astype(BF16), vx_ref[0][:, sl], preferred_element_type=F32))
    o_x = jnp.concatenate(heads, axis=1).astype(BF16)

    merged = g0_ref[0].astype(F32) * jnp.dot(osb_ref[0], wb_ref[0], preferred_element_type=F32)
    merged += g1_ref[0].astype(F32) * jnp.dot(olru_ref[0], wb_ref[1], preferred_element_type=F32)
    merged += g2_ref[0].astype(F32) * jnp.dot(o_x, wb_ref[2], preferred_element_type=F32)
    x1 = x_ref[0] + jnp.dot(merged.astype(BF16), wo_ref[...], preferred_element_type=F32)
    x1_ref[0] = x1

    hn = _rms(x1, gf_ref[...])
    hn_ref[0] = hn.astype(BF16)
    h_hi, h_lo = _split_bf16(hn)
    w_hi, w_lo = _split_bf16(wr_ref[...])
    both = jnp.dot(h_hi, jnp.concatenate([w_hi, w_lo], axis=1), preferred_element_type=F32)
    logits = both[:, :LANES] + both[:, LANES:] + jnp.dot(h_lo, w_hi, preferred_element_type=F32) + br_ref[...]
    comb_ref[0] = _router(logits)


def _merge(proj3, o_sb, o_lru, kx, vx, x, w_branch, w_out, g_ffn, w_router, b_router, tm):
    b, s, d = x.shape
    m = kx.shape[1]
    gate_blk = lambda n: pl.BlockSpec((1, tm, d), lambda bi, si, n=n: (bi, si, COL_GATE * PROJ_TILE // d + n))
    tok512 = lambda: pl.BlockSpec((1, tm, SB_WIDTH), lambda bi, si: (bi, si, 0))
    const2 = lambda shape: pl.BlockSpec(shape, lambda bi, si: (0, 0))
    return pl.pallas_call(
        _merge_kernel,
        grid=(b, s // tm),
        in_specs=[
            pl.BlockSpec((1, tm, X_WIDTH), lambda bi, si: (bi, si, COL_QX)),
            gate_blk(0), gate_blk(1), gate_blk(2),
            tok512(), tok512(),
            pl.BlockSpec((1, m, X_WIDTH), lambda bi, si: (bi, 0, 0)),
            pl.BlockSpec((1, m, X_WIDTH), lambda bi, si: (bi, 0, 0)),
            pl.BlockSpec((1, tm, d), lambda bi, si: (bi, si, 0)),
            pl.BlockSpec((N_BRANCH, SB_WIDTH, d), lambda bi, si: (0, 0, 0)),
            const2((d, d)), const2((1, d)), const2((d, LANES)), const2((1, LANES)),
        ],
        out_specs=[
            pl.BlockSpec((1, tm, d), lambda bi, si: (bi, si, 0)),
            pl.BlockSpec((1, tm, d), lambda bi, si: (bi, si, 0)),
            pl.BlockSpec((1, tm, LANES), lambda bi, si: (bi, si, 0)),
        ],
        out_shape=[
            jax.ShapeDtypeStruct((b, s, d), F32),
            jax.ShapeDtypeStruct((b, s, d), BF16),
            jax.ShapeDtypeStruct((b, s, LANES), F32),
        ],
        compiler_params=_params(("parallel", "parallel")),
        name="merge_router",
    )(proj3, proj3, proj3, proj3, o_sb, o_lru, kx, vx, x, w_branch, w_out, g_ffn, w_router, b_router)


def _moe_sort(hn_ref, comb_ref, xs_ref, cs_ref, ys_ref, pt_ref, seg_ref, *, tt, rows, sub):
    comb = comb_ref[...]
    comb_t = comb.T
    gid_row = comb_t[MOE_GID_LANE:MOE_GID_LANE + 1, :]
    group_of_row = lax.broadcasted_iota(jnp.int32, (SUBLANES, tt), 0).astype(F32)
    member = jnp.where(group_of_row == gid_row, 1.0, 0.0)
    r = lax.broadcasted_iota(jnp.int32, (MXU_TILE, MXU_TILE), 0)
    c = lax.broadcasted_iota(jnp.int32, (MXU_TILE, MXU_TILE), 1)
    upto = jnp.where(r <= c, 1.0, 0.0).astype(BF16)
    before = jnp.zeros((SUBLANES, 1), F32)
    ranks = []
    for blk in range(tt // MXU_TILE):
        m = member[:, blk * MXU_TILE:(blk + 1) * MXU_TILE]
        incl = jnp.dot(m.astype(BF16), upto, preferred_element_type=F32) + before
        ranks.append(jnp.sum(m * (incl - m), axis=0, keepdims=True))
        before = before + jnp.sum(m, axis=1, keepdims=True)
    pos_row = jnp.concatenate(ranks, axis=1)
    start = jnp.int32(0)
    for g in range(N_GROUPS):
        count = jnp.sum(member[g:g + 1, :]).astype(jnp.int32)
        padded = lax.shift_left(lax.shift_right_logical(count + (MOE_PAD - 1), MOE_PAD_LOG2), MOE_PAD_LOG2)
        seg_ref[g] = start
        seg_ref[N_GROUPS + g] = padded
        pos_row = pos_row + member[g:g + 1, :] * start.astype(F32)
        start = start + padded

    hi_lo = jnp.concatenate(_split_bf16(comb), axis=1)
    hn = hn_ref[...]
    for blk in range(rows // LANES):
        rr = (lax.broadcasted_iota(jnp.int32, (LANES, tt), 0) + blk * LANES).astype(F32)
        p = jnp.where(rr == pos_row, 1.0, 0.0).astype(BF16)
        sl = pl.ds(blk * LANES, LANES)
        xs_ref[sl, :] = jnp.dot(p, hn, preferred_element_type=F32).astype(BF16)
        hl = jnp.dot(p, hi_lo, preferred_element_type=F32)
        cs_ref[sl, :] = hl[:, :LANES] + hl[:, LANES:]
    tail = pl.ds(rows, sub)
    xs_ref[tail, :] = jnp.zeros((sub, xs_ref.shape[1]), BF16)
    cs_ref[tail, :] = jnp.zeros((sub, LANES), F32)
    ys_ref[...] = jnp.zeros_like(ys_ref)

    sub_t = lax.broadcasted_iota(jnp.int32, (LANES, tt), 0)
    pos_col = jnp.where(sub_t == MOE_POS_LANE, pos_row, comb_t).T[:, MOE_POS_LANE:MOE_POS_LANE + 1]
    for blk in range(tt // LANES):
        cc = lax.broadcasted_iota(jnp.int32, (LANES, rows), 1).astype(F32)
        sl = pl.ds(blk * LANES, LANES)
        pt_ref[sl, :] = jnp.where(cc == pos_col[blk * LANES:(blk + 1) * LANES, :], 1.0, 0.0).astype(BF16)


def _moe_kernel(hn_ref, comb_ref, x1_ref, wgu_ref, wd_ref, o_ref, xs_ref, cs_ref, ys_ref, pt_ref, seg_ref,
                *, tt, rows, sub):
    c = pl.program_id(1)

    @pl.when(c == 0)
    def _():
        _moe_sort(hn_ref, comb_ref, xs_ref, cs_ref, ys_ref, pt_ref, seg_ref, tt=tt, rows=rows, sub=sub)

    group = lax.shift_right_logical(c, (EXPERTS_PER_GROUP // MOE_CHUNK).bit_length() - 1)
    start = seg_ref[group]
    padded = seg_ref[N_GROUPS + group]
    n_sub = sum((padded > k * sub).astype(jnp.int32) for k in range(-(-rows // sub)))
    lane = lax.broadcasted_iota(jnp.int32, (sub, LANES), 1)

    def sub_tile(s, carry):
        sl = pl.ds(pl.multiple_of(start + s * sub, MOE_PAD), sub)
        x = xs_ref[sl, :]
        cw = cs_ref[sl, :]
        acc = ys_ref[sl, :]
        for k in range(MOE_CHUNK):
            gu = jnp.dot(x, wgu_ref[k], preferred_element_type=F32)
            gate, up = gu[:, :EXPERT_FF], gu[:, EXPERT_FF:]
            weight = jnp.sum(jnp.where(lane == c * MOE_CHUNK + k, cw, 0.0), axis=-1, keepdims=True)
            act = jax.nn.silu(gate) * up * weight
            acc = acc + jnp.dot(act.astype(BF16), wd_ref[k], preferred_element_type=F32)
        ys_ref[sl, :] = acc
        return carry

    lax.fori_loop(0, n_sub, sub_tile, 0)

    @pl.when(c == pl.num_programs(1) - 1)
    def _():
        y = ys_ref[pl.ds(0, rows), :].astype(BF16)
        o_ref[...] = x1_ref[...] + jnp.dot(pt_ref[...], y, preferred_element_type=F32)


def _moe(hn2, comb2, x1_2, w_gu, w_down, tt):
    t, d = hn2.shape
    rows = -(-(tt + N_GROUPS * MOE_PAD) // LANES) * LANES
    spread = 3.5 * (tt * (N_GROUPS - 1)) ** 0.5 / N_GROUPS
    sub = min(rows, -(-int(tt / N_GROUPS + spread) // MOE_PAD) * MOE_PAD)
    return pl.pallas_call(
        functools.partial(_moe_kernel, tt=tt, rows=rows, sub=sub),
        grid=(t // tt, N_EXPERTS // MOE_CHUNK),
        in_specs=[
            pl.BlockSpec((tt, d), lambda i, c: (i, 0)),
            pl.BlockSpec((tt, LANES), lambda i, c: (i, 0)),
            pl.BlockSpec((tt, d), lambda i, c: (i, 0)),
            pl.BlockSpec((MOE_CHUNK, d, 2 * EXPERT_FF), lambda i, c: (c, 0, 0)),
            pl.BlockSpec((MOE_CHUNK, EXPERT_FF, d), lambda i, c: (c, 0, 0)),
        ],
        out_specs=pl.BlockSpec((tt, d), lambda i, c: (i, 0)),
        out_shape=jax.ShapeDtypeStruct((t, d), F32),
        scratch_shapes=[pltpu.VMEM((rows + sub, d), BF16), pltpu.VMEM((rows + sub, LANES), F32),
                        pltpu.VMEM((rows + sub, d), F32), pltpu.VMEM((tt, rows), BF16),
                        pltpu.SMEM((2 * N_GROUPS,), jnp.int32)],
        compiler_params=_params(("parallel", "arbitrary")),
        name="moe",
    )(hn2, comb2, x1_2, w_gu, w_down)


def _block_diag(w):
    n, bd, _ = w.shape
    eye = jnp.eye(n, dtype=w.dtype)
    return jnp.einsum("nij,nm->nimj", w, eye).reshape(n * bd, n * bd)


def _tile(n, pref):
    while n % pref:
        pref //= 2
    return pref


def _layer(x, mem, g_mix, w_in, g_q_sb, g_k_sb, conv_w, conv_b, lru_w_a, lru_b_a, lru_w_i, lru_b_i,
           lru_lambda, g_mem, w_mem_kv, g_q_x, g_k_x, w_branch, w_out, g_ffn, w_group, b_group,
           w_expert, b_expert, w_gate, w_up, w_down):
    b, s, d = x.shape
    t = b * s
    row = lambda v: v.reshape(1, -1).astype(F32)

    ones = jnp.ones((PROJ_TILE,), F32)
    col_gain = jnp.concatenate([
        jnp.tile(g_q_sb, SB_WIDTH // SB_HEAD_DIM) * (SB_HEAD_DIM ** -0.5 * LOG2E),
        jnp.tile(g_k_sb, SB_WIDTH // SB_HEAD_DIM),
        ones, ones, ones,
        jnp.tile(g_q_x, X_HEADS) * X_HEAD_DIM ** -0.5,
    ] + [ones] * 6).reshape(1, -1)
    w_router = jnp.zeros((d, LANES), F32).at[:, :N_EXPERTS].set(w_expert)
    w_router = w_router.at[:, N_EXPERTS:N_EXPERTS + N_GROUPS].set(w_group)
    b_router = jnp.zeros((1, LANES), F32).at[0, :N_EXPERTS].set(b_expert)
    b_router = b_router.at[0, N_EXPERTS:N_EXPERTS + N_GROUPS].set(b_group)
    w_gu = jnp.concatenate([w_gate, w_up], axis=-1).reshape(N_EXPERTS, d, 2 * EXPERT_FF).astype(BF16)
    w_dn = w_down.reshape(N_EXPERTS, EXPERT_FF, d).astype(BF16)

    proj, o_lru = _in_proj(x.reshape(t, d), row(g_mix), w_in.astype(BF16), col_gain, conv_w, row(conv_b),
                           _block_diag(lru_w_a).astype(BF16), row(lru_b_a), _block_diag(lru_w_i).astype(BF16),
                           row(lru_b_i), row(lru_lambda), _tile(s, 512), s)
    proj3 = proj.reshape(b, s, -1)
    o_lru = o_lru.reshape(b, s, LRU_WIDTH)
    q_gain = jnp.abs(col_gain[0, :PROJ_TILE])
    z_max = 1.05 * SB_HEAD_DIM * jnp.max(q_gain) * jnp.max(jnp.abs(g_k_sb))
    carry_done = (z_max + BF16_ZERO_EXP).reshape(1, 1).astype(F32)
    o_sb = _sb_attention(proj3, carry_done, _tile(s, 256))
    kx, vx = _mem_kv(mem, row(g_mem), w_mem_kv.astype(BF16), row(jnp.tile(g_k_x, X_HEADS)))
    x1, hn, comb = _merge(proj3, o_sb, o_lru, kx, vx, x, w_branch.astype(BF16), w_out.astype(BF16),
                          row(g_ffn), w_router, b_router, _tile(s, 512))
    out = _moe(hn.reshape(t, d), comb.reshape(t, LANES), x1.reshape(t, d), w_gu, w_dn, _tile(t, 1024))
    return out.reshape(b, s, d)


def kernel(x, mem, g_mix, w_in, g_q_sb, g_k_sb, conv_w, conv_b, lru_w_a, lru_b_a, lru_w_i, lru_b_i, lru_lambda, g_mem, w_mem_kv, g_q_x, g_k_x, w_branch, w_out, g_ffn, w_group, b_group, w_expert, b_expert, w_gate, w_up, w_down):
    params = (g_mix, w_in, g_q_sb, g_k_sb, conv_w, conv_b, lru_w_a, lru_b_a, lru_w_i, lru_b_i, lru_lambda,
              g_mem, w_mem_kv, g_q_x, g_k_x, w_branch, w_out, g_ffn, w_group, b_group, w_expert, b_expert,
              w_gate, w_up, w_down)
    for layer in range(g_mix.shape[0]):
        x = _layer(x, mem, *[p[layer] for p in params])
    return x
```

```python
import functools

import jax
import jax.numpy as jnp
from jax import lax
from jax.experimental import pallas as pl
from jax.experimental.pallas import tpu as pltpu

F32 = jnp.float32
BF16 = jnp.bfloat16

EPS = 1e-6
SB_HEAD_DIM = 64
SB_WIDTH = 512
LRU_WIDTH = 512
LRU_BLOCKS = 8
CONV_WIDTH = 4
LRU_C = 8.0
X_HEADS = 4
X_WIDTH = 512
X_HEAD_DIM = 128
N_BRANCH = 3
N_GROUPS = 4
EXPERTS_PER_GROUP = 8
N_EXPERTS = N_GROUPS * EXPERTS_PER_GROUP
EXPERT_FF = 256

LOG2E = 1.4426950408889634
SB_GROUP = 4
SB_GROUP_WIDTH = SB_GROUP * SB_HEAD_DIM
SB_QBLOCKS = 4
BF16_ZERO_EXP = 160.0

MOE_CHUNK = 4
MXU_TILE = 256
MOE_PAD_LOG2 = 4
MOE_PAD = 1 << MOE_PAD_LOG2
MOE_GID_LANE = N_EXPERTS
MOE_POS_LANE = N_EXPERTS + 1
SUBLANES = 8

LANES = 128
VMEM_LIMIT = 56 * 1024 * 1024

PROJ_TILE = 512
KIND_HEAD64, KIND_HEAD128, KIND_PLAIN, KIND_SIGMOID, KIND_LRU = 0, 1, 2, 3, 4
PROJ_KINDS = (KIND_HEAD64, KIND_HEAD64, KIND_PLAIN, KIND_LRU, KIND_LRU, KIND_HEAD128) + (KIND_SIGMOID,) * 6
W_COL_XL, W_COL_YL = 3, 4
PROJ_TILES_BEFORE_LRU = 2
PROJ_OUT_COL = {j: sum(k != KIND_LRU for k in PROJ_KINDS[:j]) for j, kind in enumerate(PROJ_KINDS) if kind != KIND_LRU}
COL_Q, COL_K, COL_V, COL_QX, COL_GATE = (PROJ_OUT_COL[j] for j in (0, 1, 2, 5, 6))


def _params(sem):
    return pltpu.CompilerParams(dimension_semantics=sem, vmem_limit_bytes=VMEM_LIMIT)


def _rms(xf, g):
    return xf * lax.rsqrt(jnp.mean(xf * xf, axis=-1, keepdims=True) + EPS) * g


def _sigmoid(v):
    return 0.5 * jnp.tanh(0.5 * v) + 0.5


def _group_mean_matrix(n, group):
    shift = group.bit_length() - 1
    r = lax.shift_right_logical(lax.broadcasted_iota(jnp.int32, (n, n), 0), shift)
    c = lax.shift_right_logical(lax.broadcasted_iota(jnp.int32, (n, n), 1), shift)
    return jnp.where(r == c, 1.0 / group, 0.0).astype(BF16)


def _proj_kernel(x_ref, g_ref, w_ref, cg_ref, cw_ref, cb_ref, wa_ref, ba_ref, wi_ref, bi_ref, lam_ref,
                 o_ref, olru_ref, xbuf_ref, h_ref, *, tm, tiles_per_seq):
    seq_tile = pl.program_id(0) % tiles_per_seq

    @pl.when(seq_tile == 0)
    def _():
        xbuf_ref[pl.ds(0, SUBLANES), :] = jnp.zeros((SUBLANES, LRU_WIDTH), F32)
        h_ref[...] = jnp.zeros_like(h_ref)

    @pl.when(seq_tile > 0)
    def _():
        xbuf_ref[pl.ds(0, SUBLANES), :] = xbuf_ref[pl.ds(tm, SUBLANES), :]

    hn = _rms(x_ref[...], g_ref[...]).astype(BF16)
    group_mean = {KIND_HEAD64: _group_mean_matrix(PROJ_TILE, SB_HEAD_DIM),
                  KIND_HEAD128: _group_mean_matrix(PROJ_TILE, X_HEAD_DIM)}

    def column_tile(j):
        return jnp.dot(hn, w_ref[:, pl.ds(j * PROJ_TILE, PROJ_TILE)], preferred_element_type=F32)

    def stored_tile(j):
        kind, acc = PROJ_KINDS[j], column_tile(j)
        if kind in group_mean:
            ms = jnp.dot((acc * acc).astype(BF16), group_mean[kind], preferred_element_type=F32)
            acc = acc * lax.rsqrt(ms + EPS) * cg_ref[:, pl.ds(j * PROJ_TILE, PROJ_TILE)]
        elif kind == KIND_SIGMOID:
            acc = _sigmoid(acc)
        o_ref[:, pl.ds(PROJ_OUT_COL[j] * PROJ_TILE, PROJ_TILE)] = acc.astype(o_ref.dtype)

    stored = [j for j, kind in enumerate(PROJ_KINDS) if kind != KIND_LRU]
    x_lru, y_lru = column_tile(W_COL_XL), column_tile(W_COL_YL)
    for j in stored[:PROJ_TILES_BEFORE_LRU]:
        stored_tile(j)

    xbuf_ref[pl.ds(SUBLANES, tm), :] = x_lru
    xc = cb_ref[...] + jnp.zeros((tm, LRU_WIDTH), F32)
    for j in range(CONV_WIDTH):
        xc = xc + cw_ref[pl.ds(j, 1), :] * xbuf_ref[pl.ds(SUBLANES - (CONV_WIDTH - 1) + j, tm), :]
    xcb = xc.astype(BF16)
    r = _sigmoid(jnp.dot(xcb, wa_ref[...], preferred_element_type=F32) + ba_ref[...])
    gi = _sigmoid(jnp.dot(xcb, wi_ref[...], preferred_element_type=F32) + bi_ref[...])

    nlam = -lam_ref[...]
    sp_nlam = jnp.maximum(nlam, 0.0) + jnp.log(1.0 + jnp.exp(-jnp.abs(nlam)))
    state = h_ref[pl.ds(0, 1), :]
    rest = stored[PROJ_TILES_BEFORE_LRU:]
    chunk = tm // len(rest)
    for c, j in enumerate(rest):
        rows = slice(c * chunk, (c + 1) * chunk)
        log_a = (-LRU_C * r[rows]) * sp_nlam
        a = jnp.exp(log_a)
        th = jnp.tanh(log_a)
        n = -2.0 * th
        coef = jnp.where(n > 0.0, n * lax.rsqrt(n * (1.0 - th)), 0.0)
        b = coef * (gi[rows] * xc[rows])
        a = a.reshape(chunk // SUBLANES, SUBLANES, LRU_WIDTH)
        b = b.reshape(chunk // SUBLANES, SUBLANES, LRU_WIDTH)
        row_in_vreg = lax.broadcasted_iota(jnp.int32, a.shape, 1)
        d = 1
        while d < SUBLANES:
            keep = row_in_vreg >= d
            b = jnp.where(keep, a * pltpu.roll(b, d, 1), 0.0) + b
            a = jnp.where(keep, a * pltpu.roll(a, d, 1), a)
            d *= 2
        pieces = []
        for i in range(chunk // SUBLANES):
            pieces.append(b[i] + a[i] * state)
            state = pieces[-1][SUBLANES - 1:SUBLANES, :]
        h = jnp.concatenate(pieces, axis=0)
        olru_ref[pl.ds(c * chunk, chunk), :] = (h * jax.nn.gelu(y_lru[rows])).astype(olru_ref.dtype)
        stored_tile(j)
    h_ref[...] = jnp.broadcast_to(state, h_ref.shape)


def _in_proj(x2, g_mix, w_cat, col_gain, conv_w, conv_b, wa_bd, b_a, wi_bd, b_i, lam, tm, seq_len):
    t, d = x2.shape
    ncols = len([k for k in PROJ_KINDS if k != KIND_LRU]) * PROJ_TILE
    resident = lambda shape: pl.BlockSpec(shape, lambda i: (0, 0), pipeline_mode=pl.Buffered(1))
    vec = lambda: resident((1, LRU_WIDTH))
    mat = lambda: resident((LRU_WIDTH, LRU_WIDTH))
    return pl.pallas_call(
        functools.partial(_proj_kernel, tm=tm, tiles_per_seq=seq_len // tm),
        grid=(t // tm,),
        in_specs=[
            pl.BlockSpec((tm, d), lambda i: (i, 0)),
            resident((1, d)),
            resident((d, w_cat.shape[1])),
            resident((1, w_cat.shape[1])),
            resident((CONV_WIDTH, LRU_WIDTH)), vec(), mat(), vec(), mat(), vec(), vec(),
        ],
        out_specs=[pl.BlockSpec((tm, ncols), lambda i: (i, 0)),
                   pl.BlockSpec((tm, LRU_WIDTH), lambda i: (i, 0))],
        out_shape=[jax.ShapeDtypeStruct((t, ncols), BF16), jax.ShapeDtypeStruct((t, LRU_WIDTH), BF16)],
        scratch_shapes=[pltpu.VMEM((tm + SUBLANES, LRU_WIDTH), F32), pltpu.VMEM((SUBLANES, LRU_WIDTH), F32)],
        compiler_params=_params(("arbitrary",)),
        name="in_proj_rglru",
    )(x2, g_mix, w_cat, col_gain, conv_w, conv_b, wa_bd, b_a, wi_bd, b_i, lam)


def _sb_attn_kernel(done_ref, q_ref, k_ref, v_ref, o_ref, vexp_ref, acc_ref, carry_ref, cprev_ref, z_ref, d_ref, w_ref,
                    *, tq, nkb):
    nq = acc_ref.shape[0]
    step = pl.program_id(2)
    lane = lax.broadcasted_iota(jnp.int32, (tq, SB_GROUP_WIDTH), 1)
    head_lanes = [(lane >= h * SB_HEAD_DIM) & (lane < (h + 1) * SB_HEAD_DIM) for h in range(SB_GROUP)]
    keep_head = lambda m, a: jnp.where(m, a.astype(F32), 0.0).astype(BF16)

    @pl.when(step == 0)
    def _():
        def fill(j, c):
            v4 = v_ref[0, pl.ds(pl.multiple_of(j * tq, tq), tq), :]
            for h in range(SB_GROUP):
                vexp_ref[j, pl.ds(h * tq, tq), :] = keep_head(head_lanes[h], v4)
            return c
        lax.fori_loop(0, nkb, fill, 0)

    q_h = [[keep_head(m, q_ref[0, pl.ds(u * tq, tq), :]) for m in head_lanes] for u in range(nq)]
    qb = [step * nq + u for u in range(nq)]
    row = lax.broadcasted_iota(jnp.int32, (tq, tq), 0)
    col = lax.broadcasted_iota(jnp.int32, (tq, tq), 1)
    later_or_self = jnp.where(row >= col, 1.0, 0.0).astype(BF16)
    causal = col < row
    heads = range(SB_GROUP)
    sign_bit = jnp.uint32(0x80000000)

    def keys(kb):
        return k_ref[0, pl.ds(pl.multiple_of(kb * tq, tq), tq), :]

    def score(u, h, k4):
        return lax.dot_general(q_h[u][h], k4, (((1,), (1,)), ((), ())), preferred_element_type=F32)

    def softplus2(z):
        neg_abs = lax.bitcast_convert_type(lax.bitcast_convert_type(z, jnp.uint32) | sign_bit, F32)
        return jnp.maximum(z, 0.0) + jnp.log(1.0 + jnp.exp2(neg_abs)) * LOG2E

    def later_sums(sp):
        return jnp.dot(sp.astype(BF16), later_or_self, preferred_element_type=F32)

    has_left = [q > 0 for q in qb]
    left = [jnp.maximum(q - 1, 0) for q in qb]
    k_pair = [(keys(qb[u]), keys(left[u])) for u in range(nq)]
    per_q = 2 * SB_GROUP
    n_chain = nq * per_q
    z_pair, sp_pair, totals, d_pair, w_pair, acc_pair = {}, {}, {}, {}, {}, {}

    def pair_scores(c):
        u, i = divmod(c, per_q)
        z_pair[c] = score(u, i % SB_GROUP, k_pair[u][i // SB_GROUP])

    def pair_softplus(c):
        sp = softplus2(z_pair[c])
        sp_pair[c] = jnp.where(causal, sp, 0.0) if c % per_q < SB_GROUP else sp
        totals[c] = jnp.sum(sp_pair[c], axis=-1, keepdims=True)

    def pair_sums(c):
        d = z_pair[c] - later_sums(sp_pair[c])
        if c % per_q < SB_GROUP:
            d_pair[c] = jnp.where(causal, d, -jnp.inf)
        else:
            d_pair[c] = d - jnp.where(has_left[c // per_q], totals[c - SB_GROUP], jnp.inf)

    def pair_weights(c):
        w_pair[c] = jnp.exp2(d_pair[c]).astype(BF16)
        if c % SB_GROUP == SB_GROUP - 1:
            u, i = divmod(c, per_q)
            w = jnp.concatenate([w_pair[c - SB_GROUP + 1 + h] for h in heads], axis=1)
            kb = qb[u] if i < SB_GROUP else left[u]
            acc_pair[c] = jnp.dot(w, vexp_ref[kb], preferred_element_type=F32)

    stages = (pair_scores, pair_softplus, pair_sums, pair_weights)
    for t in range(n_chain + len(stages) - 1):
        for s, stage in enumerate(stages):
            if 0 <= t - s < n_chain:
                stage(t - s)
    for u in range(nq):
        base = u * per_q
        acc_ref[u] = acc_pair[base + SB_GROUP - 1] + acc_pair[base + per_q - 1]
        for h in heads:
            both = totals[base + h] + jnp.where(has_left[u], totals[base + SB_GROUP + h], 0.0)
            carry_ref[u, h] = jnp.broadcast_to(both, (tq, LANES))

    def stick_left(u):
        least = functools.reduce(jnp.minimum, [carry_ref[u, h] for h in heads])
        return jnp.min(least) < done_ref[0, 0]

    def sweep(u):
        def scores(h, k4):
            z_ref[h] = score(u, h, k4)

        def sums(h):
            z = z_ref[h]
            sp = softplus2(z)
            d_ref[h] = z - later_sums(sp)
            carry = carry_ref[u, h]
            cprev_ref[h] = carry
            carry_ref[u, h] = carry + jnp.sum(sp, axis=-1, keepdims=True)

        def weights(h):
            carry_t = jnp.concatenate([cprev_ref[h]] * (tq // LANES), axis=1)
            w_ref[:, pl.ds(h * tq, tq)] = jnp.exp2(d_ref[h] - carry_t).astype(BF16)

        def values(kb):
            acc_ref[u] += jnp.dot(w_ref[...], vexp_ref[kb], preferred_element_type=F32)

        first = qb[u] - 2
        w_ref[...] = jnp.zeros_like(w_ref)
        k_first = keys(first)
        for h in heads:
            scores(h, k_first)
        k_next = keys(jnp.maximum(first - 1, 0))
        for h in heads:
            sums(h)
            scores(h, k_next)

        def body(c):
            j, _ = c
            values(jnp.minimum(first - j + 2, nkb - 1))
            for h in heads:
                weights(h)
            k_next = keys(jnp.maximum(first - j - 1, 0))
            for h in heads:
                sums(h)
                scores(h, k_next)
            return j + 1, stick_left(u)

        j_end, _ = lax.while_loop(lambda c: (c[0] <= first) & c[1], body, (jnp.int32(1), stick_left(u)))
        values(jnp.minimum(first - j_end + 2, nkb - 1))
        for h in heads:
            weights(h)
        values(first - j_end + 1)

    for u in range(nq):
        pl.when((qb[u] >= 2) & stick_left(u))(functools.partial(sweep, u))
        o_ref[0, pl.ds(u * tq, tq), :] = acc_ref[u].astype(o_ref.dtype)


def _sb_attention(proj3, carry_done, tq):
    b, s, _ = proj3.shape
    w = SB_GROUP_WIDTH
    groups = SB_WIDTH // w
    qoff, koff, voff = (COL_Q * PROJ_TILE // w, COL_K * PROJ_TILE // w, COL_V * PROJ_TILE // w)
    nq = SB_QBLOCKS
    while (s // tq) % nq:
        nq //= 2
    tqs = nq * tq
    return pl.pallas_call(
        functools.partial(_sb_attn_kernel, tq=tq, nkb=s // tq),
        grid=(b, groups, s // tqs),
        in_specs=[
            pl.BlockSpec(memory_space=pltpu.SMEM),
            pl.BlockSpec((1, tqs, w), lambda bi, p, qi: (bi, qi, qoff + p)),
            pl.BlockSpec((1, s, w), lambda bi, p, qi: (bi, 0, koff + p)),
            pl.BlockSpec((1, s, w), lambda bi, p, qi: (bi, 0, voff + p)),
        ],
        out_specs=pl.BlockSpec((1, tqs, w), lambda bi, p, qi: (bi, qi, p)),
        out_shape=jax.ShapeDtypeStruct((b, s, SB_WIDTH), BF16),
        scratch_shapes=[pltpu.VMEM((s // tq, SB_GROUP * tq, w), BF16),
                        pltpu.VMEM((nq, tq, w), F32), pltpu.VMEM((nq, SB_GROUP, tq, LANES), F32),
                        pltpu.VMEM((SB_GROUP, tq, LANES), F32), pltpu.VMEM((SB_GROUP, tq, tq), F32),
                        pltpu.VMEM((SB_GROUP, tq, tq), F32), pltpu.VMEM((tq, SB_GROUP * tq), BF16)],
        compiler_params=_params(("parallel", "parallel", "arbitrary")),
        name="sb_attention",
    )(carry_done, proj3, proj3, proj3)


def _mem_kv_kernel(m_ref, g_ref, w_ref, gk_ref, k_ref, v_ref):
    mn = _rms(m_ref[0], g_ref[...]).astype(BF16)
    kv = jnp.dot(mn, w_ref[...], preferred_element_type=F32)
    k = kv[:, :X_WIDTH]
    ms = jnp.dot((k * k).astype(BF16), _group_mean_matrix(X_WIDTH, X_HEAD_DIM), preferred_element_type=F32)
    k_ref[0] = (k * lax.rsqrt(ms + EPS) * gk_ref[...]).astype(BF16)
    v_ref[0] = kv[:, X_WIDTH:].astype(BF16)


def _mem_kv(mem, g_mem, w_kv, gk_cols):
    b, m, d = mem.shape
    return pl.pallas_call(
        _mem_kv_kernel,
        grid=(b,),
        in_specs=[
            pl.BlockSpec((1, m, d), lambda bi: (bi, 0, 0)),
            pl.BlockSpec((1, d), lambda bi: (0, 0)),
            pl.BlockSpec((d, 2 * X_WIDTH), lambda bi: (0, 0)),
            pl.BlockSpec((1, X_WIDTH), lambda bi: (0, 0)),
        ],
        out_specs=[pl.BlockSpec((1, m, X_WIDTH), lambda bi: (bi, 0, 0))] * 2,
        out_shape=[jax.ShapeDtypeStruct((b, m, X_WIDTH), BF16)] * 2,
        compiler_params=_params(("parallel",)),
        name="mem_kv",
    )(mem, g_mem, w_kv, gk_cols)


def _split_bf16(v):
    hi = v.astype(BF16)
    return hi, (v - hi.astype(F32)).astype(BF16)


def _router(logits):
    lane = lax.broadcasted_iota(jnp.int32, logits.shape, 1).astype(F32)
    ninf = -jnp.inf
    far = float(LANES)
    is_group = (lane >= N_EXPERTS) & (lane < N_EXPERTS + N_GROUPS)
    gl = jnp.where(is_group, logits, ninf)
    gmax = jnp.max(gl, axis=-1, keepdims=True)
    gidx = jnp.min(jnp.where(gl == gmax, lane, far), axis=-1, keepdims=True) - N_EXPERTS
    g_prob = 1.0 / jnp.sum(jnp.exp(gl - gmax), axis=-1, keepdims=True)
    first = gidx * EXPERTS_PER_GROUP
    el = jnp.where((lane >= first) & (lane < first + EXPERTS_PER_GROUP), logits, ninf)
    m1 = jnp.max(el, axis=-1, keepdims=True)
    i1 = jnp.min(jnp.where(el == m1, lane, far), axis=-1, keepdims=True)
    el2 = jnp.where(lane == i1, ninf, el)
    m2 = jnp.max(el2, axis=-1, keepdims=True)
    i2 = jnp.min(jnp.where(el2 == m2, lane, far), axis=-1, keepdims=True)
    e2 = jnp.exp(m2 - m1)
    w1 = 1.0 / (1.0 + e2)
    w2 = e2 / (1.0 + e2)
    combine = g_prob * (jnp.where(lane == i1, w1, 0.0) + jnp.where(lane == i2, w2, 0.0))
    return combine + jnp.where(lane == MOE_GID_LANE, gidx, 0.0)


def _merge_kernel(qx_ref, g0_ref, g1_ref, g2_ref, osb_ref, olru_ref, kx_ref, vx_ref, x_ref,
                  wb_ref, wo_ref, gf_ref, wr_ref, br_ref, x1_ref, hn_ref, comb_ref):
    qx = qx_ref[0]
    heads = []
    for h in range(X_HEADS):
        sl = slice(h * X_HEAD_DIM, (h + 1) * X_HEAD_DIM)
        s = lax.dot_general(qx[:, sl], kx_ref[0][:, sl], (((1,), (1,)), ((), ())),
                            preferred_element_type=F32)
        p = jnp.exp(s - jnp.max(s, axis=-1, keepdims=True))
        p = p / jnp.sum(p, axis=-1, keepdims=True)
        heads.append(jnp.dot(p.astype(BF16), vx_ref[0][:, sl], preferred_element_type=F32))
    o_x = jnp.concatenate(heads, axis=1).astype(BF16)

    merged = g0_ref[0].astype(F32) * jnp.dot(osb_ref[0], wb_ref[0], preferred_element_type=F32)
    merged += g1_ref[0].astype(F32) * jnp.dot(olru_ref[0], wb_ref[1], preferred_element_type=F32)
    merged += g2_ref[0].astype(F32) * jnp.dot(o_x, wb_ref[2], preferred_element_type=F32)
    x1 = x_ref[0] + jnp.dot(merged.astype(BF16), wo_ref[...], preferred_element_type=F32)
    x1_ref[0] = x1

    hn = _rms(x1, gf_ref[...])
    hn_ref[0] = hn.astype(BF16)
    h_hi, h_lo = _split_bf16(hn)
    w_hi, w_lo = _split_bf16(wr_ref[...])
    both = jnp.dot(h_hi, jnp.concatenate([w_hi, w_lo], axis=1), preferred_element_type=F32)
    logits = both[:, :LANES] + both[:, LANES:] + jnp.dot(h_lo, w_hi, preferred_element_type=F32) + br_ref[...]
    comb_ref[0] = _router(logits)


def _merge(proj3, o_sb, o_lru, kx, vx, x, w_branch, w_out, g_ffn, w_router, b_router, tm):
    b, s, d = x.shape
    m = kx.shape[1]
    gate_blk = lambda n: pl.BlockSpec((1, tm, d), lambda bi, si, n=n: (bi, si, COL_GATE * PROJ_TILE // d + n))
    tok512 = lambda: pl.BlockSpec((1, tm, SB_WIDTH), lambda bi, si: (bi, si, 0))
    const2 = lambda shape: pl.BlockSpec(shape, lambda bi, si: (0, 0))
    return pl.pallas_call(
        _merge_kernel,
        grid=(b, s // tm),
        in_specs=[
            pl.BlockSpec((1, tm, X_WIDTH), lambda bi, si: (bi, si, COL_QX)),
            gate_blk(0), gate_blk(1), gate_blk(2),
            tok512(), tok512(),
            pl.BlockSpec((1, m, X_WIDTH), lambda bi, si: (bi, 0, 0)),
            pl.BlockSpec((1, m, X_WIDTH), lambda bi, si: (bi, 0, 0)),
            pl.BlockSpec((1, tm, d), lambda bi, si: (bi, si, 0)),
            pl.BlockSpec((N_BRANCH, SB_WIDTH, d), lambda bi, si: (0, 0, 0)),
            const2((d, d)), const2((1, d)), const2((d, LANES)), const2((1, LANES)),
        ],
        out_specs=[
            pl.BlockSpec((1, tm, d), lambda bi, si: (bi, si, 0)),
            pl.BlockSpec((1, tm, d), lambda bi, si: (bi, si, 0)),
            pl.BlockSpec((1, tm, LANES), lambda bi, si: (bi, si, 0)),
        ],
        out_shape=[
            jax.ShapeDtypeStruct((b, s, d), F32),
            jax.ShapeDtypeStruct((b, s, d), BF16),
            jax.ShapeDtypeStruct((b, s, LANES), F32),
        ],
        compiler_params=_params(("parallel", "parallel")),
        name="merge_router",
    )(proj3, proj3, proj3, proj3, o_sb, o_lru, kx, vx, x, w_branch, w_out, g_ffn, w_router, b_router)


def _moe_sort(hn_ref, comb_ref, xs_ref, cs_ref, ys_ref, pt_ref, seg_ref, *, tt, rows, sub):
    comb = comb_ref[...]
    comb_t = comb.T
    gid_row = comb_t[MOE_GID_LANE:MOE_GID_LANE + 1, :]
    group_of_row = lax.broadcasted_iota(jnp.int32, (SUBLANES, tt), 0).astype(F32)
    member = jnp.where(group_of_row == gid_row, 1.0, 0.0)
    r = lax.broadcasted_iota(jnp.int32, (MXU_TILE, MXU_TILE), 0)
    c = lax.broadcasted_iota(jnp.int32, (MXU_TILE, MXU_TILE), 1)
    upto = jnp.where(r <= c, 1.0, 0.0).astype(BF16)
    before = jnp.zeros((SUBLANES, 1), F32)
    ranks = []
    for blk in range(tt // MXU_TILE):
        m = member[:, blk * MXU_TILE:(blk + 1) * MXU_TILE]
        incl = jnp.dot(m.astype(BF16), upto, preferred_element_type=F32) + before
        ranks.append(jnp.sum(m * (incl - m), axis=0, keepdims=True))
        before = before + jnp.sum(m, axis=1, keepdims=True)
    pos_row = jnp.concatenate(ranks, axis=1)
    start = jnp.int32(0)
    for g in range(N_GROUPS):
        count = jnp.sum(member[g:g + 1, :]).astype(jnp.int32)
        padded = lax.shift_left(lax.shift_right_logical(count + (MOE_PAD - 1), MOE_PAD_LOG2), MOE_PAD_LOG2)
        seg_ref[g] = start
        seg_ref[N_GROUPS + g] = padded
        pos_row = pos_row + member[g:g + 1, :] * start.astype(F32)
        start = start + padded

    hi_lo = jnp.concatenate(_split_bf16(comb), axis=1)
    hn = hn_ref[...]
    for blk in range(rows // LANES):
        rr = (lax.broadcasted_iota(jnp.int32, (LANES, tt), 0) + blk * LANES).astype(F32)
        p = jnp.where(rr == pos_row, 1.0, 0.0).astype(BF16)
        sl = pl.ds(blk * LANES, LANES)
        xs_ref[sl, :] = jnp.dot(p, hn, preferred_element_type=F32).astype(BF16)
        hl = jnp.dot(p, hi_lo, preferred_element_type=F32)
        cs_ref[sl, :] = hl[:, :LANES] + hl[:, LANES:]
    tail = pl.ds(rows, sub)
    xs_ref[tail, :] = jnp.zeros((sub, xs_ref.shape[1]), BF16)
    cs_ref[tail, :] = jnp.zeros((sub, LANES), F32)
    ys_ref[...] = jnp.zeros_like(ys_ref)

    sub_t = lax.broadcasted_iota(jnp.int32, (LANES, tt), 0)
    pos_col = jnp.where(sub_t == MOE_POS_LANE, pos_row, comb_t).T[:, MOE_POS_LANE:MOE_POS_LANE + 1]
    for blk in range(tt // LANES):
        cc = lax.broadcasted_iota(jnp.int32, (LANES, rows), 1).astype(F32)
        sl = pl.ds(blk * LANES, LANES)
        pt_ref[sl, :] = jnp.where(cc == pos_col[blk * LANES:(blk + 1) * LANES, :], 1.0, 0.0).astype(BF16)


def _moe_kernel(hn_ref, comb_ref, x1_ref, wgu_ref, wd_ref, o_ref, xs_ref, cs_ref, ys_ref, pt_ref, seg_ref,
                *, tt, rows, sub):
    c = pl.program_id(1)

    @pl.when(c == 0)
    def _():
        _moe_sort(hn_ref, comb_ref, xs_ref, cs_ref, ys_ref, pt_ref, seg_ref, tt=tt, rows=rows, sub=sub)

    group = lax.shift_right_logical(c, (EXPERTS_PER_GROUP // MOE_CHUNK).bit_length() - 1)
    start = seg_ref[group]
    padded = seg_ref[N_GROUPS + group]
    n_sub = sum((padded > k * sub).astype(jnp.int32) for k in range(-(-rows // sub)))
    lane = lax.broadcasted_iota(jnp.int32, (sub, LANES), 1)

    def sub_tile(s, carry):
        sl = pl.ds(pl.multiple_of(start + s * sub, MOE_PAD), sub)
        x = xs_ref[sl, :]
        cw = cs_ref[sl, :]
        acc = ys_ref[sl, :]
        gate_up = {}
        for step in range(MOE_CHUNK + 1):
            if step < MOE_CHUNK:
                gate_up[step] = jnp.dot(x, wgu_ref[step], preferred_element_type=F32)
            if step > 0:
                k = step - 1
                gate, up = gate_up[k][:, :EXPERT_FF], gate_up[k][:, EXPERT_FF:]
                weight = jnp.sum(jnp.where(lane == c * MOE_CHUNK + k, cw, 0.0), axis=-1, keepdims=True)
                act = jax.nn.silu(gate) * up * weight
                acc = acc + jnp.dot(act.astype(BF16), wd_ref[k], preferred_element_type=F32)
        ys_ref[sl, :] = acc
        return carry

    lax.fori_loop(0, n_sub, sub_tile, 0)

    @pl.when(c == pl.num_programs(1) - 1)
    def _():
        y = ys_ref[pl.ds(0, rows), :].astype(BF16)
        o_ref[...] = x1_ref[...] + jnp.dot(pt_ref[...], y, preferred_element_type=F32)


def _moe(hn2, comb2, x1_2, w_gu, w_down, tt):
    t, d = hn2.shape
    rows = -(-(tt + N_GROUPS * MOE_PAD) // LANES) * LANES
    spread = 3.5 * (tt * (N_GROUPS - 1)) ** 0.5 / N_GROUPS
    sub = min(rows, -(-int(tt / N_GROUPS + spread) // MOE_PAD) * MOE_PAD)
    return pl.pallas_call(
        functools.partial(_moe_kernel, tt=tt, rows=rows, sub=sub),
        grid=(t // tt, N_EXPERTS // MOE_CHUNK),
        in_specs=[
            pl.BlockSpec((tt, d), lambda i, c: (i, 0)),
            pl.BlockSpec((tt, LANES), lambda i, c: (i, 0)),
            pl.BlockSpec((tt, d), lambda i, c: (i, 0)),
            pl.BlockSpec((MOE_CHUNK, d, 2 * EXPERT_FF), lambda i, c: (c, 0, 0)),
            pl.BlockSpec((MOE_CHUNK, EXPERT_FF, d), lambda i, c: (c, 0, 0)),
        ],
        out_specs=pl.BlockSpec((tt, d), lambda i, c: (i, 0)),
        out_shape=jax.ShapeDtypeStruct((t, d), F32),
        scratch_shapes=[pltpu.VMEM((rows + sub, d), BF16), pltpu.VMEM((rows + sub, LANES), F32),
                        pltpu.VMEM((rows + sub, d), F32), pltpu.VMEM((tt, rows), BF16),
                        pltpu.SMEM((2 * N_GROUPS,), jnp.int32)],
        compiler_params=_params(("parallel", "arbitrary")),
        name="moe",
    )(hn2, comb2, x1_2, w_gu, w_down)


def _block_diag(w):
    n, bd, _ = w.shape
    eye = jnp.eye(n, dtype=w.dtype)
    return jnp.einsum("nij,nm->nimj", w, eye).reshape(n * bd, n * bd)


def _tile(n, pref):
    while n % pref:
        pref //= 2
    return pref


def _layer(x, mem, g_mix, w_in, g_q_sb, g_k_sb, conv_w, conv_b, lru_w_a, lru_b_a, lru_w_i, lru_b_i,
           lru_lambda, g_mem, w_mem_kv, g_q_x, g_k_x, w_branch, w_out, g_ffn, w_group, b_group,
           w_expert, b_expert, w_gate, w_up, w_down):
    b, s, d = x.shape
    t = b * s
    row = lambda v: v.reshape(1, -1).astype(F32)

    ones = jnp.ones((PROJ_TILE,), F32)
    col_gain = jnp.concatenate([
        jnp.tile(g_q_sb, SB_WIDTH // SB_HEAD_DIM) * (SB_HEAD_DIM ** -0.5 * LOG2E),
        jnp.tile(g_k_sb, SB_WIDTH // SB_HEAD_DIM),
        ones, ones, ones,
        jnp.tile(g_q_x, X_HEADS) * X_HEAD_DIM ** -0.5,
    ] + [ones] * 6).reshape(1, -1)
    w_router = jnp.zeros((d, LANES), F32).at[:, :N_EXPERTS].set(w_expert)
    w_router = w_router.at[:, N_EXPERTS:N_EXPERTS + N_GROUPS].set(w_group)
    b_router = jnp.zeros((1, LANES), F32).at[0, :N_EXPERTS].set(b_expert)
    b_router = b_router.at[0, N_EXPERTS:N_EXPERTS + N_GROUPS].set(b_group)
    w_gu = jnp.concatenate([w_gate, w_up], axis=-1).reshape(N_EXPERTS, d, 2 * EXPERT_FF).astype(BF16)
    w_dn = w_down.reshape(N_EXPERTS, EXPERT_FF, d).astype(BF16)

    proj, o_lru = _in_proj(x.reshape(t, d), row(g_mix), w_in.astype(BF16), col_gain, conv_w, row(conv_b),
                           _block_diag(lru_w_a).astype(BF16), row(lru_b_a), _block_diag(lru_w_i).astype(BF16),
                           row(lru_b_i), row(lru_lambda), _tile(s, 512), s)
    proj3 = proj.reshape(b, s, -1)
    o_lru = o_lru.reshape(b, s, LRU_WIDTH)
    q_gain = jnp.abs(col_gain[0, :PROJ_TILE])
    z_max = 1.05 * SB_HEAD_DIM * jnp.max(q_gain) * jnp.max(jnp.abs(g_k_sb))
    carry_done = (z_max + BF16_ZERO_EXP).reshape(1, 1).astype(F32)
    o_sb = _sb_attention(proj3, carry_done, _tile(s, 256))
    kx, vx = _mem_kv(mem, row(g_mem), w_mem_kv.astype(BF16), row(jnp.tile(g_k_x, X_HEADS)))
    x1, hn, comb = _merge(proj3, o_sb, o_lru, kx, vx, x, w_branch.astype(BF16), w_out.astype(BF16),
                          row(g_ffn), w_router, b_router, _tile(s, 512))
    out = _moe(hn.reshape(t, d), comb.reshape(t, LANES), x1.reshape(t, d), w_gu, w_dn, _tile(t, 1024))
    return out.reshape(b, s, d)


def kernel(x, mem, g_mix, w_in, g_q_sb, g_k_sb, conv_w, conv_b, lru_w_a, lru_b_a, lru_w_i, lru_b_i, lru_lambda, g_mem, w_mem_kv, g_q_x, g_k_x, w_branch, w_out, g_ffn, w_group, b_group, w_expert, b_expert, w_gate, w_up, w_down):
    params = (g_mix, w_in, g_q_sb, g_k_sb, conv_w, conv_b, lru_w_a, lru_b_a, lru_w_i, lru_b_i, lru_lambda,
              g_mem, w_mem_kv, g_q_x, g_k_x, w_branch, w_out, g_ffn, w_group, b_group, w_expert, b_expert,
              w_gate, w_up, w_down)
    for layer in range(g_mix.shape[0]):
        x = _layer(x, mem, *[p[layer] for p in params])
    return x
```

```python
import functools

import jax
import jax.numpy as jnp
from jax import lax
from jax.experimental import pallas as pl
from jax.experimental.pallas import tpu as pltpu

F32 = jnp.float32
BF16 = jnp.bfloat16

EPS = 1e-6
SB_HEAD_DIM = 64
SB_WIDTH = 512
LRU_WIDTH = 512
LRU_BLOCKS = 8
CONV_WIDTH = 4
LRU_C = 8.0
X_HEADS = 4
X_WIDTH = 512
X_HEAD_DIM = 128
N_BRANCH = 3
N_GROUPS = 4
EXPERTS_PER_GROUP = 8
N_EXPERTS = N_GROUPS * EXPERTS_PER_GROUP
EXPERT_FF = 256

LOG2E = 1.4426950408889634
SB_GROUP = 4
SB_GROUP_WIDTH = SB_GROUP * SB_HEAD_DIM
SB_QBLOCKS = 4
BF16_ZERO_EXP = 160.0

MOE_CHUNK = 8
MXU_TILE = 256
MOE_PAD_LOG2 = 4
MOE_PAD = 1 << MOE_PAD_LOG2
MOE_GID_LANE = N_EXPERTS
MOE_POS_LANE = N_EXPERTS + 1
SUBLANES = 8

LANES = 128
VMEM_LIMIT = 56 * 1024 * 1024
MOE_VMEM_LIMIT = 60 * 1024 * 1024

PROJ_TILE = 512
KIND_HEAD64, KIND_HEAD128, KIND_PLAIN, KIND_SIGMOID, KIND_LRU = 0, 1, 2, 3, 4
PROJ_KINDS = (KIND_HEAD64, KIND_HEAD64, KIND_PLAIN, KIND_LRU, KIND_LRU, KIND_HEAD128) + (KIND_SIGMOID,) * 6
W_COL_XL, W_COL_YL = 3, 4
PROJ_TILES_BEFORE_LRU = 2
PROJ_OUT_COL = {j: sum(k != KIND_LRU for k in PROJ_KINDS[:j]) for j, kind in enumerate(PROJ_KINDS) if kind != KIND_LRU}
COL_Q, COL_K, COL_V, COL_QX, COL_GATE = (PROJ_OUT_COL[j] for j in (0, 1, 2, 5, 6))


def _params(sem, vmem_limit=VMEM_LIMIT):
    return pltpu.CompilerParams(dimension_semantics=sem, vmem_limit_bytes=vmem_limit)


def _rms(xf, g):
    return xf * lax.rsqrt(jnp.mean(xf * xf, axis=-1, keepdims=True) + EPS) * g


def _sigmoid(v):
    return 0.5 * jnp.tanh(0.5 * v) + 0.5


def _group_mean_matrix(n, group):
    shift = group.bit_length() - 1
    r = lax.shift_right_logical(lax.broadcasted_iota(jnp.int32, (n, n), 0), shift)
    c = lax.shift_right_logical(lax.broadcasted_iota(jnp.int32, (n, n), 1), shift)
    return jnp.where(r == c, 1.0 / group, 0.0).astype(BF16)


def _proj_kernel(x_ref, g_ref, w_ref, cg_ref, cw_ref, cb_ref, wa_ref, ba_ref, wi_ref, bi_ref, lam_ref,
                 o_ref, olru_ref, xbuf_ref, h_ref, *, tm, tiles_per_seq):
    seq_tile = pl.program_id(0) % tiles_per_seq

    @pl.when(seq_tile == 0)
    def _():
        xbuf_ref[pl.ds(0, SUBLANES), :] = jnp.zeros((SUBLANES, LRU_WIDTH), F32)
        h_ref[...] = jnp.zeros_like(h_ref)

    @pl.when(seq_tile > 0)
    def _():
        xbuf_ref[pl.ds(0, SUBLANES), :] = xbuf_ref[pl.ds(tm, SUBLANES), :]

    hn = _rms(x_ref[...], g_ref[...]).astype(BF16)
    group_mean = {KIND_HEAD64: _group_mean_matrix(PROJ_TILE, SB_HEAD_DIM),
                  KIND_HEAD128: _group_mean_matrix(PROJ_TILE, X_HEAD_DIM)}

    def column_tile(j):
        return jnp.dot(hn, w_ref[:, pl.ds(j * PROJ_TILE, PROJ_TILE)], preferred_element_type=F32)

    def stored_tile(j):
        kind, acc = PROJ_KINDS[j], column_tile(j)
        if kind in group_mean:
            ms = jnp.dot((acc * acc).astype(BF16), group_mean[kind], preferred_element_type=F32)
            acc = acc * lax.rsqrt(ms + EPS) * cg_ref[:, pl.ds(j * PROJ_TILE, PROJ_TILE)]
        elif kind == KIND_SIGMOID:
            acc = _sigmoid(acc)
        o_ref[:, pl.ds(PROJ_OUT_COL[j] * PROJ_TILE, PROJ_TILE)] = acc.astype(o_ref.dtype)

    stored = [j for j, kind in enumerate(PROJ_KINDS) if kind != KIND_LRU]
    x_lru, y_lru = column_tile(W_COL_XL), column_tile(W_COL_YL)
    for j in stored[:PROJ_TILES_BEFORE_LRU]:
        stored_tile(j)

    xbuf_ref[pl.ds(SUBLANES, tm), :] = x_lru
    xc = cb_ref[...] + jnp.zeros((tm, LRU_WIDTH), F32)
    for j in range(CONV_WIDTH):
        xc = xc + cw_ref[pl.ds(j, 1), :] * xbuf_ref[pl.ds(SUBLANES - (CONV_WIDTH - 1) + j, tm), :]
    xcb = xc.astype(BF16)
    r = _sigmoid(jnp.dot(xcb, wa_ref[...], preferred_element_type=F32) + ba_ref[...])
    gi = _sigmoid(jnp.dot(xcb, wi_ref[...], preferred_element_type=F32) + bi_ref[...])

    nlam = -lam_ref[...]
    sp_nlam = jnp.maximum(nlam, 0.0) + jnp.log(1.0 + jnp.exp(-jnp.abs(nlam)))
    state = h_ref[pl.ds(0, 1), :]
    rest = stored[PROJ_TILES_BEFORE_LRU:]
    chunk = tm // len(rest)
    for c, j in enumerate(rest):
        rows = slice(c * chunk, (c + 1) * chunk)
        log_a = (-LRU_C * r[rows]) * sp_nlam
        a = jnp.exp(log_a)
        th = jnp.tanh(log_a)
        n = -2.0 * th
        coef = jnp.where(n > 0.0, n * lax.rsqrt(n * (1.0 - th)), 0.0)
        b = coef * (gi[rows] * xc[rows])
        a = a.reshape(chunk // SUBLANES, SUBLANES, LRU_WIDTH)
        b = b.reshape(chunk // SUBLANES, SUBLANES, LRU_WIDTH)
        row_in_vreg = lax.broadcasted_iota(jnp.int32, a.shape, 1)
        d = 1
        while d < SUBLANES:
            keep = row_in_vreg >= d
            b = jnp.where(keep, a * pltpu.roll(b, d, 1), 0.0) + b
            a = jnp.where(keep, a * pltpu.roll(a, d, 1), a)
            d *= 2
        pieces = []
        for i in range(chunk // SUBLANES):
            pieces.append(b[i] + a[i] * state)
            state = pieces[-1][SUBLANES - 1:SUBLANES, :]
        h = jnp.concatenate(pieces, axis=0)
        olru_ref[pl.ds(c * chunk, chunk), :] = (h * jax.nn.gelu(y_lru[rows])).astype(olru_ref.dtype)
        stored_tile(j)
    h_ref[...] = jnp.broadcast_to(state, h_ref.shape)


def _in_proj(x2, g_mix, w_cat, col_gain, conv_w, conv_b, wa_bd, b_a, wi_bd, b_i, lam, tm, seq_len):
    t, d = x2.shape
    ncols = len([k for k in PROJ_KINDS if k != KIND_LRU]) * PROJ_TILE
    resident = lambda shape: pl.BlockSpec(shape, lambda i: (0, 0), pipeline_mode=pl.Buffered(1))
    vec = lambda: resident((1, LRU_WIDTH))
    mat = lambda: resident((LRU_WIDTH, LRU_WIDTH))
    return pl.pallas_call(
        functools.partial(_proj_kernel, tm=tm, tiles_per_seq=seq_len // tm),
        grid=(t // tm,),
        in_specs=[
            pl.BlockSpec((tm, d), lambda i: (i, 0)),
            resident((1, d)),
            resident((d, w_cat.shape[1])),
            resident((1, w_cat.shape[1])),
            resident((CONV_WIDTH, LRU_WIDTH)), vec(), mat(), vec(), mat(), vec(), vec(),
        ],
        out_specs=[pl.BlockSpec((tm, ncols), lambda i: (i, 0)),
                   pl.BlockSpec((tm, LRU_WIDTH), lambda i: (i, 0))],
        out_shape=[jax.ShapeDtypeStruct((t, ncols), BF16), jax.ShapeDtypeStruct((t, LRU_WIDTH), BF16)],
        scratch_shapes=[pltpu.VMEM((tm + SUBLANES, LRU_WIDTH), F32), pltpu.VMEM((SUBLANES, LRU_WIDTH), F32)],
        compiler_params=_params(("arbitrary",)),
        name="in_proj_rglru",
    )(x2, g_mix, w_cat, col_gain, conv_w, conv_b, wa_bd, b_a, wi_bd, b_i, lam)


def _sb_attn_kernel(done_ref, q_ref, k_ref, v_ref, o_ref, vexp_ref, acc_ref, carry_ref, cprev_ref, z_ref, d_ref, w_ref,
                    *, tq, nkb):
    nq = acc_ref.shape[0]
    step = pl.program_id(2)
    lane = lax.broadcasted_iota(jnp.int32, (tq, SB_GROUP_WIDTH), 1)
    head_lanes = [(lane >= h * SB_HEAD_DIM) & (lane < (h + 1) * SB_HEAD_DIM) for h in range(SB_GROUP)]
    keep_head = lambda m, a: jnp.where(m, a.astype(F32), 0.0).astype(BF16)

    @pl.when(step == 0)
    def _():
        def fill(j, c):
            v4 = v_ref[0, pl.ds(pl.multiple_of(j * tq, tq), tq), :]
            for h in range(SB_GROUP):
                vexp_ref[j, pl.ds(h * tq, tq), :] = keep_head(head_lanes[h], v4)
            return c
        lax.fori_loop(0, nkb, fill, 0)

    q_h = [[keep_head(m, q_ref[0, pl.ds(u * tq, tq), :]) for m in head_lanes] for u in range(nq)]
    qb = [step * nq + u for u in range(nq)]
    row = lax.broadcasted_iota(jnp.int32, (tq, tq), 0)
    col = lax.broadcasted_iota(jnp.int32, (tq, tq), 1)
    later_or_self = jnp.where(row >= col, 1.0, 0.0).astype(BF16)
    causal = col < row
    heads = range(SB_GROUP)
    sign_bit = jnp.uint32(0x80000000)

    def keys(kb):
        return k_ref[0, pl.ds(pl.multiple_of(kb * tq, tq), tq), :]

    def score(u, h, k4):
        return lax.dot_general(q_h[u][h], k4, (((1,), (1,)), ((), ())), preferred_element_type=F32)

    def softplus2(z):
        neg_abs = lax.bitcast_convert_type(lax.bitcast_convert_type(z, jnp.uint32) | sign_bit, F32)
        return jnp.maximum(z, 0.0) + jnp.log(1.0 + jnp.exp2(neg_abs)) * LOG2E

    def later_sums(sp):
        return jnp.dot(sp.astype(BF16), later_or_self, preferred_element_type=F32)

    has_left = [q > 0 for q in qb]
    left = [jnp.maximum(q - 1, 0) for q in qb]
    k_pair = [(keys(qb[u]), keys(left[u])) for u in range(nq)]
    per_q = 2 * SB_GROUP
    n_chain = nq * per_q
    z_pair, sp_pair, totals, d_pair, w_pair, acc_pair = {}, {}, {}, {}, {}, {}

    def pair_scores(c):
        u, i = divmod(c, per_q)
        z_pair[c] = score(u, i % SB_GROUP, k_pair[u][i // SB_GROUP])

    def pair_softplus(c):
        sp = softplus2(z_pair[c])
        sp_pair[c] = jnp.where(causal, sp, 0.0) if c % per_q < SB_GROUP else sp
        totals[c] = jnp.sum(sp_pair[c], axis=-1, keepdims=True)

    def pair_sums(c):
        d = z_pair[c] - later_sums(sp_pair[c])
        if c % per_q < SB_GROUP:
            d_pair[c] = jnp.where(causal, d, -jnp.inf)
        else:
            d_pair[c] = d - jnp.where(has_left[c // per_q], totals[c - SB_GROUP], jnp.inf)

    def pair_weights(c):
        w_pair[c] = jnp.exp2(d_pair[c]).astype(BF16)
        if c % SB_GROUP == SB_GROUP - 1:
            u, i = divmod(c, per_q)
            w = jnp.concatenate([w_pair[c - SB_GROUP + 1 + h] for h in heads], axis=1)
            kb = qb[u] if i < SB_GROUP else left[u]
            acc_pair[c] = jnp.dot(w, vexp_ref[kb], preferred_element_type=F32)

    stages = (pair_scores, pair_softplus, pair_sums, pair_weights)
    for t in range(n_chain + len(stages) - 1):
        for s, stage in enumerate(stages):
            if 0 <= t - s < n_chain:
                stage(t - s)
    for u in range(nq):
        base = u * per_q
        acc_ref[u] = acc_pair[base + SB_GROUP - 1] + acc_pair[base + per_q - 1]
        for h in heads:
            both = totals[base + h] + jnp.where(has_left[u], totals[base + SB_GROUP + h], 0.0)
            carry_ref[u, h] = jnp.broadcast_to(both, (tq, LANES))

    def stick_left(u):
        least = functools.reduce(jnp.minimum, [carry_ref[u, h] for h in heads])
        return jnp.min(least) < done_ref[0, 0]

    def sweep(u):
        def scores(h, k4):
            z_ref[h] = score(u, h, k4)

        def sums(h):
            z = z_ref[h]
            sp = softplus2(z)
            d_ref[h] = z - later_sums(sp)
            carry = carry_ref[u, h]
            cprev_ref[h] = carry
            carry_ref[u, h] = carry + jnp.sum(sp, axis=-1, keepdims=True)

        def weights(h):
            carry_t = jnp.concatenate([cprev_ref[h]] * (tq // LANES), axis=1)
            w_ref[:, pl.ds(h * tq, tq)] = jnp.exp2(d_ref[h] - carry_t).astype(BF16)

        def values(kb):
            acc_ref[u] += jnp.dot(w_ref[...], vexp_ref[kb], preferred_element_type=F32)

        first = qb[u] - 2
        w_ref[...] = jnp.zeros_like(w_ref)
        k_first = keys(first)
        for h in heads:
            scores(h, k_first)
        k_next = keys(jnp.maximum(first - 1, 0))
        for h in heads:
            sums(h)
            scores(h, k_next)

        def body(c):
            j, _ = c
            values(jnp.minimum(first - j + 2, nkb - 1))
            for h in heads:
                weights(h)
            k_next = keys(jnp.maximum(first - j - 1, 0))
            for h in heads:
                sums(h)
                scores(h, k_next)
            return j + 1, stick_left(u)

        j_end, _ = lax.while_loop(lambda c: (c[0] <= first) & c[1], body, (jnp.int32(1), stick_left(u)))
        values(jnp.minimum(first - j_end + 2, nkb - 1))
        for h in heads:
            weights(h)
        values(first - j_end + 1)

    for u in range(nq):
        pl.when((qb[u] >= 2) & stick_left(u))(functools.partial(sweep, u))
        o_ref[0, pl.ds(u * tq, tq), :] = acc_ref[u].astype(o_ref.dtype)


def _sb_attention(proj3, carry_done, tq):
    b, s, _ = proj3.shape
    w = SB_GROUP_WIDTH
    groups = SB_WIDTH // w
    qoff, koff, voff = (COL_Q * PROJ_TILE // w, COL_K * PROJ_TILE // w, COL_V * PROJ_TILE // w)
    nq = SB_QBLOCKS
    while (s // tq) % nq:
        nq //= 2
    tqs = nq * tq
    return pl.pallas_call(
        functools.partial(_sb_attn_kernel, tq=tq, nkb=s // tq),
        grid=(b, groups, s // tqs),
        in_specs=[
            pl.BlockSpec(memory_space=pltpu.SMEM),
            pl.BlockSpec((1, tqs, w), lambda bi, p, qi: (bi, qi, qoff + p)),
            pl.BlockSpec((1, s, w), lambda bi, p, qi: (bi, 0, koff + p)),
            pl.BlockSpec((1, s, w), lambda bi, p, qi: (bi, 0, voff + p)),
        ],
        out_specs=pl.BlockSpec((1, tqs, w), lambda bi, p, qi: (bi, qi, p)),
        out_shape=jax.ShapeDtypeStruct((b, s, SB_WIDTH), BF16),
        scratch_shapes=[pltpu.VMEM((s // tq, SB_GROUP * tq, w), BF16),
                        pltpu.VMEM((nq, tq, w), F32), pltpu.VMEM((nq, SB_GROUP, tq, LANES), F32),
                        pltpu.VMEM((SB_GROUP, tq, LANES), F32), pltpu.VMEM((SB_GROUP, tq, tq), F32),
                        pltpu.VMEM((SB_GROUP, tq, tq), F32), pltpu.VMEM((tq, SB_GROUP * tq), BF16)],
        compiler_params=_params(("parallel", "parallel", "arbitrary")),
        name="sb_attention",
    )(carry_done, proj3, proj3, proj3)


def _mem_kv_kernel(m_ref, g_ref, w_ref, gk_ref, k_ref, v_ref):
    mn = _rms(m_ref[0], g_ref[...]).astype(BF16)
    kv = jnp.dot(mn, w_ref[...], preferred_element_type=F32)
    k = kv[:, :X_WIDTH]
    ms = jnp.dot((k * k).astype(BF16), _group_mean_matrix(X_WIDTH, X_HEAD_DIM), preferred_element_type=F32)
    k_ref[0] = (k * lax.rsqrt(ms + EPS) * gk_ref[...]).astype(BF16)
    v_ref[0] = kv[:, X_WIDTH:].astype(BF16)


def _mem_kv(mem, g_mem, w_kv, gk_cols):
    b, m, d = mem.shape
    return pl.pallas_call(
        _mem_kv_kernel,
        grid=(b,),
        in_specs=[
            pl.BlockSpec((1, m, d), lambda bi: (bi, 0, 0)),
            pl.BlockSpec((1, d), lambda bi: (0, 0)),
            pl.BlockSpec((d, 2 * X_WIDTH), lambda bi: (0, 0)),
            pl.BlockSpec((1, X_WIDTH), lambda bi: (0, 0)),
        ],
        out_specs=[pl.BlockSpec((1, m, X_WIDTH), lambda bi: (bi, 0, 0))] * 2,
        out_shape=[jax.ShapeDtypeStruct((b, m, X_WIDTH), BF16)] * 2,
        compiler_params=_params(("parallel",)),
        name="mem_kv",
    )(mem, g_mem, w_kv, gk_cols)


def _split_bf16(v):
    hi = v.astype(BF16)
    return hi, (v - hi.astype(F32)).astype(BF16)


def _router(logits):
    lane = lax.broadcasted_iota(jnp.int32, logits.shape, 1).astype(F32)
    ninf = -jnp.inf
    far = float(LANES)
    is_group = (lane >= N_EXPERTS) & (lane < N_EXPERTS + N_GROUPS)
    gl = jnp.where(is_group, logits, ninf)
    gmax = jnp.max(gl, axis=-1, keepdims=True)
    gidx = jnp.min(jnp.where(gl == gmax, lane, far), axis=-1, keepdims=True) - N_EXPERTS
    g_prob = 1.0 / jnp.sum(jnp.exp(gl - gmax), axis=-1, keepdims=True)
    first = gidx * EXPERTS_PER_GROUP
    el = jnp.where((lane >= first) & (lane < first + EXPERTS_PER_GROUP), logits, ninf)
    m1 = jnp.max(el, axis=-1, keepdims=True)
    i1 = jnp.min(jnp.where(el == m1, lane, far), axis=-1, keepdims=True)
    el2 = jnp.where(lane == i1, ninf, el)
    m2 = jnp.max(el2, axis=-1, keepdims=True)
    i2 = jnp.min(jnp.where(el2 == m2, lane, far), axis=-1, keepdims=True)
    e2 = jnp.exp(m2 - m1)
    w1 = 1.0 / (1.0 + e2)
    w2 = e2 / (1.0 + e2)
    combine = g_prob * (jnp.where(lane == i1, w1, 0.0) + jnp.where(lane == i2, w2, 0.0))
    return combine + jnp.where(lane == MOE_GID_LANE, gidx, 0.0)


def _merge_kernel(qx_ref, g0_ref, g1_ref, g2_ref, osb_ref, olru_ref, kx_ref, vx_ref, x_ref,
                  wb_ref, wo_ref, gf_ref, wr_ref, br_ref, x1_ref, hn_ref, comb_ref):
    qx = qx_ref[0]
    heads = []
    for h in range(X_HEADS):
        sl = slice(h * X_HEAD_DIM, (h + 1) * X_HEAD_DIM)
        s = lax.dot_general(qx[:, sl], kx_ref[0][:, sl], (((1,), (1,)), ((), ())),
                            preferred_element_type=F32)
        p = jnp.exp(s - jnp.max(s, axis=-1, keepdims=True))
        p = p / jnp.sum(p, axis=-1, keepdims=True)
        heads.append(jnp.dot(p.astype(BF16), vx_ref[0][:, sl], preferred_element_type=F32))
    o_x = jnp.concatenate(heads, axis=1).astype(BF16)

    merged = g0_ref[0].astype(F32) * jnp.dot(osb_ref[0], wb_ref[0], preferred_element_type=F32)
    merged += g1_ref[0].astype(F32) * jnp.dot(olru_ref[0], wb_ref[1], preferred_element_type=F32)
    merged += g2_ref[0].astype(F32) * jnp.dot(o_x, wb_ref[2], preferred_element_type=F32)
    x1 = x_ref[0] + jnp.dot(merged.astype(BF16), wo_ref[...], preferred_element_type=F32)
    x1_ref[0] = x1

    hn = _rms(x1, gf_ref[...])
    hn_ref[0] = hn.astype(BF16)
    h_hi, h_lo = _split_bf16(hn)
    w_hi, w_lo = _split_bf16(wr_ref[...])
    both = jnp.dot(h_hi, jnp.concatenate([w_hi, w_lo], axis=1), preferred_element_type=F32)
    logits = both[:, :LANES] + both[:, LANES:] + jnp.dot(h_lo, w_hi, preferred_element_type=F32) + br_ref[...]
    comb_ref[0] = _router(logits)


def _merge(proj3, o_sb, o_lru, kx, vx, x, w_branch, w_out, g_ffn, w_router, b_router, tm):
    b, s, d = x.shape
    m = kx.shape[1]
    gate_blk = lambda n: pl.BlockSpec((1, tm, d), lambda bi, si, n=n: (bi, si, COL_GATE * PROJ_TILE // d + n))
    tok512 = lambda: pl.BlockSpec((1, tm, SB_WIDTH), lambda bi, si: (bi, si, 0))
    const2 = lambda shape: pl.BlockSpec(shape, lambda bi, si: (0, 0))
    return pl.pallas_call(
        _merge_kernel,
        grid=(b, s // tm),
        in_specs=[
            pl.BlockSpec((1, tm, X_WIDTH), lambda bi, si: (bi, si, COL_QX)),
            gate_blk(0), gate_blk(1), gate_blk(2),
            tok512(), tok512(),
            pl.BlockSpec((1, m, X_WIDTH), lambda bi, si: (bi, 0, 0)),
            pl.BlockSpec((1, m, X_WIDTH), lambda bi, si: (bi, 0, 0)),
            pl.BlockSpec((1, tm, d), lambda bi, si: (bi, si, 0)),
            pl.BlockSpec((N_BRANCH, SB_WIDTH, d), lambda bi, si: (0, 0, 0)),
            const2((d, d)), const2((1, d)), const2((d, LANES)), const2((1, LANES)),
        ],
        out_specs=[
            pl.BlockSpec((1, tm, d), lambda bi, si: (bi, si, 0)),
            pl.BlockSpec((1, tm, d), lambda bi, si: (bi, si, 0)),
            pl.BlockSpec((1, tm, LANES), lambda bi, si: (bi, si, 0)),
        ],
        out_shape=[
            jax.ShapeDtypeStruct((b, s, d), F32),
            jax.ShapeDtypeStruct((b, s, d), BF16),
            jax.ShapeDtypeStruct((b, s, LANES), F32),
        ],
        compiler_params=_params(("parallel", "parallel")),
        name="merge_router",
    )(proj3, proj3, proj3, proj3, o_sb, o_lru, kx, vx, x, w_branch, w_out, g_ffn, w_router, b_router)


def _moe_sort(hn_ref, comb_ref, xs_ref, cs_ref, ys_ref, pt_ref, seg_ref, *, tt, rows, sub):
    comb = comb_ref[...]
    comb_t = comb.T
    gid_row = comb_t[MOE_GID_LANE:MOE_GID_LANE + 1, :]
    group_of_row = lax.broadcasted_iota(jnp.int32, (SUBLANES, tt), 0).astype(F32)
    member = jnp.where(group_of_row == gid_row, 1.0, 0.0)
    r = lax.broadcasted_iota(jnp.int32, (MXU_TILE, MXU_TILE), 0)
    c = lax.broadcasted_iota(jnp.int32, (MXU_TILE, MXU_TILE), 1)
    upto = jnp.where(r <= c, 1.0, 0.0).astype(BF16)
    before = jnp.zeros((SUBLANES, 1), F32)
    ranks = []
    for blk in range(tt // MXU_TILE):
        m = member[:, blk * MXU_TILE:(blk + 1) * MXU_TILE]
        incl = jnp.dot(m.astype(BF16), upto, preferred_element_type=F32) + before
        ranks.append(jnp.sum(m * (incl - m), axis=0, keepdims=True))
        before = before + jnp.sum(m, axis=1, keepdims=True)
    pos_row = jnp.concatenate(ranks, axis=1)
    start = jnp.int32(0)
    for g in range(N_GROUPS):
        count = jnp.sum(member[g:g + 1, :]).astype(jnp.int32)
        padded = lax.shift_left(lax.shift_right_logical(count + (MOE_PAD - 1), MOE_PAD_LOG2), MOE_PAD_LOG2)
        seg_ref[g] = start
        seg_ref[N_GROUPS + g] = padded
        pos_row = pos_row + member[g:g + 1, :] * start.astype(F32)
        start = start + padded

    hi_lo = jnp.concatenate(_split_bf16(comb), axis=1)
    hn = hn_ref[...]
    for blk in range(rows // LANES):
        rr = (lax.broadcasted_iota(jnp.int32, (LANES, tt), 0) + blk * LANES).astype(F32)
        p = jnp.where(rr == pos_row, 1.0, 0.0).astype(BF16)
        sl = pl.ds(blk * LANES, LANES)
        xs_ref[sl, :] = jnp.dot(p, hn, preferred_element_type=F32).astype(BF16)
        hl = jnp.dot(p, hi_lo, preferred_element_type=F32)
        cs_ref[sl, :] = hl[:, :LANES] + hl[:, LANES:]
    tail = pl.ds(rows, sub)
    xs_ref[tail, :] = jnp.zeros((sub, xs_ref.shape[1]), BF16)
    cs_ref[tail, :] = jnp.zeros((sub, LANES), F32)
    ys_ref[...] = jnp.zeros_like(ys_ref)

    sub_t = lax.broadcasted_iota(jnp.int32, (LANES, tt), 0)
    pos_col = jnp.where(sub_t == MOE_POS_LANE, pos_row, comb_t).T[:, MOE_POS_LANE:MOE_POS_LANE + 1]
    for blk in range(tt // LANES):
        cc = lax.broadcasted_iota(jnp.int32, (LANES, rows), 1).astype(F32)
        sl = pl.ds(blk * LANES, LANES)
        pt_ref[sl, :] = jnp.where(cc == pos_col[blk * LANES:(blk + 1) * LANES, :], 1.0, 0.0).astype(BF16)


def _moe_kernel(hn_ref, comb_ref, x1_ref, wgu_ref, wd_ref, o_ref, xs_ref, cs_ref, ys_ref, pt_ref, seg_ref,
                *, tt, rows, sub):
    c = pl.program_id(1)

    @pl.when(c == 0)
    def _():
        _moe_sort(hn_ref, comb_ref, xs_ref, cs_ref, ys_ref, pt_ref, seg_ref, tt=tt, rows=rows, sub=sub)

    group = lax.shift_right_logical(c, (EXPERTS_PER_GROUP // MOE_CHUNK).bit_length() - 1)
    start = seg_ref[group]
    padded = seg_ref[N_GROUPS + group]
    n_sub = sum((padded > k * sub).astype(jnp.int32) for k in range(-(-rows // sub)))
    lane = lax.broadcasted_iota(jnp.int32, (sub, LANES), 1)

    def sub_tile(s, carry):
        sl = pl.ds(pl.multiple_of(start + s * sub, MOE_PAD), sub)
        x = xs_ref[sl, :]
        cw = cs_ref[sl, :]
        acc = jnp.zeros((sub, ys_ref.shape[1]), F32)
        gate_up = {}
        for step in range(MOE_CHUNK + 1):
            if step < MOE_CHUNK:
                gate_up[step] = jnp.dot(x, wgu_ref[step], preferred_element_type=F32)
            if step > 0:
                k = step - 1
                gate, up = gate_up[k][:, :EXPERT_FF], gate_up[k][:, EXPERT_FF:]
                weight = jnp.sum(jnp.where(lane == c * MOE_CHUNK + k, cw, 0.0), axis=-1, keepdims=True)
                act = jax.nn.silu(gate) * up * weight
                acc = acc + jnp.dot(act.astype(BF16), wd_ref[k], preferred_element_type=F32)
        ys_ref[sl, :] = acc.astype(BF16)
        return carry

    lax.fori_loop(0, n_sub, sub_tile, 0)

    @pl.when(c == pl.num_programs(1) - 1)
    def _():
        o_ref[...] = x1_ref[...] + jnp.dot(pt_ref[...], ys_ref[pl.ds(0, rows), :], preferred_element_type=F32)


def _moe(hn2, comb2, x1_2, w_gu, w_down, tt):
    t, d = hn2.shape
    rows = -(-(tt + N_GROUPS * MOE_PAD) // LANES) * LANES
    spread = 3.5 * (tt * (N_GROUPS - 1)) ** 0.5 / N_GROUPS
    sub = min(rows, -(-int(tt / N_GROUPS + spread) // MOE_PAD) * MOE_PAD)
    return pl.pallas_call(
        functools.partial(_moe_kernel, tt=tt, rows=rows, sub=sub),
        grid=(t // tt, N_EXPERTS // MOE_CHUNK),
        in_specs=[
            pl.BlockSpec((tt, d), lambda i, c: (i, 0)),
            pl.BlockSpec((tt, LANES), lambda i, c: (i, 0)),
            pl.BlockSpec((tt, d), lambda i, c: (i, 0)),
            pl.BlockSpec((MOE_CHUNK, d, 2 * EXPERT_FF), lambda i, c: (c, 0, 0)),
            pl.BlockSpec((MOE_CHUNK, EXPERT_FF, d), lambda i, c: (c, 0, 0)),
        ],
        out_specs=pl.BlockSpec((tt, d), lambda i, c: (i, 0)),
        out_shape=jax.ShapeDtypeStruct((t, d), F32),
        scratch_shapes=[pltpu.VMEM((rows + sub, d), BF16), pltpu.VMEM((rows + sub, LANES), F32),
                        pltpu.VMEM((rows + sub, d), BF16), pltpu.VMEM((tt, rows), BF16),
                        pltpu.SMEM((2 * N_GROUPS,), jnp.int32)],
        compiler_params=_params(("parallel", "arbitrary"), MOE_VMEM_LIMIT),
        name="moe",
    )(hn2, comb2, x1_2, w_gu, w_down)


def _block_diag(w):
    n, bd, _ = w.shape
    eye = jnp.eye(n, dtype=w.dtype)
    return jnp.einsum("nij,nm->nimj", w, eye).reshape(n * bd, n * bd)


def _tile(n, pref):
    while n % pref:
        pref //= 2
    return pref


def _layer(x, mem, g_mix, w_in, g_q_sb, g_k_sb, conv_w, conv_b, lru_w_a, lru_b_a, lru_w_i, lru_b_i,
           lru_lambda, g_mem, w_mem_kv, g_q_x, g_k_x, w_branch, w_out, g_ffn, w_group, b_group,
           w_expert, b_expert, w_gate, w_up, w_down):
    b, s, d = x.shape
    t = b * s
    row = lambda v: v.reshape(1, -1).astype(F32)

    ones = jnp.ones((PROJ_TILE,), F32)
    col_gain = jnp.concatenate([
        jnp.tile(g_q_sb, SB_WIDTH // SB_HEAD_DIM) * (SB_HEAD_DIM ** -0.5 * LOG2E),
        jnp.tile(g_k_sb, SB_WIDTH // SB_HEAD_DIM),
        ones, ones, ones,
        jnp.tile(g_q_x, X_HEADS) * X_HEAD_DIM ** -0.5,
    ] + [ones] * 6).reshape(1, -1)
    w_router = jnp.zeros((d, LANES), F32).at[:, :N_EXPERTS].set(w_expert)
    w_router = w_router.at[:, N_EXPERTS:N_EXPERTS + N_GROUPS].set(w_group)
    b_router = jnp.zeros((1, LANES), F32).at[0, :N_EXPERTS].set(b_expert)
    b_router = b_router.at[0, N_EXPERTS:N_EXPERTS + N_GROUPS].set(b_group)
    w_gu = jnp.concatenate([w_gate, w_up], axis=-1).reshape(N_EXPERTS, d, 2 * EXPERT_FF).astype(BF16)
    w_dn = w_down.reshape(N_EXPERTS, EXPERT_FF, d).astype(BF16)

    proj, o_lru = _in_proj(x.reshape(t, d), row(g_mix), w_in.astype(BF16), col_gain, conv_w, row(conv_b),
                           _block_diag(lru_w_a).astype(BF16), row(lru_b_a), _block_diag(lru_w_i).astype(BF16),
                           row(lru_b_i), row(lru_lambda), _tile(s, 512), s)
    proj3 = proj.reshape(b, s, -1)
    o_lru = o_lru.reshape(b, s, LRU_WIDTH)
    q_gain = jnp.abs(col_gain[0, :PROJ_TILE])
    z_max = 1.05 * SB_HEAD_DIM * jnp.max(q_gain) * jnp.max(jnp.abs(g_k_sb))
    carry_done = (z_max + BF16_ZERO_EXP).reshape(1, 1).astype(F32)
    o_sb = _sb_attention(proj3, carry_done, _tile(s, 256))
    kx, vx = _mem_kv(mem, row(g_mem), w_mem_kv.astype(BF16), row(jnp.tile(g_k_x, X_HEADS)))
    x1, hn, comb = _merge(proj3, o_sb, o_lru, kx, vx, x, w_branch.astype(BF16), w_out.astype(BF16),
                          row(g_ffn), w_router, b_router, _tile(s, 512))
    out = _moe(hn.reshape(t, d), comb.reshape(t, LANES), x1.reshape(t, d), w_gu, w_dn, _tile(t, 1024))
    return out.reshape(b, s, d)


def kernel(x, mem, g_mix, w_in, g_q_sb, g_k_sb, conv_w, conv_b, lru_w_a, lru_b_a, lru_w_i, lru_b_i, lru_lambda, g_mem, w_mem_kv, g_q_x, g_k_x, w_branch, w_out, g_ffn, w_group, b_group, w_expert, b_expert, w_gate, w_up, w_down):
    params = (g_mix, w_in, g_q_sb, g_k_sb, conv_w, conv_b, lru_w_a, lru_b_a, lru_w_i, lru_b_i, lru_lambda,
              g_mem, w_mem_kv, g_q_x, g_k_x, w_branch, w_out, g_ffn, w_group, b_group, w_expert, b_expert,
              w_gate, w_up, w_down)
    for layer in range(g_mix.shape[0]):
        x = _layer(x, mem, *[p[layer] for p in params])
    return x
```

```python
import functools

import jax
import jax.numpy as jnp
from jax import lax
from jax.experimental import pallas as pl
from jax.experimental.pallas import tpu as pltpu

F32 = jnp.float32
BF16 = jnp.bfloat16

EPS = 1e-6
SB_HEAD_DIM = 64
SB_WIDTH = 512
LRU_WIDTH = 512
LRU_BLOCKS = 8
CONV_WIDTH = 4
LRU_C = 8.0
X_HEADS = 4
X_WIDTH = 512
X_HEAD_DIM = 128
N_BRANCH = 3
N_GROUPS = 4
EXPERTS_PER_GROUP = 8
N_EXPERTS = N_GROUPS * EXPERTS_PER_GROUP
EXPERT_FF = 256

LOG2E = 1.4426950408889634
SB_GROUP = 4
SB_GROUP_WIDTH = SB_GROUP * SB_HEAD_DIM
SB_QBLOCKS = 4
BF16_ZERO_EXP = 160.0

MOE_CHUNK = 8
MXU_TILE = 256
MOE_PAD_LOG2 = 4
MOE_PAD = 1 << MOE_PAD_LOG2
MOE_GID_LANE = N_EXPERTS
MOE_POS_LANE = N_EXPERTS + 1
SUBLANES = 8

LANES = 128
BF16_ROWS = 16
VMEM_LIMIT = 56 * 1024 * 1024
MOE_VMEM_LIMIT = 60 * 1024 * 1024

PROJ_TILE = 512
KIND_HEAD64, KIND_HEAD128, KIND_PLAIN, KIND_SIGMOID, KIND_LRU = 0, 1, 2, 3, 4
PROJ_KINDS = (KIND_HEAD64, KIND_HEAD64, KIND_PLAIN, KIND_LRU, KIND_LRU, KIND_HEAD128) + (KIND_SIGMOID,) * 6
W_COL_XL, W_COL_YL = 3, 4
PROJ_TILES_BEFORE_LRU = 2
PROJ_OUT_COL = {j: sum(k != KIND_LRU for k in PROJ_KINDS[:j]) for j, kind in enumerate(PROJ_KINDS) if kind != KIND_LRU}
COL_Q, COL_K, COL_V, COL_QX, COL_GATE = (PROJ_OUT_COL[j] for j in (0, 1, 2, 5, 6))


def _params(sem, vmem_limit=VMEM_LIMIT):
    return pltpu.CompilerParams(dimension_semantics=sem, vmem_limit_bytes=vmem_limit)


def _rms(xf, g):
    return xf * lax.rsqrt(jnp.mean(xf * xf, axis=-1, keepdims=True) + EPS) * g


def _sigmoid(v):
    return 0.5 * jnp.tanh(0.5 * v) + 0.5


def _group_mean_matrix(n, group):
    shift = group.bit_length() - 1
    r = lax.shift_right_logical(lax.broadcasted_iota(jnp.int32, (n, n), 0), shift)
    c = lax.shift_right_logical(lax.broadcasted_iota(jnp.int32, (n, n), 1), shift)
    return jnp.where(r == c, 1.0 / group, 0.0).astype(BF16)


def _proj_kernel(x_ref, g_ref, w_ref, cg_ref, cw_ref, cb_ref, wa_ref, ba_ref, wi_ref, bi_ref, lam_ref,
                 o_ref, olru_ref, xbuf_ref, h_ref, *, tm, tiles_per_seq):
    seq_tile = pl.program_id(0) % tiles_per_seq

    @pl.when(seq_tile == 0)
    def _():
        xbuf_ref[pl.ds(0, SUBLANES), :] = jnp.zeros((SUBLANES, LRU_WIDTH), F32)
        h_ref[...] = jnp.zeros_like(h_ref)

    @pl.when(seq_tile > 0)
    def _():
        xbuf_ref[pl.ds(0, SUBLANES), :] = xbuf_ref[pl.ds(tm, SUBLANES), :]

    hn = _rms(x_ref[...], g_ref[...]).astype(BF16)
    group_mean = {KIND_HEAD64: _group_mean_matrix(PROJ_TILE, SB_HEAD_DIM),
                  KIND_HEAD128: _group_mean_matrix(PROJ_TILE, X_HEAD_DIM)}

    def column_tile(j):
        return jnp.dot(hn, w_ref[:, pl.ds(j * PROJ_TILE, PROJ_TILE)], preferred_element_type=F32)

    def stored_tile(j):
        kind, acc = PROJ_KINDS[j], column_tile(j)
        if kind in group_mean:
            ms = jnp.dot((acc * acc).astype(BF16), group_mean[kind], preferred_element_type=F32)
            acc = acc * lax.rsqrt(ms + EPS) * cg_ref[:, pl.ds(j * PROJ_TILE, PROJ_TILE)]
        elif kind == KIND_SIGMOID:
            acc = _sigmoid(acc)
        o_ref[:, pl.ds(PROJ_OUT_COL[j] * PROJ_TILE, PROJ_TILE)] = acc.astype(o_ref.dtype)

    stored = [j for j, kind in enumerate(PROJ_KINDS) if kind != KIND_LRU]
    x_lru, y_lru = column_tile(W_COL_XL), column_tile(W_COL_YL)
    for j in stored[:PROJ_TILES_BEFORE_LRU]:
        stored_tile(j)

    xbuf_ref[pl.ds(SUBLANES, tm), :] = x_lru
    xc = cb_ref[...] + jnp.zeros((tm, LRU_WIDTH), F32)
    for j in range(CONV_WIDTH):
        xc = xc + cw_ref[pl.ds(j, 1), :] * xbuf_ref[pl.ds(SUBLANES - (CONV_WIDTH - 1) + j, tm), :]
    xcb = xc.astype(BF16)
    r = _sigmoid(jnp.dot(xcb, wa_ref[...], preferred_element_type=F32) + ba_ref[...])
    gi = _sigmoid(jnp.dot(xcb, wi_ref[...], preferred_element_type=F32) + bi_ref[...])

    nlam = -lam_ref[...]
    sp_nlam = jnp.maximum(nlam, 0.0) + jnp.log(1.0 + jnp.exp(-jnp.abs(nlam)))
    state = h_ref[pl.ds(0, 1), :]
    rest = stored[PROJ_TILES_BEFORE_LRU:]
    chunk = tm // len(rest)
    for c, j in enumerate(rest):
        rows = slice(c * chunk, (c + 1) * chunk)
        log_a = (-LRU_C * r[rows]) * sp_nlam
        a = jnp.exp(log_a)
        th = jnp.tanh(log_a)
        n = -2.0 * th
        coef = jnp.where(n > 0.0, n * lax.rsqrt(n * (1.0 - th)), 0.0)
        b = coef * (gi[rows] * xc[rows])
        a = a.reshape(chunk // SUBLANES, SUBLANES, LRU_WIDTH)
        b = b.reshape(chunk // SUBLANES, SUBLANES, LRU_WIDTH)
        row_in_vreg = lax.broadcasted_iota(jnp.int32, a.shape, 1)
        d = 1
        while d < SUBLANES:
            keep = row_in_vreg >= d
            b = jnp.where(keep, a * pltpu.roll(b, d, 1), 0.0) + b
            a = jnp.where(keep, a * pltpu.roll(a, d, 1), a)
            d *= 2
        pieces = []
        for i in range(chunk // SUBLANES):
            pieces.append(b[i] + a[i] * state)
            state = pieces[-1][SUBLANES - 1:SUBLANES, :]
        h = jnp.concatenate(pieces, axis=0)
        olru_ref[pl.ds(c * chunk, chunk), :] = (h * jax.nn.gelu(y_lru[rows])).astype(olru_ref.dtype)
        stored_tile(j)
    h_ref[...] = jnp.broadcast_to(state, h_ref.shape)


def _in_proj(x2, g_mix, w_cat, col_gain, conv_w, conv_b, wa_bd, b_a, wi_bd, b_i, lam, tm, seq_len):
    t, d = x2.shape
    ncols = len([k for k in PROJ_KINDS if k != KIND_LRU]) * PROJ_TILE
    resident = lambda shape: pl.BlockSpec(shape, lambda i: (0, 0), pipeline_mode=pl.Buffered(1))
    vec = lambda: resident((1, LRU_WIDTH))
    mat = lambda: resident((LRU_WIDTH, LRU_WIDTH))
    return pl.pallas_call(
        functools.partial(_proj_kernel, tm=tm, tiles_per_seq=seq_len // tm),
        grid=(t // tm,),
        in_specs=[
            pl.BlockSpec((tm, d), lambda i: (i, 0)),
            resident((1, d)),
            resident((d, w_cat.shape[1])),
            resident((1, w_cat.shape[1])),
            resident((CONV_WIDTH, LRU_WIDTH)), vec(), mat(), vec(), mat(), vec(), vec(),
        ],
        out_specs=[pl.BlockSpec((tm, ncols), lambda i: (i, 0)),
                   pl.BlockSpec((tm, LRU_WIDTH), lambda i: (i, 0))],
        out_shape=[jax.ShapeDtypeStruct((t, ncols), BF16), jax.ShapeDtypeStruct((t, LRU_WIDTH), BF16)],
        scratch_shapes=[pltpu.VMEM((tm + SUBLANES, LRU_WIDTH), F32), pltpu.VMEM((SUBLANES, LRU_WIDTH), F32)],
        compiler_params=_params(("arbitrary",)),
        name="in_proj_rglru",
    )(x2, g_mix, w_cat, col_gain, conv_w, conv_b, wa_bd, b_a, wi_bd, b_i, lam)


def _sb_attn_kernel(done_ref, q_ref, k_ref, v_ref, o_ref, vexp_ref, acc_ref, carry_ref, cprev_ref, z_ref, d_ref, w_ref,
                    *, tq, nkb):
    nq = acc_ref.shape[0]
    step = pl.program_id(2)
    lane = lax.broadcasted_iota(jnp.int32, (tq, SB_GROUP_WIDTH), 1)
    head_lanes = [(lane >= h * SB_HEAD_DIM) & (lane < (h + 1) * SB_HEAD_DIM) for h in range(SB_GROUP)]
    keep_head = lambda m, a: jnp.where(m, a.astype(F32), 0.0).astype(BF16)

    @pl.when(step == 0)
    def _():
        def fill(j, c):
            v4 = v_ref[0, pl.ds(pl.multiple_of(j * tq, tq), tq), :]
            for h in range(SB_GROUP):
                vexp_ref[j, pl.ds(h * tq, tq), :] = keep_head(head_lanes[h], v4)
            return c
        lax.fori_loop(0, nkb, fill, 0)

    q_h = [[keep_head(m, q_ref[0, pl.ds(u * tq, tq), :]) for m in head_lanes] for u in range(nq)]
    qb = [step * nq + u for u in range(nq)]
    row = lax.broadcasted_iota(jnp.int32, (tq, tq), 0)
    col = lax.broadcasted_iota(jnp.int32, (tq, tq), 1)
    later_or_self = jnp.where(row >= col, 1.0, 0.0).astype(BF16)
    causal = col < row
    heads = range(SB_GROUP)
    sign_bit = jnp.uint32(0x80000000)

    def keys(kb):
        return k_ref[0, pl.ds(pl.multiple_of(kb * tq, tq), tq), :]

    def score(u, h, k4):
        return lax.dot_general(q_h[u][h], k4, (((1,), (1,)), ((), ())), preferred_element_type=F32)

    def softplus2(z):
        neg_abs = lax.bitcast_convert_type(lax.bitcast_convert_type(z, jnp.uint32) | sign_bit, F32)
        return jnp.maximum(z, 0.0) + jnp.log(1.0 + jnp.exp2(neg_abs)) * LOG2E

    def later_sums(sp):
        return jnp.dot(sp.astype(BF16), later_or_self, preferred_element_type=F32)

    has_left = [q > 0 for q in qb]
    left = [jnp.maximum(q - 1, 0) for q in qb]
    k_pair = [(keys(qb[u]), keys(left[u])) for u in range(nq)]
    per_q = 2 * SB_GROUP
    n_chain = nq * per_q
    z_pair, sp_pair, totals, d_pair, w_pair, acc_pair = {}, {}, {}, {}, {}, {}

    def pair_scores(c):
        u, i = divmod(c, per_q)
        z_pair[c] = score(u, i % SB_GROUP, k_pair[u][i // SB_GROUP])

    def pair_softplus(c):
        sp = softplus2(z_pair[c])
        sp_pair[c] = jnp.where(causal, sp, 0.0) if c % per_q < SB_GROUP else sp
        totals[c] = jnp.sum(sp_pair[c], axis=-1, keepdims=True)

    def pair_sums(c):
        d = z_pair[c] - later_sums(sp_pair[c])
        if c % per_q < SB_GROUP:
            d_pair[c] = jnp.where(causal, d, -jnp.inf)
        else:
            d_pair[c] = d - jnp.where(has_left[c // per_q], totals[c - SB_GROUP], jnp.inf)

    def pair_weights(c):
        w_pair[c] = jnp.exp2(d_pair[c]).astype(BF16)
        if c % SB_GROUP == SB_GROUP - 1:
            u, i = divmod(c, per_q)
            w = jnp.concatenate([w_pair[c - SB_GROUP + 1 + h] for h in heads], axis=1)
            kb = qb[u] if i < SB_GROUP else left[u]
            acc_pair[c] = jnp.dot(w, vexp_ref[kb], preferred_element_type=F32)

    stages = (pair_scores, pair_softplus, pair_sums, pair_weights)
    for t in range(n_chain + len(stages) - 1):
        for s, stage in enumerate(stages):
            if 0 <= t - s < n_chain:
                stage(t - s)
    for u in range(nq):
        base = u * per_q
        acc_ref[u] = acc_pair[base + SB_GROUP - 1] + acc_pair[base + per_q - 1]
        for h in heads:
            both = totals[base + h] + jnp.where(has_left[u], totals[base + SB_GROUP + h], 0.0)
            carry_ref[u, h] = jnp.broadcast_to(both, (tq, LANES))

    def stick_left(u):
        least = functools.reduce(jnp.minimum, [carry_ref[u, h] for h in heads])
        return jnp.min(least) < done_ref[0, 0]

    def sweep(u):
        def scores(h, k4):
            z_ref[h] = score(u, h, k4)

        def sums(h):
            z = z_ref[h]
            sp = softplus2(z)
            d_ref[h] = z - later_sums(sp)
            carry = carry_ref[u, h]
            cprev_ref[h] = carry
            carry_ref[u, h] = carry + jnp.sum(sp, axis=-1, keepdims=True)

        def weights(h):
            carry_t = jnp.concatenate([cprev_ref[h]] * (tq // LANES), axis=1)
            w_ref[:, pl.ds(h * tq, tq)] = jnp.exp2(d_ref[h] - carry_t).astype(BF16)

        def values(kb):
            acc_ref[u] += jnp.dot(w_ref[...], vexp_ref[kb], preferred_element_type=F32)

        first = qb[u] - 2
        w_ref[...] = jnp.zeros_like(w_ref)
        k_first = keys(first)
        for h in heads:
            scores(h, k_first)
        k_next = keys(jnp.maximum(first - 1, 0))
        for h in heads:
            sums(h)
            scores(h, k_next)

        def body(c):
            j, _ = c
            values(jnp.minimum(first - j + 2, nkb - 1))
            for h in heads:
                weights(h)
            k_next = keys(jnp.maximum(first - j - 1, 0))
            for h in heads:
                sums(h)
                scores(h, k_next)
            return j + 1, stick_left(u)

        j_end, _ = lax.while_loop(lambda c: (c[0] <= first) & c[1], body, (jnp.int32(1), stick_left(u)))
        values(jnp.minimum(first - j_end + 2, nkb - 1))
        for h in heads:
            weights(h)
        values(first - j_end + 1)

    for u in range(nq):
        pl.when((qb[u] >= 2) & stick_left(u))(functools.partial(sweep, u))
        o_ref[0, pl.ds(u * tq, tq), :] = acc_ref[u].astype(o_ref.dtype)


def _sb_attention(proj3, carry_done, tq):
    b, s, _ = proj3.shape
    w = SB_GROUP_WIDTH
    groups = SB_WIDTH // w
    qoff, koff, voff = (COL_Q * PROJ_TILE // w, COL_K * PROJ_TILE // w, COL_V * PROJ_TILE // w)
    nq = SB_QBLOCKS
    while (s // tq) % nq:
        nq //= 2
    tqs = nq * tq
    return pl.pallas_call(
        functools.partial(_sb_attn_kernel, tq=tq, nkb=s // tq),
        grid=(b, groups, s // tqs),
        in_specs=[
            pl.BlockSpec(memory_space=pltpu.SMEM),
            pl.BlockSpec((1, tqs, w), lambda bi, p, qi: (bi, qi, qoff + p)),
            pl.BlockSpec((1, s, w), lambda bi, p, qi: (bi, 0, koff + p)),
            pl.BlockSpec((1, s, w), lambda bi, p, qi: (bi, 0, voff + p)),
        ],
        out_specs=pl.BlockSpec((1, tqs, w), lambda bi, p, qi: (bi, qi, p)),
        out_shape=jax.ShapeDtypeStruct((b, s, SB_WIDTH), BF16),
        scratch_shapes=[pltpu.VMEM((s // tq, SB_GROUP * tq, w), BF16),
                        pltpu.VMEM((nq, tq, w), F32), pltpu.VMEM((nq, SB_GROUP, tq, LANES), F32),
                        pltpu.VMEM((SB_GROUP, tq, LANES), F32), pltpu.VMEM((SB_GROUP, tq, tq), F32),
                        pltpu.VMEM((SB_GROUP, tq, tq), F32), pltpu.VMEM((tq, SB_GROUP * tq), BF16)],
        compiler_params=_params(("parallel", "parallel", "arbitrary")),
        name="sb_attention",
    )(carry_done, proj3, proj3, proj3)


def _mem_kv_kernel(m_ref, g_ref, w_ref, gk_ref, k_ref, v_ref):
    mn = _rms(m_ref[0], g_ref[...]).astype(BF16)
    kv = jnp.dot(mn, w_ref[...], preferred_element_type=F32)
    k = kv[:, :X_WIDTH]
    ms = jnp.dot((k * k).astype(BF16), _group_mean_matrix(X_WIDTH, X_HEAD_DIM), preferred_element_type=F32)
    k_ref[0] = (k * lax.rsqrt(ms + EPS) * gk_ref[...]).astype(BF16)
    v_ref[0] = kv[:, X_WIDTH:].astype(BF16)


def _mem_kv(mem, g_mem, w_kv, gk_cols):
    b, m, d = mem.shape
    return pl.pallas_call(
        _mem_kv_kernel,
        grid=(b,),
        in_specs=[
            pl.BlockSpec((1, m, d), lambda bi: (bi, 0, 0)),
            pl.BlockSpec((1, d), lambda bi: (0, 0)),
            pl.BlockSpec((d, 2 * X_WIDTH), lambda bi: (0, 0)),
            pl.BlockSpec((1, X_WIDTH), lambda bi: (0, 0)),
        ],
        out_specs=[pl.BlockSpec((1, m, X_WIDTH), lambda bi: (bi, 0, 0))] * 2,
        out_shape=[jax.ShapeDtypeStruct((b, m, X_WIDTH), BF16)] * 2,
        compiler_params=_params(("parallel",)),
        name="mem_kv",
    )(mem, g_mem, w_kv, gk_cols)


def _split_bf16(v):
    hi = v.astype(BF16)
    return hi, (v - hi.astype(F32)).astype(BF16)


def _router(logits):
    lane = lax.broadcasted_iota(jnp.int32, logits.shape, 1).astype(F32)
    ninf = -jnp.inf
    far = float(LANES)
    is_group = (lane >= N_EXPERTS) & (lane < N_EXPERTS + N_GROUPS)
    gl = jnp.where(is_group, logits, ninf)
    gmax = jnp.max(gl, axis=-1, keepdims=True)
    gidx = jnp.min(jnp.where(gl == gmax, lane, far), axis=-1, keepdims=True) - N_EXPERTS
    g_prob = 1.0 / jnp.sum(jnp.exp(gl - gmax), axis=-1, keepdims=True)
    first = gidx * EXPERTS_PER_GROUP
    el = jnp.where((lane >= first) & (lane < first + EXPERTS_PER_GROUP), logits, ninf)
    m1 = jnp.max(el, axis=-1, keepdims=True)
    i1 = jnp.min(jnp.where(el == m1, lane, far), axis=-1, keepdims=True)
    el2 = jnp.where(lane == i1, ninf, el)
    m2 = jnp.max(el2, axis=-1, keepdims=True)
    i2 = jnp.min(jnp.where(el2 == m2, lane, far), axis=-1, keepdims=True)
    e2 = jnp.exp(m2 - m1)
    w1 = 1.0 / (1.0 + e2)
    w2 = e2 / (1.0 + e2)
    combine = g_prob * (jnp.where(lane == i1, w1, 0.0) + jnp.where(lane == i2, w2, 0.0))
    return combine + jnp.where(lane == MOE_GID_LANE, gidx, 0.0)


def _merge_kernel(qx_ref, g0_ref, g1_ref, g2_ref, osb_ref, olru_ref, kx_ref, vx_ref, x_ref,
                  wb_ref, wo_ref, gf_ref, wr_ref, br_ref, x1_ref, hn_ref, comb_ref):
    tm = x_ref.shape[1]
    n_part = 2 if tm % (2 * BF16_ROWS) == 0 else 1
    part = tm // n_part
    head_cols = [slice(h * X_HEAD_DIM, (h + 1) * X_HEAD_DIM) for h in range(X_HEADS)]
    w_hi, w_lo = _split_bf16(wr_ref[...])
    w_hi_lo = jnp.concatenate([w_hi, w_lo], axis=1)
    val = [{} for _ in range(n_part)]

    def scores(a, rows):
        v = val[a]
        qx = qx_ref[0, rows, :]
        v["s"] = [lax.dot_general(qx[:, c], kx_ref[0][:, c], (((1,), (1,)), ((), ())),
                                  preferred_element_type=F32) for c in head_cols]
        v["u01"] = (g0_ref[0, rows, :].astype(F32) * jnp.dot(osb_ref[0, rows, :], wb_ref[0],
                                                               preferred_element_type=F32)
                    + g1_ref[0, rows, :].astype(F32) * jnp.dot(olru_ref[0, rows, :], wb_ref[1],
                                                                 preferred_element_type=F32))

    def attend(a, rows):
        v = val[a]
        outs = []
        for s, c in zip(v["s"], head_cols):
            p = jnp.exp(s - jnp.max(s, axis=-1, keepdims=True))
            p = p / jnp.sum(p, axis=-1, keepdims=True)
            outs.append(jnp.dot(p.astype(BF16), vx_ref[0][:, c], preferred_element_type=F32))
        v["o_x"] = jnp.concatenate(outs, axis=1).astype(BF16)

    def project(a, rows):
        v = val[a]
        merged = v["u01"] + g2_ref[0, rows, :].astype(F32) * jnp.dot(v["o_x"], wb_ref[2],
                                                                       preferred_element_type=F32)
        x1 = x_ref[0, rows, :] + jnp.dot(merged.astype(BF16), wo_ref[...], preferred_element_type=F32)
        x1_ref[0, rows, :] = x1
        v["hn"] = _rms(x1, gf_ref[...])
        hn_ref[0, rows, :] = v["hn"].astype(BF16)

    def route(a, rows):
        h_hi, h_lo = _split_bf16(val[a]["hn"])
        both = jnp.dot(h_hi, w_hi_lo, preferred_element_type=F32)
        logits = both[:, :LANES] + both[:, LANES:] + jnp.dot(h_lo, w_hi, preferred_element_type=F32) + br_ref[...]
        comb_ref[0, rows, :] = _router(logits)

    stages = (scores, attend, project, route)
    for t in range(n_part + len(stages) - 1):
        for s, stage in enumerate(stages):
            a = t - s
            if 0 <= a < n_part:
                stage(a, pl.ds(a * part, part))


def _merge(proj3, o_sb, o_lru, kx, vx, x, w_branch, w_out, g_ffn, w_router, b_router, tm):
    b, s, d = x.shape
    m = kx.shape[1]
    gate_blk = lambda n: pl.BlockSpec((1, tm, d), lambda bi, si, n=n: (bi, si, COL_GATE * PROJ_TILE // d + n))
    tok512 = lambda: pl.BlockSpec((1, tm, SB_WIDTH), lambda bi, si: (bi, si, 0))
    const2 = lambda shape: pl.BlockSpec(shape, lambda bi, si: (0, 0), pipeline_mode=pl.Buffered(1))
    return pl.pallas_call(
        _merge_kernel,
        grid=(b, s // tm),
        in_specs=[
            pl.BlockSpec((1, tm, X_WIDTH), lambda bi, si: (bi, si, COL_QX)),
            gate_blk(0), gate_blk(1), gate_blk(2),
            tok512(), tok512(),
            pl.BlockSpec((1, m, X_WIDTH), lambda bi, si: (bi, 0, 0)),
            pl.BlockSpec((1, m, X_WIDTH), lambda bi, si: (bi, 0, 0)),
            pl.BlockSpec((1, tm, d), lambda bi, si: (bi, si, 0)),
            pl.BlockSpec((N_BRANCH, SB_WIDTH, d), lambda bi, si: (0, 0, 0), pipeline_mode=pl.Buffered(1)),
            const2((d, d)), const2((1, d)), const2((d, LANES)), const2((1, LANES)),
        ],
        out_specs=[
            pl.BlockSpec((1, tm, d), lambda bi, si: (bi, si, 0)),
            pl.BlockSpec((1, tm, d), lambda bi, si: (bi, si, 0)),
            pl.BlockSpec((1, tm, LANES), lambda bi, si: (bi, si, 0)),
        ],
        out_shape=[
            jax.ShapeDtypeStruct((b, s, d), F32),
            jax.ShapeDtypeStruct((b, s, d), BF16),
            jax.ShapeDtypeStruct((b, s, LANES), F32),
        ],
        compiler_params=_params(("parallel", "parallel")),
        name="merge_router",
    )(proj3, proj3, proj3, proj3, o_sb, o_lru, kx, vx, x, w_branch, w_out, g_ffn, w_router, b_router)


def _moe_sort(hn_ref, comb_ref, xs_ref, cs_ref, ys_ref, pt_ref, seg_ref, *, tt, rows, sub):
    comb = comb_ref[...]
    comb_t = comb.T
    gid_row = comb_t[MOE_GID_LANE:MOE_GID_LANE + 1, :]
    group_of_row = lax.broadcasted_iota(jnp.int32, (SUBLANES, tt), 0).astype(F32)
    member = jnp.where(group_of_row == gid_row, 1.0, 0.0)
    r = lax.broadcasted_iota(jnp.int32, (MXU_TILE, MXU_TILE), 0)
    c = lax.broadcasted_iota(jnp.int32, (MXU_TILE, MXU_TILE), 1)
    upto = jnp.where(r <= c, 1.0, 0.0).astype(BF16)
    before = jnp.zeros((SUBLANES, 1), F32)
    ranks = []
    for blk in range(tt // MXU_TILE):
        m = member[:, blk * MXU_TILE:(blk + 1) * MXU_TILE]
        incl = jnp.dot(m.astype(BF16), upto, preferred_element_type=F32) + before
        ranks.append(jnp.sum(m * (incl - m), axis=0, keepdims=True))
        before = before + jnp.sum(m, axis=1, keepdims=True)
    pos_row = jnp.concatenate(ranks, axis=1)
    start = jnp.int32(0)
    for g in range(N_GROUPS):
        count = jnp.sum(member[g:g + 1, :]).astype(jnp.int32)
        padded = lax.shift_left(lax.shift_right_logical(count + (MOE_PAD - 1), MOE_PAD_LOG2), MOE_PAD_LOG2)
        seg_ref[g] = start
        seg_ref[N_GROUPS + g] = padded
        pos_row = pos_row + member[g:g + 1, :] * start.astype(F32)
        start = start + padded

    hi_lo = jnp.concatenate(_split_bf16(comb), axis=1)
    hn = hn_ref[...]
    for blk in range(rows // LANES):
        rr = (lax.broadcasted_iota(jnp.int32, (LANES, tt), 0) + blk * LANES).astype(F32)
        p = jnp.where(rr == pos_row, 1.0, 0.0).astype(BF16)
        sl = pl.ds(blk * LANES, LANES)
        xs_ref[sl, :] = jnp.dot(p, hn, preferred_element_type=F32).astype(BF16)
        hl = jnp.dot(p, hi_lo, preferred_element_type=F32)
        cs_ref[sl, :] = hl[:, :LANES] + hl[:, LANES:]
    tail = pl.ds(rows, sub)
    xs_ref[tail, :] = jnp.zeros((sub, xs_ref.shape[1]), BF16)
    cs_ref[tail, :] = jnp.zeros((sub, LANES), F32)
    ys_ref[...] = jnp.zeros_like(ys_ref)

    sub_t = lax.broadcasted_iota(jnp.int32, (LANES, tt), 0)
    pos_col = jnp.where(sub_t == MOE_POS_LANE, pos_row, comb_t).T[:, MOE_POS_LANE:MOE_POS_LANE + 1]
    for blk in range(tt // LANES):
        cc = lax.broadcasted_iota(jnp.int32, (LANES, rows), 1).astype(F32)
        sl = pl.ds(blk * LANES, LANES)
        pt_ref[sl, :] = jnp.where(cc == pos_col[blk * LANES:(blk + 1) * LANES, :], 1.0, 0.0).astype(BF16)


def _moe_kernel(hn_ref, comb_ref, x1_ref, wgu_ref, wd_ref, o_ref, xs_ref, cs_ref, ys_ref, pt_ref, seg_ref,
                *, tt, rows, sub):
    c = pl.program_id(1)

    @pl.when(c == 0)
    def _():
        _moe_sort(hn_ref, comb_ref, xs_ref, cs_ref, ys_ref, pt_ref, seg_ref, tt=tt, rows=rows, sub=sub)

    group = lax.shift_right_logical(c, (EXPERTS_PER_GROUP // MOE_CHUNK).bit_length() - 1)
    start = seg_ref[group]
    padded = seg_ref[N_GROUPS + group]
    n_sub = sum((padded > k * sub).astype(jnp.int32) for k in range(-(-rows // sub)))
    lane = lax.broadcasted_iota(jnp.int32, (sub, LANES), 1)

    def sub_tile(s, carry):
        sl = pl.ds(pl.multiple_of(start + s * sub, MOE_PAD), sub)
        x = xs_ref[sl, :]
        cw = cs_ref[sl, :]
        acc = jnp.zeros((sub, ys_ref.shape[1]), F32)
        gate_up = {}
        for step in range(MOE_CHUNK + 1):
            if step < MOE_CHUNK:
                gate_up[step] = jnp.dot(x, wgu_ref[step], preferred_element_type=F32)
            if step > 0:
                k = step - 1
                gate, up = gate_up[k][:, :EXPERT_FF], gate_up[k][:, EXPERT_FF:]
                weight = jnp.sum(jnp.where(lane == c * MOE_CHUNK + k, cw, 0.0), axis=-1, keepdims=True)
                act = jax.nn.silu(gate) * up * weight
                acc = acc + jnp.dot(act.astype(BF16), wd_ref[k], preferred_element_type=F32)
        ys_ref[sl, :] = acc.astype(BF16)
        return carry

    lax.fori_loop(0, n_sub, sub_tile, 0)

    @pl.when(c == pl.num_programs(1) - 1)
    def _():
        o_ref[...] = x1_ref[...] + jnp.dot(pt_ref[...], ys_ref[pl.ds(0, rows), :], preferred_element_type=F32)


def _moe(hn2, comb2, x1_2, w_gu, w_down, tt):
    t, d = hn2.shape
    rows = -(-(tt + N_GROUPS * MOE_PAD) // LANES) * LANES
    spread = 3.5 * (tt * (N_GROUPS - 1)) ** 0.5 / N_GROUPS
    sub = min(rows, -(-int(tt / N_GROUPS + spread) // MOE_PAD) * MOE_PAD)
    return pl.pallas_call(
        functools.partial(_moe_kernel, tt=tt, rows=rows, sub=sub),
        grid=(t // tt, N_EXPERTS // MOE_CHUNK),
        in_specs=[
            pl.BlockSpec((tt, d), lambda i, c: (i, 0)),
            pl.BlockSpec((tt, LANES), lambda i, c: (i, 0)),
            pl.BlockSpec((tt, d), lambda i, c: (i, 0)),
            pl.BlockSpec((MOE_CHUNK, d, 2 * EXPERT_FF), lambda i, c: (c, 0, 0)),
            pl.BlockSpec((MOE_CHUNK, EXPERT_FF, d), lambda i, c: (c, 0, 0)),
        ],
        out_specs=pl.BlockSpec((tt, d), lambda i, c: (i, 0)),
        out_shape=jax.ShapeDtypeStruct((t, d), F32),
        scratch_shapes=[pltpu.VMEM((rows + sub, d), BF16), pltpu.VMEM((rows + sub, LANES), F32),
                        pltpu.VMEM((rows + sub, d), BF16), pltpu.VMEM((tt, rows), BF16),
                        pltpu.SMEM((2 * N_GROUPS,), jnp.int32)],
        compiler_params=_params(("parallel", "arbitrary"), MOE_VMEM_LIMIT),
        name="moe",
    )(hn2, comb2, x1_2, w_gu, w_down)


def _block_diag(w):
    n, bd, _ = w.shape
    eye = jnp.eye(n, dtype=w.dtype)
    return jnp.einsum("nij,nm->nimj", w, eye).reshape(n * bd, n * bd)


def _tile(n, pref):
    while n % pref:
        pref //= 2
    return pref


def _layer(x, mem, g_mix, w_in, g_q_sb, g_k_sb, conv_w, conv_b, lru_w_a, lru_b_a, lru_w_i, lru_b_i,
           lru_lambda, g_mem, w_mem_kv, g_q_x, g_k_x, w_branch, w_out, g_ffn, w_group, b_group,
           w_expert, b_expert, w_gate, w_up, w_down):
    b, s, d = x.shape
    t = b * s
    row = lambda v: v.reshape(1, -1).astype(F32)

    ones = jnp.ones((PROJ_TILE,), F32)
    col_gain = jnp.concatenate([
        jnp.tile(g_q_sb, SB_WIDTH // SB_HEAD_DIM) * (SB_HEAD_DIM ** -0.5 * LOG2E),
        jnp.tile(g_k_sb, SB_WIDTH // SB_HEAD_DIM),
        ones, ones, ones,
        jnp.tile(g_q_x, X_HEADS) * X_HEAD_DIM ** -0.5,
    ] + [ones] * 6).reshape(1, -1)
    w_router = jnp.zeros((d, LANES), F32).at[:, :N_EXPERTS].set(w_expert)
    w_router = w_router.at[:, N_EXPERTS:N_EXPERTS + N_GROUPS].set(w_group)
    b_router = jnp.zeros((1, LANES), F32).at[0, :N_EXPERTS].set(b_expert)
    b_router = b_router.at[0, N_EXPERTS:N_EXPERTS + N_GROUPS].set(b_group)
    w_gu = jnp.concatenate([w_gate, w_up], axis=-1).reshape(N_EXPERTS, d, 2 * EXPERT_FF).astype(BF16)
    w_dn = w_down.reshape(N_EXPERTS, EXPERT_FF, d).astype(BF16)

    proj, o_lru = _in_proj(x.reshape(t, d), row(g_mix), w_in.astype(BF16), col_gain, conv_w, row(conv_b),
                           _block_diag(lru_w_a).astype(BF16), row(lru_b_a), _block_diag(lru_w_i).astype(BF16),
                           row(lru_b_i), row(lru_lambda), _tile(s, 512), s)
    proj3 = proj.reshape(b, s, -1)
    o_lru = o_lru.reshape(b, s, LRU_WIDTH)
    q_gain = jnp.abs(col_gain[0, :PROJ_TILE])
    z_max = 1.05 * SB_HEAD_DIM * jnp.max(q_gain) * jnp.max(jnp.abs(g_k_sb))
    carry_done = (z_max + BF16_ZERO_EXP).reshape(1, 1).astype(F32)
    o_sb = _sb_attention(proj3, carry_done, _tile(s, 256))
    kx, vx = _mem_kv(mem, row(g_mem), w_mem_kv.astype(BF16), row(jnp.tile(g_k_x, X_HEADS)))
    x1, hn, comb = _merge(proj3, o_sb, o_lru, kx, vx, x, w_branch.astype(BF16), w_out.astype(BF16),
                          row(g_ffn), w_router, b_router, _tile(s, 1024))
    out = _moe(hn.reshape(t, d), comb.reshape(t, LANES), x1.reshape(t, d), w_gu, w_dn, _tile(t, 1024))
    return out.reshape(b, s, d)


def kernel(x, mem, g_mix, w_in, g_q_sb, g_k_sb, conv_w, conv_b, lru_w_a, lru_b_a, lru_w_i, lru_b_i, lru_lambda, g_mem, w_mem_kv, g_q_x, g_k_x, w_branch, w_out, g_ffn, w_group, b_group, w_expert, b_expert, w_gate, w_up, w_down):
    params = (g_mix, w_in, g_q_sb, g_k_sb, conv_w, conv_b, lru_w_a, lru_b_a, lru_w_i, lru_b_i, lru_lambda,
              g_mem, w_mem_kv, g_q_x, g_k_x, w_branch, w_out, g_ffn, w_group, b_group, w_expert, b_expert,
              w_gate, w_up, w_down)
    for layer in range(g_mix.shape[0]):
        x = _layer(x, mem, *[p[layer] for p in params])
    return x
```

```python
import functools

import jax
import jax.numpy as jnp
from jax import lax
from jax.experimental import pallas as pl
from jax.experimental.pallas import tpu as pltpu

F32 = jnp.float32
BF16 = jnp.bfloat16

EPS = 1e-6
SB_HEAD_DIM = 64
SB_WIDTH = 512
LRU_WIDTH = 512
CONV_WIDTH = 4
LRU_C = 8.0
X_HEADS = 4
X_WIDTH = 512
X_HEAD_DIM = 128
N_BRANCH = 3
N_GROUPS = 4
EXPERTS_PER_GROUP = 8
N_EXPERTS = N_GROUPS * EXPERTS_PER_GROUP
EXPERT_FF = 256

LOG2E = 1.4426950408889634
SB_GROUP = 4
SB_GROUP_WIDTH = SB_GROUP * SB_HEAD_DIM
SB_QBLOCKS = 4
BF16_ZERO_EXP = 160.0

MOE_CHUNK = 8
MXU_TILE = 256
MOE_PAD_LOG2 = 4
MOE_PAD = 1 << MOE_PAD_LOG2
MOE_GID_LANE = N_EXPERTS
MOE_POS_LANE = N_EXPERTS + 1
SUBLANES = 8

LANES = 128
BF16_ROWS = 16
VMEM_LIMIT = 56 * 1024 * 1024
MOE_VMEM_LIMIT = 60 * 1024 * 1024

PROJ_TILE = 512
KIND_HEAD64, KIND_HEAD128, KIND_PLAIN, KIND_SIGMOID, KIND_LRU = 0, 1, 2, 3, 4
PROJ_KINDS = (KIND_HEAD64, KIND_HEAD64, KIND_PLAIN, KIND_LRU, KIND_LRU, KIND_HEAD128) + (KIND_SIGMOID,) * 6
W_COL_XL, W_COL_YL = 3, 4
PROJ_TILES_BEFORE_LRU = 2
PROJ_OUT_COL = {j: sum(k != KIND_LRU for k in PROJ_KINDS[:j]) for j, kind in enumerate(PROJ_KINDS) if kind != KIND_LRU}
COL_Q, COL_K, COL_V, COL_QX, COL_GATE = (PROJ_OUT_COL[j] for j in (0, 1, 2, 5, 6))


def _params(sem, vmem_limit=VMEM_LIMIT):
    return pltpu.CompilerParams(dimension_semantics=sem, vmem_limit_bytes=vmem_limit)


def _rms(xf, g):
    return xf * lax.rsqrt(jnp.mean(xf * xf, axis=-1, keepdims=True) + EPS) * g


def _sigmoid(v):
    return 0.5 * jnp.tanh(0.5 * v) + 0.5


def _group_mean_matrix(n, group):
    shift = group.bit_length() - 1
    r = lax.shift_right_logical(lax.broadcasted_iota(jnp.int32, (n, n), 0), shift)
    c = lax.shift_right_logical(lax.broadcasted_iota(jnp.int32, (n, n), 1), shift)
    return jnp.where(r == c, 1.0 / group, 0.0).astype(BF16)


def _proj_kernel(x_ref, g_ref, w_ref, cg_ref, cw_ref, cb_ref, wa_ref, ba_ref, wi_ref, bi_ref, lam_ref,
                 o_ref, olru_ref, tail_ref, h_ref, *, tm, tiles_per_seq):
    seq_tile = pl.program_id(0) % tiles_per_seq

    @pl.when(seq_tile == 0)
    def _():
        tail_ref[...] = jnp.zeros_like(tail_ref)
        h_ref[...] = jnp.zeros_like(h_ref)

    hn = _rms(x_ref[...], g_ref[...]).astype(BF16)
    group_mean = {KIND_HEAD64: _group_mean_matrix(PROJ_TILE, SB_HEAD_DIM),
                  KIND_HEAD128: _group_mean_matrix(PROJ_TILE, X_HEAD_DIM)}

    def column_tile(j):
        return jnp.dot(hn, w_ref[:, pl.ds(j * PROJ_TILE, PROJ_TILE)], preferred_element_type=F32)

    def stored_tile(j):
        kind, acc = PROJ_KINDS[j], column_tile(j)
        if kind in group_mean:
            ms = jnp.dot((acc * acc).astype(BF16), group_mean[kind], preferred_element_type=F32)
            acc = acc * lax.rsqrt(ms + EPS) * cg_ref[:, pl.ds(j * PROJ_TILE, PROJ_TILE)]
        elif kind == KIND_SIGMOID:
            acc = _sigmoid(acc)
        o_ref[:, pl.ds(PROJ_OUT_COL[j] * PROJ_TILE, PROJ_TILE)] = acc.astype(o_ref.dtype)

    stored = [j for j, kind in enumerate(PROJ_KINDS) if kind != KIND_LRU]
    x_lru, y_lru = column_tile(W_COL_XL), column_tile(W_COL_YL)
    for j in stored[:PROJ_TILES_BEFORE_LRU]:
        stored_tile(j)

    x3 = x_lru.reshape(tm // SUBLANES, SUBLANES, LRU_WIDTH)
    sub_row = lax.broadcasted_iota(jnp.int32, x3.shape, 1)
    xc3 = cb_ref[...] + cw_ref[pl.ds(CONV_WIDTH - 1, 1), :] * x3
    for d in range(1, CONV_WIDTH):
        rolled = pltpu.roll(x3, d, 1)
        before = jnp.concatenate([pltpu.roll(tail_ref[...], d, 0)[None], rolled[:-1]], axis=0)
        xc3 = xc3 + cw_ref[pl.ds(CONV_WIDTH - 1 - d, 1), :] * jnp.where(sub_row >= d, rolled, before)
    tail_ref[...] = x3[tm // SUBLANES - 1]
    xc = xc3.reshape(tm, LRU_WIDTH)
    xcb = xc.astype(BF16)
    r = _sigmoid(jnp.dot(xcb, wa_ref[...], preferred_element_type=F32) + ba_ref[...])
    gi = _sigmoid(jnp.dot(xcb, wi_ref[...], preferred_element_type=F32) + bi_ref[...])

    nlam = -lam_ref[...]
    sp_nlam = jnp.maximum(nlam, 0.0) + jnp.log(1.0 + jnp.exp(-jnp.abs(nlam)))
    state = h_ref[pl.ds(0, 1), :]
    rest = stored[PROJ_TILES_BEFORE_LRU:]
    chunk = tm // len(rest)
    for c, j in enumerate(rest):
        rows = slice(c * chunk, (c + 1) * chunk)
        log_a = (-LRU_C * r[rows]) * sp_nlam
        a = jnp.exp(log_a)
        th = jnp.tanh(log_a)
        n = -2.0 * th
        coef = jnp.where(n > 0.0, n * lax.rsqrt(n * (1.0 - th)), 0.0)
        b = coef * (gi[rows] * xc[rows])
        a = a.reshape(chunk // SUBLANES, SUBLANES, LRU_WIDTH)
        b = b.reshape(chunk // SUBLANES, SUBLANES, LRU_WIDTH)
        row_in_vreg = lax.broadcasted_iota(jnp.int32, a.shape, 1)
        d = 1
        while d < SUBLANES:
            keep = row_in_vreg >= d
            b = jnp.where(keep, a * pltpu.roll(b, d, 1), 0.0) + b
            a = jnp.where(keep, a * pltpu.roll(a, d, 1), a)
            d *= 2
        pieces = []
        for i in range(chunk // SUBLANES):
            pieces.append(b[i] + a[i] * state)
            state = pieces[-1][SUBLANES - 1:SUBLANES, :]
        h = jnp.concatenate(pieces, axis=0)
        olru_ref[pl.ds(c * chunk, chunk), :] = (h * jax.nn.gelu(y_lru[rows])).astype(olru_ref.dtype)
        stored_tile(j)
    h_ref[...] = jnp.broadcast_to(state, h_ref.shape)


def _in_proj(x2, g_mix, w_cat, col_gain, conv_w, conv_b, wa_bd, b_a, wi_bd, b_i, lam, tm, seq_len):
    t, d = x2.shape
    ncols = len([k for k in PROJ_KINDS if k != KIND_LRU]) * PROJ_TILE
    resident = lambda shape: pl.BlockSpec(shape, lambda i: (0, 0), pipeline_mode=pl.Buffered(1))
    vec = lambda: resident((1, LRU_WIDTH))
    mat = lambda: resident((LRU_WIDTH, LRU_WIDTH))
    return pl.pallas_call(
        functools.partial(_proj_kernel, tm=tm, tiles_per_seq=seq_len // tm),
        grid=(t // tm,),
        in_specs=[
            pl.BlockSpec((tm, d), lambda i: (i, 0)),
            resident((1, d)),
            resident((d, w_cat.shape[1])),
            resident((1, w_cat.shape[1])),
            resident((CONV_WIDTH, LRU_WIDTH)), vec(), mat(), vec(), mat(), vec(), vec(),
        ],
        out_specs=[pl.BlockSpec((tm, ncols), lambda i: (i, 0)),
                   pl.BlockSpec((tm, LRU_WIDTH), lambda i: (i, 0))],
        out_shape=[jax.ShapeDtypeStruct((t, ncols), BF16), jax.ShapeDtypeStruct((t, LRU_WIDTH), BF16)],
        scratch_shapes=[pltpu.VMEM((SUBLANES, LRU_WIDTH), F32), pltpu.VMEM((SUBLANES, LRU_WIDTH), F32)],
        compiler_params=_params(("arbitrary",)),
        name="in_proj_rglru",
    )(x2, g_mix, w_cat, col_gain, conv_w, conv_b, wa_bd, b_a, wi_bd, b_i, lam)


def _sb_attn_kernel(done_ref, q_ref, k_ref, v_ref, o_ref, vexp_ref, acc_ref, carry_ref, cprev_ref, z_ref, d_ref, w_ref,
                    *, tq, nkb):
    nq = acc_ref.shape[0]
    step = pl.program_id(2)
    lane = lax.broadcasted_iota(jnp.int32, (tq, SB_GROUP_WIDTH), 1)
    head_lanes = [(lane >= h * SB_HEAD_DIM) & (lane < (h + 1) * SB_HEAD_DIM) for h in range(SB_GROUP)]
    keep_head = lambda m, a: jnp.where(m, a.astype(F32), 0.0).astype(BF16)

    @pl.when(step == 0)
    def _():
        def fill(j, c):
            v4 = v_ref[0, pl.ds(pl.multiple_of(j * tq, tq), tq), :]
            for h in range(SB_GROUP):
                vexp_ref[j, pl.ds(h * tq, tq), :] = keep_head(head_lanes[h], v4)
            return c
        lax.fori_loop(0, nkb, fill, 0)

    q_h = [[keep_head(m, q_ref[0, pl.ds(u * tq, tq), :]) for m in head_lanes] for u in range(nq)]
    qb = [step * nq + u for u in range(nq)]
    row = lax.broadcasted_iota(jnp.int32, (tq, tq), 0)
    col = lax.broadcasted_iota(jnp.int32, (tq, tq), 1)
    later_or_self = jnp.where(row >= col, 1.0, 0.0).astype(BF16)
    causal = col < row
    heads = range(SB_GROUP)

    def keys(kb):
        return k_ref[0, pl.ds(pl.multiple_of(kb * tq, tq), tq), :]

    def score(u, h, k4):
        return lax.dot_general(q_h[u][h], k4, (((1,), (1,)), ((), ())), preferred_element_type=F32)

    def softplus2(z):
        return jnp.maximum(z, 0.0) + jnp.log(1.0 + jnp.exp2(-jnp.abs(z))) * LOG2E

    def later_sums(sp):
        return jnp.dot(sp.astype(BF16), later_or_self, preferred_element_type=F32)

    has_left = [q > 0 for q in qb]
    left = [jnp.maximum(q - 1, 0) for q in qb]
    k_pair = [(keys(qb[u]), keys(left[u])) for u in range(nq)]
    per_q = 2 * SB_GROUP
    n_chain = nq * per_q
    z_pair, sp_pair, totals, d_pair, w_pair, acc_pair = {}, {}, {}, {}, {}, {}

    def pair_scores(c):
        u, i = divmod(c, per_q)
        z_pair[c] = score(u, i % SB_GROUP, k_pair[u][i // SB_GROUP])

    def pair_softplus(c):
        sp = softplus2(z_pair[c])
        sp_pair[c] = jnp.where(causal, sp, 0.0) if c % per_q < SB_GROUP else sp
        totals[c] = jnp.sum(sp_pair[c], axis=-1, keepdims=True)

    def pair_sums(c):
        d = z_pair[c] - later_sums(sp_pair[c])
        if c % per_q < SB_GROUP:
            d_pair[c] = jnp.where(causal, d, -jnp.inf)
        else:
            d_pair[c] = d - jnp.where(has_left[c // per_q], totals[c - SB_GROUP], jnp.inf)

    def pair_weights(c):
        w_pair[c] = jnp.exp2(d_pair[c]).astype(BF16)
        if c % SB_GROUP == SB_GROUP - 1:
            u, i = divmod(c, per_q)
            w = jnp.concatenate([w_pair[c - SB_GROUP + 1 + h] for h in heads], axis=1)
            kb = qb[u] if i < SB_GROUP else left[u]
            acc_pair[c] = jnp.dot(w, vexp_ref[kb], preferred_element_type=F32)

    stages = (pair_scores, pair_softplus, pair_sums, pair_weights)
    for t in range(n_chain + len(stages) - 1):
        for s, stage in enumerate(stages):
            if 0 <= t - s < n_chain:
                stage(t - s)
    for u in range(nq):
        base = u * per_q
        acc_ref[u] = acc_pair[base + SB_GROUP - 1] + acc_pair[base + per_q - 1]
        for h in heads:
            both = totals[base + h] + jnp.where(has_left[u], totals[base + SB_GROUP + h], 0.0)
            carry_ref[u, h] = jnp.broadcast_to(both, (tq, LANES))

    def stick_left(u):
        least = functools.reduce(jnp.minimum, [carry_ref[u, h] for h in heads])
        return jnp.min(least) < done_ref[0, 0]

    def sweep(u):
        def scores(h, k4):
            z_ref[h] = score(u, h, k4)

        def sums(h):
            z = z_ref[h]
            sp = softplus2(z)
            d_ref[h] = z - later_sums(sp)
            carry = carry_ref[u, h]
            cprev_ref[h] = carry
            carry_ref[u, h] = carry + jnp.sum(sp, axis=-1, keepdims=True)

        def weights(h):
            carry_t = jnp.concatenate([cprev_ref[h]] * (tq // LANES), axis=1)
            w_ref[:, pl.ds(h * tq, tq)] = jnp.exp2(d_ref[h] - carry_t).astype(BF16)

        def values(kb):
            acc_ref[u] += jnp.dot(w_ref[...], vexp_ref[kb], preferred_element_type=F32)

        first = qb[u] - 2
        w_ref[...] = jnp.zeros_like(w_ref)
        k_first = keys(first)
        for h in heads:
            scores(h, k_first)
        k_next = keys(jnp.maximum(first - 1, 0))
        for h in heads:
            sums(h)
            scores(h, k_next)

        def body(c):
            j, _ = c
            values(jnp.minimum(first - j + 2, nkb - 1))
            for h in heads:
                weights(h)
            k_next = keys(jnp.maximum(first - j - 1, 0))
            for h in heads:
                sums(h)
                scores(h, k_next)
            return j + 1, stick_left(u)

        j_end, _ = lax.while_loop(lambda c: (c[0] <= first) & c[1], body, (jnp.int32(1), stick_left(u)))
        values(jnp.minimum(first - j_end + 2, nkb - 1))
        for h in heads:
            weights(h)
        values(first - j_end + 1)

    for u in range(nq):
        pl.when((qb[u] >= 2) & stick_left(u))(functools.partial(sweep, u))
        o_ref[0, pl.ds(u * tq, tq), :] = acc_ref[u].astype(o_ref.dtype)


def _sb_attention(proj3, carry_done, tq):
    b, s, _ = proj3.shape
    w = SB_GROUP_WIDTH
    groups = SB_WIDTH // w
    qoff, koff, voff = (COL_Q * PROJ_TILE // w, COL_K * PROJ_TILE // w, COL_V * PROJ_TILE // w)
    nq = SB_QBLOCKS
    while (s // tq) % nq:
        nq //= 2
    tqs = nq * tq
    return pl.pallas_call(
        functools.partial(_sb_attn_kernel, tq=tq, nkb=s // tq),
        grid=(b, groups, s // tqs),
        in_specs=[
            pl.BlockSpec(memory_space=pltpu.SMEM),
            pl.BlockSpec((1, tqs, w), lambda bi, p, qi: (bi, qi, qoff + p)),
            pl.BlockSpec((1, s, w), lambda bi, p, qi: (bi, 0, koff + p)),
            pl.BlockSpec((1, s, w), lambda bi, p, qi: (bi, 0, voff + p)),
        ],
        out_specs=pl.BlockSpec((1, tqs, w), lambda bi, p, qi: (bi, qi, p)),
        out_shape=jax.ShapeDtypeStruct((b, s, SB_WIDTH), BF16),
        scratch_shapes=[pltpu.VMEM((s // tq, SB_GROUP * tq, w), BF16),
                        pltpu.VMEM((nq, tq, w), F32), pltpu.VMEM((nq, SB_GROUP, tq, LANES), F32),
                        pltpu.VMEM((SB_GROUP, tq, LANES), F32), pltpu.VMEM((SB_GROUP, tq, tq), F32),
                        pltpu.VMEM((SB_GROUP, tq, tq), F32), pltpu.VMEM((tq, SB_GROUP * tq), BF16)],
        compiler_params=_params(("parallel", "parallel", "arbitrary")),
        name="sb_attention",
    )(carry_done, proj3, proj3, proj3)


def _mem_kv_kernel(m_ref, g_ref, w_ref, gk_ref, k_ref, v_ref):
    mn = _rms(m_ref[0], g_ref[...]).astype(BF16)
    kv = jnp.dot(mn, w_ref[...], preferred_element_type=F32)
    k = kv[:, :X_WIDTH]
    ms = jnp.dot((k * k).astype(BF16), _group_mean_matrix(X_WIDTH, X_HEAD_DIM), preferred_element_type=F32)
    k_ref[0] = (k * lax.rsqrt(ms + EPS) * gk_ref[...]).astype(BF16)
    v_ref[0] = kv[:, X_WIDTH:].astype(BF16)


def _mem_kv(mem, g_mem, w_kv, gk_cols):
    b, m, d = mem.shape
    return pl.pallas_call(
        _mem_kv_kernel,
        grid=(b,),
        in_specs=[
            pl.BlockSpec((1, m, d), lambda bi: (bi, 0, 0)),
            pl.BlockSpec((1, d), lambda bi: (0, 0)),
            pl.BlockSpec((d, 2 * X_WIDTH), lambda bi: (0, 0)),
            pl.BlockSpec((1, X_WIDTH), lambda bi: (0, 0)),
        ],
        out_specs=[pl.BlockSpec((1, m, X_WIDTH), lambda bi: (bi, 0, 0))] * 2,
        out_shape=[jax.ShapeDtypeStruct((b, m, X_WIDTH), BF16)] * 2,
        compiler_params=_params(("parallel",)),
        name="mem_kv",
    )(mem, g_mem, w_kv, gk_cols)


def _split_bf16(v):
    hi = v.astype(BF16)
    return hi, (v - hi.astype(F32)).astype(BF16)


def _router(logits):
    lane = lax.broadcasted_iota(jnp.int32, logits.shape, 1).astype(F32)
    ninf = -jnp.inf
    far = float(LANES)
    is_group = (lane >= N_EXPERTS) & (lane < N_EXPERTS + N_GROUPS)
    gl = jnp.where(is_group, logits, ninf)
    gmax = jnp.max(gl, axis=-1, keepdims=True)
    gidx = jnp.min(jnp.where(gl == gmax, lane, far), axis=-1, keepdims=True) - N_EXPERTS
    g_prob = 1.0 / jnp.sum(jnp.exp(gl - gmax), axis=-1, keepdims=True)
    first = gidx * EXPERTS_PER_GROUP
    el = jnp.where((lane >= first) & (lane < first + EXPERTS_PER_GROUP), logits, ninf)
    m1 = jnp.max(el, axis=-1, keepdims=True)
    i1 = jnp.min(jnp.where(el == m1, lane, far), axis=-1, keepdims=True)
    el2 = jnp.where(lane == i1, ninf, el)
    m2 = jnp.max(el2, axis=-1, keepdims=True)
    i2 = jnp.min(jnp.where(el2 == m2, lane, far), axis=-1, keepdims=True)
    e2 = jnp.exp(m2 - m1)
    w1 = 1.0 / (1.0 + e2)
    w2 = e2 / (1.0 + e2)
    combine = g_prob * (jnp.where(lane == i1, w1, 0.0) + jnp.where(lane == i2, w2, 0.0))
    return combine + jnp.where(lane == MOE_GID_LANE, gidx, 0.0)


def _merge_kernel(qx_ref, g0_ref, g1_ref, g2_ref, osb_ref, olru_ref, kx_ref, vx_ref, x_ref,
                  wb_ref, wo_ref, gf_ref, wr_ref, br_ref, x1_ref, hn_ref, comb_ref):
    tm = x_ref.shape[1]
    n_part = 2 if tm % (2 * BF16_ROWS) == 0 else 1
    part = tm // n_part
    head_cols = [slice(h * X_HEAD_DIM, (h + 1) * X_HEAD_DIM) for h in range(X_HEADS)]
    w_hi, w_lo = _split_bf16(wr_ref[...])
    w_hi_lo = jnp.concatenate([w_hi, w_lo], axis=1)
    val = [{} for _ in range(n_part)]

    def scores(a, rows):
        v = val[a]
        qx = qx_ref[0, rows, :]
        v["s"] = [lax.dot_general(qx[:, c], kx_ref[0][:, c], (((1,), (1,)), ((), ())),
                                  preferred_element_type=F32) for c in head_cols]
        v["u01"] = (g0_ref[0, rows, :].astype(F32) * jnp.dot(osb_ref[0, rows, :], wb_ref[0],
                                                               preferred_element_type=F32)
                    + g1_ref[0, rows, :].astype(F32) * jnp.dot(olru_ref[0, rows, :], wb_ref[1],
                                                                 preferred_element_type=F32))

    def attend(a, rows):
        v = val[a]
        outs = []
        for s, c in zip(v["s"], head_cols):
            p = jnp.exp(s - jnp.max(s, axis=-1, keepdims=True))
            p = p / jnp.sum(p, axis=-1, keepdims=True)
            outs.append(jnp.dot(p.astype(BF16), vx_ref[0][:, c], preferred_element_type=F32))
        v["o_x"] = jnp.concatenate(outs, axis=1).astype(BF16)

    def project(a, rows):
        v = val[a]
        merged = v["u01"] + g2_ref[0, rows, :].astype(F32) * jnp.dot(v["o_x"], wb_ref[2],
                                                                       preferred_element_type=F32)
        x1 = x_ref[0, rows, :] + jnp.dot(merged.astype(BF16), wo_ref[...], preferred_element_type=F32)
        x1_ref[0, rows, :] = x1
        v["hn"] = _rms(x1, gf_ref[...])
        hn_ref[0, rows, :] = v["hn"].astype(BF16)

    def route(a, rows):
        h_hi, h_lo = _split_bf16(val[a]["hn"])
        both = jnp.dot(h_hi, w_hi_lo, preferred_element_type=F32)
        logits = both[:, :LANES] + both[:, LANES:] + jnp.dot(h_lo, w_hi, preferred_element_type=F32) + br_ref[...]
        comb_ref[0, rows, :] = _router(logits)

    stages = (scores, attend, project, route)
    for t in range(n_part + len(stages) - 1):
        for s, stage in enumerate(stages):
            a = t - s
            if 0 <= a < n_part:
                stage(a, pl.ds(a * part, part))


def _merge(proj3, o_sb, o_lru, kx, vx, x, w_branch, w_out, g_ffn, w_router, b_router, tm):
    b, s, d = x.shape
    m = kx.shape[1]
    gate_blk = lambda n: pl.BlockSpec((1, tm, d), lambda bi, si, n=n: (bi, si, COL_GATE * PROJ_TILE // d + n))
    tok512 = lambda: pl.BlockSpec((1, tm, SB_WIDTH), lambda bi, si: (bi, si, 0))
    const2 = lambda shape: pl.BlockSpec(shape, lambda bi, si: (0, 0), pipeline_mode=pl.Buffered(1))
    return pl.pallas_call(
        _merge_kernel,
        grid=(b, s // tm),
        in_specs=[
            pl.BlockSpec((1, tm, X_WIDTH), lambda bi, si: (bi, si, COL_QX)),
            gate_blk(0), gate_blk(1), gate_blk(2),
            tok512(), tok512(),
            pl.BlockSpec((1, m, X_WIDTH), lambda bi, si: (bi, 0, 0)),
            pl.BlockSpec((1, m, X_WIDTH), lambda bi, si: (bi, 0, 0)),
            pl.BlockSpec((1, tm, d), lambda bi, si: (bi, si, 0)),
            pl.BlockSpec((N_BRANCH, SB_WIDTH, d), lambda bi, si: (0, 0, 0), pipeline_mode=pl.Buffered(1)),
            const2((d, d)), const2((1, d)), const2((d, LANES)), const2((1, LANES)),
        ],
        out_specs=[
            pl.BlockSpec((1, tm, d), lambda bi, si: (bi, si, 0)),
            pl.BlockSpec((1, tm, d), lambda bi, si: (bi, si, 0)),
            pl.BlockSpec((1, tm, LANES), lambda bi, si: (bi, si, 0)),
        ],
        out_shape=[
            jax.ShapeDtypeStruct((b, s, d), F32),
            jax.ShapeDtypeStruct((b, s, d), BF16),
            jax.ShapeDtypeStruct((b, s, LANES), F32),
        ],
        compiler_params=_params(("parallel", "parallel")),
        name="merge_router",
    )(proj3, proj3, proj3, proj3, o_sb, o_lru, kx, vx, x, w_branch, w_out, g_ffn, w_router, b_router)


def _moe_sort(hn_ref, comb_ref, xs_ref, cs_ref, ys_ref, pt_ref, seg_ref, *, tt, rows, sub):
    comb = comb_ref[...]
    comb_t = comb.T
    gid_row = comb_t[MOE_GID_LANE:MOE_GID_LANE + 1, :]
    group_of_row = lax.broadcasted_iota(jnp.int32, (SUBLANES, tt), 0).astype(F32)
    member = jnp.where(group_of_row == gid_row, 1.0, 0.0)
    r = lax.broadcasted_iota(jnp.int32, (MXU_TILE, MXU_TILE), 0)
    c = lax.broadcasted_iota(jnp.int32, (MXU_TILE, MXU_TILE), 1)
    upto = jnp.where(r <= c, 1.0, 0.0).astype(BF16)
    before = jnp.zeros((SUBLANES, 1), F32)
    ranks = []
    for blk in range(tt // MXU_TILE):
        m = member[:, blk * MXU_TILE:(blk + 1) * MXU_TILE]
        incl = jnp.dot(m.astype(BF16), upto, preferred_element_type=F32) + before
        ranks.append(jnp.sum(m * (incl - m), axis=0, keepdims=True))
        before = before + jnp.sum(m, axis=1, keepdims=True)
    pos_row = jnp.concatenate(ranks, axis=1)
    start = jnp.int32(0)
    for g in range(N_GROUPS):
        count = jnp.sum(member[g:g + 1, :]).astype(jnp.int32)
        padded = lax.shift_left(lax.shift_right_logical(count + (MOE_PAD - 1), MOE_PAD_LOG2), MOE_PAD_LOG2)
        seg_ref[g] = start
        seg_ref[N_GROUPS + g] = padded
        pos_row = pos_row + member[g:g + 1, :] * start.astype(F32)
        start = start + padded

    hi_lo = jnp.concatenate(_split_bf16(comb), axis=1)
    hn = hn_ref[...]
    for blk in range(rows // LANES):
        rr = (lax.broadcasted_iota(jnp.int32, (LANES, tt), 0) + blk * LANES).astype(F32)
        p = jnp.where(rr == pos_row, 1.0, 0.0).astype(BF16)
        sl = pl.ds(blk * LANES, LANES)
        xs_ref[sl, :] = jnp.dot(p, hn, preferred_element_type=F32).astype(BF16)
        hl = jnp.dot(p, hi_lo, preferred_element_type=F32)
        cs_ref[sl, :] = hl[:, :LANES] + hl[:, LANES:]
    tail = pl.ds(rows, sub)
    xs_ref[tail, :] = jnp.zeros((sub, xs_ref.shape[1]), BF16)
    cs_ref[tail, :] = jnp.zeros((sub, LANES), F32)
    ys_ref[...] = jnp.zeros_like(ys_ref)

    sub_t = lax.broadcasted_iota(jnp.int32, (LANES, tt), 0)
    pos_col = jnp.where(sub_t == MOE_POS_LANE, pos_row, comb_t).T[:, MOE_POS_LANE:MOE_POS_LANE + 1]
    for blk in range(tt // LANES):
        cc = lax.broadcasted_iota(jnp.int32, (LANES, rows), 1).astype(F32)
        sl = pl.ds(blk * LANES, LANES)
        pt_ref[sl, :] = jnp.where(cc == pos_col[blk * LANES:(blk + 1) * LANES, :], 1.0, 0.0).astype(BF16)


def _moe_kernel(hn_ref, comb_ref, x1_ref, wgu_ref, wd_ref, o_ref, xs_ref, cs_ref, ys_ref, pt_ref, seg_ref,
                *, tt, rows, sub):
    c = pl.program_id(1)

    @pl.when(c == 0)
    def _():
        _moe_sort(hn_ref, comb_ref, xs_ref, cs_ref, ys_ref, pt_ref, seg_ref, tt=tt, rows=rows, sub=sub)

    group = lax.shift_right_logical(c, (EXPERTS_PER_GROUP // MOE_CHUNK).bit_length() - 1)
    start = seg_ref[group]
    padded = seg_ref[N_GROUPS + group]
    n_sub = sum((padded > k * sub).astype(jnp.int32) for k in range(-(-rows // sub)))
    lane = lax.broadcasted_iota(jnp.int32, (sub, LANES), 1)

    def sub_tile(s, carry):
        sl = pl.ds(pl.multiple_of(start + s * sub, MOE_PAD), sub)
        x = xs_ref[sl, :]
        cw = cs_ref[sl, :]
        acc = jnp.zeros((sub, ys_ref.shape[1]), F32)
        gate_up = {}
        for step in range(MOE_CHUNK + 1):
            if step < MOE_CHUNK:
                gate_up[step] = jnp.dot(x, wgu_ref[step], preferred_element_type=F32)
            if step > 0:
                k = step - 1
                gate, up = gate_up[k][:, :EXPERT_FF], gate_up[k][:, EXPERT_FF:]
                weight = jnp.sum(jnp.where(lane == c * MOE_CHUNK + k, cw, 0.0), axis=-1, keepdims=True)
                act = jax.nn.silu(gate) * up * weight
                acc = acc + jnp.dot(act.astype(BF16), wd_ref[k], preferred_element_type=F32)
        ys_ref[sl, :] = acc.astype(BF16)
        return carry

    lax.fori_loop(0, n_sub, sub_tile, 0)

    @pl.when(c == pl.num_programs(1) - 1)
    def _():
        o_ref[...] = x1_ref[...] + jnp.dot(pt_ref[...], ys_ref[pl.ds(0, rows), :], preferred_element_type=F32)


def _moe(hn2, comb2, x1_2, w_gu, w_down, tt):
    t, d = hn2.shape
    rows = -(-(tt + N_GROUPS * MOE_PAD) // LANES) * LANES
    spread = 3.5 * (tt * (N_GROUPS - 1)) ** 0.5 / N_GROUPS
    sub = min(rows, -(-int(tt / N_GROUPS + spread) // MOE_PAD) * MOE_PAD)
    return pl.pallas_call(
        functools.partial(_moe_kernel, tt=tt, rows=rows, sub=sub),
        grid=(t // tt, N_EXPERTS // MOE_CHUNK),
        in_specs=[
            pl.BlockSpec((tt, d), lambda i, c: (i, 0)),
            pl.BlockSpec((tt, LANES), lambda i, c: (i, 0)),
            pl.BlockSpec((tt, d), lambda i, c: (i, 0)),
            pl.BlockSpec((MOE_CHUNK, d, 2 * EXPERT_FF), lambda i, c: (c, 0, 0)),
            pl.BlockSpec((MOE_CHUNK, EXPERT_FF, d), lambda i, c: (c, 0, 0)),
        ],
        out_specs=pl.BlockSpec((tt, d), lambda i, c: (i, 0)),
        out_shape=jax.ShapeDtypeStruct((t, d), F32),
        scratch_shapes=[pltpu.VMEM((rows + sub, d), BF16), pltpu.VMEM((rows + sub, LANES), F32),
                        pltpu.VMEM((rows + sub, d), BF16), pltpu.VMEM((tt, rows), BF16),
                        pltpu.SMEM((2 * N_GROUPS,), jnp.int32)],
        compiler_params=_params(("parallel", "arbitrary"), MOE_VMEM_LIMIT),
        name="moe",
    )(hn2, comb2, x1_2, w_gu, w_down)


def _block_diag(w):
    n, bd, _ = w.shape
    eye = jnp.eye(n, dtype=w.dtype)
    return jnp.einsum("nij,nm->nimj", w, eye).reshape(n * bd, n * bd)


def _tile(n, pref):
    while n % pref:
        pref //= 2
    return pref


def _layer(x, mem, g_mix, w_in, g_q_sb, g_k_sb, conv_w, conv_b, lru_w_a, lru_b_a, lru_w_i, lru_b_i,
           lru_lambda, g_mem, w_mem_kv, g_q_x, g_k_x, w_branch, w_out, g_ffn, w_group, b_group,
           w_expert, b_expert, w_gate, w_up, w_down):
    b, s, d = x.shape
    t = b * s
    row = lambda v: v.reshape(1, -1).astype(F32)

    ones = jnp.ones((PROJ_TILE,), F32)
    col_gain = jnp.concatenate([
        jnp.tile(g_q_sb, SB_WIDTH // SB_HEAD_DIM) * (SB_HEAD_DIM ** -0.5 * LOG2E),
        jnp.tile(g_k_sb, SB_WIDTH // SB_HEAD_DIM),
        ones, ones, ones,
        jnp.tile(g_q_x, X_HEADS) * X_HEAD_DIM ** -0.5,
    ] + [ones] * 6).reshape(1, -1)
    w_router = jnp.zeros((d, LANES), F32).at[:, :N_EXPERTS].set(w_expert)
    w_router = w_router.at[:, N_EXPERTS:N_EXPERTS + N_GROUPS].set(w_group)
    b_router = jnp.zeros((1, LANES), F32).at[0, :N_EXPERTS].set(b_expert)
    b_router = b_router.at[0, N_EXPERTS:N_EXPERTS + N_GROUPS].set(b_group)
    w_gu = jnp.concatenate([w_gate, w_up], axis=-1).reshape(N_EXPERTS, d, 2 * EXPERT_FF).astype(BF16)
    w_dn = w_down.reshape(N_EXPERTS, EXPERT_FF, d).astype(BF16)

    proj, o_lru = _in_proj(x.reshape(t, d), row(g_mix), w_in.astype(BF16), col_gain, conv_w, row(conv_b),
                           _block_diag(lru_w_a).astype(BF16), row(lru_b_a), _block_diag(lru_w_i).astype(BF16),
                           row(lru_b_i), row(lru_lambda), _tile(s, 512), s)
    proj3 = proj.reshape(b, s, -1)
    o_lru = o_lru.reshape(b, s, LRU_WIDTH)
    q_gain = jnp.abs(col_gain[0, :PROJ_TILE])
    z_max = 1.05 * SB_HEAD_DIM * jnp.max(q_gain) * jnp.max(jnp.abs(g_k_sb))
    carry_done = (z_max + BF16_ZERO_EXP).reshape(1, 1).astype(F32)
    o_sb = _sb_attention(proj3, carry_done, _tile(s, 256))
    kx, vx = _mem_kv(mem, row(g_mem), w_mem_kv.astype(BF16), row(jnp.tile(g_k_x, X_HEADS)))
    x1, hn, comb = _merge(proj3, o_sb, o_lru, kx, vx, x, w_branch.astype(BF16), w_out.astype(BF16),
                          row(g_ffn), w_router, b_router, _tile(s, 1024))
    out = _moe(hn.reshape(t, d), comb.reshape(t, LANES), x1.reshape(t, d), w_gu, w_dn, _tile(t, 1024))
    return out.reshape(b, s, d)


def kernel(x, mem, g_mix, w_in, g_q_sb, g_k_sb, conv_w, conv_b, lru_w_a, lru_b_a, lru_w_i, lru_b_i, lru_lambda, g_mem, w_mem_kv, g_q_x, g_k_x, w_branch, w_out, g_ffn, w_group, b_group, w_expert, b_expert, w_gate, w_up, w_down):
    params = (g_mix, w_in, g_q_sb, g_k_sb, conv_w, conv_b, lru_w_a, lru_b_a, lru_w_i, lru_b_i, lru_lambda,
              g_mem, w_mem_kv, g_q_x, g_k_x, w_branch, w_out, g_ffn, w_group, b_group, w_expert, b_expert,
              w_gate, w_up, w_down)
    for layer in range(g_mix.shape[0]):
        x = _layer(x, mem, *[p[layer] for p in params])
    return x
```

```python
import functools

import jax
import jax.numpy as jnp
from jax import lax
from jax.experimental import pallas as pl
from jax.experimental.pallas import tpu as pltpu

F32 = jnp.float32
BF16 = jnp.bfloat16

EPS = 1e-6
SB_HEAD_DIM = 64
SB_WIDTH = 512
LRU_WIDTH = 512
CONV_WIDTH = 4
LRU_C = 8.0
X_HEADS = 4
X_WIDTH = 512
X_HEAD_DIM = 128
N_BRANCH = 3
N_GROUPS = 4
EXPERTS_PER_GROUP = 8
N_EXPERTS = N_GROUPS * EXPERTS_PER_GROUP
EXPERT_FF = 256

LOG2E = 1.4426950408889634
SB_GROUP = 4
SB_GROUP_WIDTH = SB_GROUP * SB_HEAD_DIM
SB_QBLOCKS = 4
BF16_ZERO_EXP = 160.0

MOE_CHUNK = 8
MXU_TILE = 256
MOE_SUB_SIGMAS = 2.3
MOE_PAD_LOG2 = 4
MOE_PAD = 1 << MOE_PAD_LOG2
MOE_GID_LANE = N_EXPERTS
MOE_POS_LANE = N_EXPERTS + 1
SUBLANES = 8

LANES = 128
BF16_ROWS = 16
VMEM_LIMIT = 56 * 1024 * 1024
MOE_VMEM_LIMIT = 60 * 1024 * 1024

PROJ_TILE = 512
KIND_HEAD64, KIND_HEAD128, KIND_PLAIN, KIND_SIGMOID, KIND_LRU = 0, 1, 2, 3, 4
PROJ_KINDS = (KIND_HEAD64, KIND_HEAD64, KIND_PLAIN, KIND_LRU, KIND_LRU, KIND_HEAD128) + (KIND_SIGMOID,) * 6
W_COL_XL, W_COL_YL = 3, 4
PROJ_TILES_BEFORE_LRU = 2
PROJ_OUT_COL = {j: sum(k != KIND_LRU for k in PROJ_KINDS[:j]) for j, kind in enumerate(PROJ_KINDS) if kind != KIND_LRU}
COL_Q, COL_K, COL_V, COL_QX, COL_GATE = (PROJ_OUT_COL[j] for j in (0, 1, 2, 5, 6))


def _params(sem, vmem_limit=VMEM_LIMIT):
    return pltpu.CompilerParams(dimension_semantics=sem, vmem_limit_bytes=vmem_limit)


def _rms(xf, g):
    return xf * lax.rsqrt(jnp.mean(xf * xf, axis=-1, keepdims=True) + EPS) * g


def _sigmoid(v):
    return 0.5 * jnp.tanh(0.5 * v) + 0.5


def _group_mean_matrix(n, group):
    shift = group.bit_length() - 1
    r = lax.shift_right_logical(lax.broadcasted_iota(jnp.int32, (n, n), 0), shift)
    c = lax.shift_right_logical(lax.broadcasted_iota(jnp.int32, (n, n), 1), shift)
    return jnp.where(r == c, 1.0 / group, 0.0).astype(BF16)


def _proj_kernel(x_ref, g_ref, w_ref, cg_ref, cw_ref, cb_ref, wa_ref, ba_ref, wi_ref, bi_ref, lam_ref,
                 o_ref, olru_ref, tail_ref, h_ref, *, tm, tiles_per_seq):
    seq_tile = pl.program_id(0) % tiles_per_seq

    @pl.when(seq_tile == 0)
    def _():
        tail_ref[...] = jnp.zeros_like(tail_ref)
        h_ref[...] = jnp.zeros_like(h_ref)

    hn = _rms(x_ref[...], g_ref[...]).astype(BF16)
    group_mean = {KIND_HEAD64: _group_mean_matrix(PROJ_TILE, SB_HEAD_DIM),
                  KIND_HEAD128: _group_mean_matrix(PROJ_TILE, X_HEAD_DIM)}

    def column_tile(j):
        return jnp.dot(hn, w_ref[:, pl.ds(j * PROJ_TILE, PROJ_TILE)], preferred_element_type=F32)

    def stored_tile(j):
        kind, acc = PROJ_KINDS[j], column_tile(j)
        if kind in group_mean:
            ms = jnp.dot((acc * acc).astype(BF16), group_mean[kind], preferred_element_type=F32)
            acc = acc * lax.rsqrt(ms + EPS) * cg_ref[:, pl.ds(j * PROJ_TILE, PROJ_TILE)]
        elif kind == KIND_SIGMOID:
            acc = _sigmoid(acc)
        o_ref[:, pl.ds(PROJ_OUT_COL[j] * PROJ_TILE, PROJ_TILE)] = acc.astype(o_ref.dtype)

    stored = [j for j, kind in enumerate(PROJ_KINDS) if kind != KIND_LRU]
    x_lru, y_lru = column_tile(W_COL_XL), column_tile(W_COL_YL)
    for j in stored[:PROJ_TILES_BEFORE_LRU]:
        stored_tile(j)

    x3 = x_lru.reshape(tm // SUBLANES, SUBLANES, LRU_WIDTH)
    sub_row = lax.broadcasted_iota(jnp.int32, x3.shape, 1)
    xc3 = cb_ref[...] + cw_ref[pl.ds(CONV_WIDTH - 1, 1), :] * x3
    for d in range(1, CONV_WIDTH):
        rolled = pltpu.roll(x3, d, 1)
        before = jnp.concatenate([pltpu.roll(tail_ref[...], d, 0)[None], rolled[:-1]], axis=0)
        xc3 = xc3 + cw_ref[pl.ds(CONV_WIDTH - 1 - d, 1), :] * jnp.where(sub_row >= d, rolled, before)
    tail_ref[...] = x3[tm // SUBLANES - 1]
    xc = xc3.reshape(tm, LRU_WIDTH)
    xcb = xc.astype(BF16)
    r = _sigmoid(jnp.dot(xcb, wa_ref[...], preferred_element_type=F32) + ba_ref[...])
    gi = _sigmoid(jnp.dot(xcb, wi_ref[...], preferred_element_type=F32) + bi_ref[...])

    nlam = -lam_ref[...]
    sp_nlam = jnp.maximum(nlam, 0.0) + jnp.log(1.0 + jnp.exp(-jnp.abs(nlam)))
    state = h_ref[pl.ds(0, 1), :]
    rest = stored[PROJ_TILES_BEFORE_LRU:]
    chunk = tm // len(rest)
    for c, j in enumerate(rest):
        rows = slice(c * chunk, (c + 1) * chunk)
        log_a = (-LRU_C * r[rows]) * sp_nlam
        a = jnp.exp(log_a)
        th = jnp.tanh(log_a)
        n = -2.0 * th
        coef = jnp.where(n > 0.0, n * lax.rsqrt(n * (1.0 - th)), 0.0)
        b = coef * (gi[rows] * xc[rows])
        a = a.reshape(chunk // SUBLANES, SUBLANES, LRU_WIDTH)
        b = b.reshape(chunk // SUBLANES, SUBLANES, LRU_WIDTH)
        row_in_vreg = lax.broadcasted_iota(jnp.int32, a.shape, 1)
        d = 1
        while d < SUBLANES:
            keep = row_in_vreg >= d
            b = jnp.where(keep, a * pltpu.roll(b, d, 1), 0.0) + b
            a = jnp.where(keep, a * pltpu.roll(a, d, 1), a)
            d *= 2
        pieces = []
        for i in range(chunk // SUBLANES):
            pieces.append(b[i] + a[i] * state)
            state = pieces[-1][SUBLANES - 1:SUBLANES, :]
        h = jnp.concatenate(pieces, axis=0)
        olru_ref[pl.ds(c * chunk, chunk), :] = (h * jax.nn.gelu(y_lru[rows])).astype(olru_ref.dtype)
        stored_tile(j)
    h_ref[...] = jnp.broadcast_to(state, h_ref.shape)


def _in_proj(x2, g_mix, w_cat, col_gain, conv_w, conv_b, wa_bd, b_a, wi_bd, b_i, lam, tm, seq_len):
    t, d = x2.shape
    ncols = len([k for k in PROJ_KINDS if k != KIND_LRU]) * PROJ_TILE
    resident = lambda shape: pl.BlockSpec(shape, lambda i: (0, 0), pipeline_mode=pl.Buffered(1))
    vec = lambda: resident((1, LRU_WIDTH))
    mat = lambda: resident((LRU_WIDTH, LRU_WIDTH))
    return pl.pallas_call(
        functools.partial(_proj_kernel, tm=tm, tiles_per_seq=seq_len // tm),
        grid=(t // tm,),
        in_specs=[
            pl.BlockSpec((tm, d), lambda i: (i, 0)),
            resident((1, d)),
            resident((d, w_cat.shape[1])),
            resident((1, w_cat.shape[1])),
            resident((CONV_WIDTH, LRU_WIDTH)), vec(), mat(), vec(), mat(), vec(), vec(),
        ],
        out_specs=[pl.BlockSpec((tm, ncols), lambda i: (i, 0)),
                   pl.BlockSpec((tm, LRU_WIDTH), lambda i: (i, 0))],
        out_shape=[jax.ShapeDtypeStruct((t, ncols), BF16), jax.ShapeDtypeStruct((t, LRU_WIDTH), BF16)],
        scratch_shapes=[pltpu.VMEM((SUBLANES, LRU_WIDTH), F32), pltpu.VMEM((SUBLANES, LRU_WIDTH), F32)],
        compiler_params=_params(("arbitrary",)),
        name="in_proj_rglru",
    )(x2, g_mix, w_cat, col_gain, conv_w, conv_b, wa_bd, b_a, wi_bd, b_i, lam)


def _sb_attn_kernel(done_ref, q_ref, k_ref, v_ref, o_ref, vexp_ref, acc_ref, carry_ref, cprev_ref, z_ref, d_ref, w_ref,
                    *, tq, nkb):
    nq = acc_ref.shape[0]
    step = pl.program_id(2)
    lane = lax.broadcasted_iota(jnp.int32, (tq, SB_GROUP_WIDTH), 1)
    head_lanes = [(lane >= h * SB_HEAD_DIM) & (lane < (h + 1) * SB_HEAD_DIM) for h in range(SB_GROUP)]
    keep_head = lambda m, a: jnp.where(m, a.astype(F32), 0.0).astype(BF16)

    @pl.when(step == 0)
    def _():
        def fill(j, c):
            v4 = v_ref[0, pl.ds(pl.multiple_of(j * tq, tq), tq), :]
            for h in range(SB_GROUP):
                vexp_ref[j, pl.ds(h * tq, tq), :] = keep_head(head_lanes[h], v4)
            return c
        lax.fori_loop(0, nkb, fill, 0)

    q_h = [[keep_head(m, q_ref[0, pl.ds(u * tq, tq), :]) for m in head_lanes] for u in range(nq)]
    qb = [step * nq + u for u in range(nq)]
    row = lax.broadcasted_iota(jnp.int32, (tq, tq), 0)
    col = lax.broadcasted_iota(jnp.int32, (tq, tq), 1)
    later_or_self = jnp.where(row >= col, 1.0, 0.0).astype(BF16)
    causal = col < row
    heads = range(SB_GROUP)

    def keys(kb):
        return k_ref[0, pl.ds(pl.multiple_of(kb * tq, tq), tq), :]

    def score(u, h, k4):
        return lax.dot_general(q_h[u][h], k4, (((1,), (1,)), ((), ())), preferred_element_type=F32)

    def softplus2(z):
        return jnp.maximum(z, 0.0) + jnp.log(1.0 + jnp.exp2(-jnp.abs(z))) * LOG2E

    def later_sums(sp):
        return jnp.dot(sp.astype(BF16), later_or_self, preferred_element_type=F32)

    has_left = [q > 0 for q in qb]
    left = [jnp.maximum(q - 1, 0) for q in qb]
    k_pair = [(keys(qb[u]), keys(left[u])) for u in range(nq)]
    per_q = 2 * SB_GROUP
    n_chain = nq * per_q
    z_pair, sp_pair, totals, d_pair, w_pair, acc_pair = {}, {}, {}, {}, {}, {}

    def pair_scores(c):
        u, i = divmod(c, per_q)
        z_pair[c] = score(u, i % SB_GROUP, k_pair[u][i // SB_GROUP])

    def pair_softplus(c):
        sp = softplus2(z_pair[c])
        sp_pair[c] = jnp.where(causal, sp, 0.0) if c % per_q < SB_GROUP else sp
        totals[c] = jnp.sum(sp_pair[c], axis=-1, keepdims=True)

    def pair_sums(c):
        d = z_pair[c] - later_sums(sp_pair[c])
        if c % per_q < SB_GROUP:
            d_pair[c] = jnp.where(causal, d, -jnp.inf)
        else:
            d_pair[c] = d - jnp.where(has_left[c // per_q], totals[c - SB_GROUP], jnp.inf)

    def pair_weights(c):
        w_pair[c] = jnp.exp2(d_pair[c]).astype(BF16)
        if c % SB_GROUP == SB_GROUP - 1:
            u, i = divmod(c, per_q)
            w = jnp.concatenate([w_pair[c - SB_GROUP + 1 + h] for h in heads], axis=1)
            kb = qb[u] if i < SB_GROUP else left[u]
            acc_pair[c] = jnp.dot(w, vexp_ref[kb], preferred_element_type=F32)

    stages = (pair_scores, pair_softplus, pair_sums, pair_weights)
    for t in range(n_chain + len(stages) - 1):
        for s, stage in enumerate(stages):
            if 0 <= t - s < n_chain:
                stage(t - s)
    for u in range(nq):
        base = u * per_q
        acc_ref[u] = acc_pair[base + SB_GROUP - 1] + acc_pair[base + per_q - 1]
        for h in heads:
            both = totals[base + h] + jnp.where(has_left[u], totals[base + SB_GROUP + h], 0.0)
            carry_ref[u, h] = jnp.broadcast_to(both, (tq, LANES))

    def stick_left(u):
        least = functools.reduce(jnp.minimum, [carry_ref[u, h] for h in heads])
        return jnp.min(least) < done_ref[0, 0]

    def sweep(u):
        def scores(h, k4):
            z_ref[h] = score(u, h, k4)

        def sums(h):
            z = z_ref[h]
            sp = softplus2(z)
            d_ref[h] = z - later_sums(sp)
            carry = carry_ref[u, h]
            cprev_ref[h] = carry
            carry_ref[u, h] = carry + jnp.sum(sp, axis=-1, keepdims=True)

        def weights(h):
            carry_t = jnp.concatenate([cprev_ref[h]] * (tq // LANES), axis=1)
            w_ref[:, pl.ds(h * tq, tq)] = jnp.exp2(d_ref[h] - carry_t).astype(BF16)

        def values(kb):
            acc_ref[u] += jnp.dot(w_ref[...], vexp_ref[kb], preferred_element_type=F32)

        first = qb[u] - 2
        w_ref[...] = jnp.zeros_like(w_ref)
        k_first = keys(first)
        for h in heads:
            scores(h, k_first)
        k_next = keys(jnp.maximum(first - 1, 0))
        for h in heads:
            sums(h)
            scores(h, k_next)

        def body(c):
            j, _ = c
            values(jnp.minimum(first - j + 2, nkb - 1))
            for h in heads:
                weights(h)
            k_next = keys(jnp.maximum(first - j - 1, 0))
            for h in heads:
                sums(h)
                scores(h, k_next)
            return j + 1, stick_left(u)

        j_end, _ = lax.while_loop(lambda c: (c[0] <= first) & c[1], body, (jnp.int32(1), stick_left(u)))
        values(jnp.minimum(first - j_end + 2, nkb - 1))
        for h in heads:
            weights(h)
        values(first - j_end + 1)

    for u in range(nq):
        pl.when((qb[u] >= 2) & stick_left(u))(functools.partial(sweep, u))
        o_ref[0, pl.ds(u * tq, tq), :] = acc_ref[u].astype(o_ref.dtype)


def _sb_attention(proj3, carry_done, tq):
    b, s, _ = proj3.shape
    w = SB_GROUP_WIDTH
    groups = SB_WIDTH // w
    qoff, koff, voff = (COL_Q * PROJ_TILE // w, COL_K * PROJ_TILE // w, COL_V * PROJ_TILE // w)
    nq = SB_QBLOCKS
    while (s // tq) % nq:
        nq //= 2
    tqs = nq * tq
    return pl.pallas_call(
        functools.partial(_sb_attn_kernel, tq=tq, nkb=s // tq),
        grid=(b, groups, s // tqs),
        in_specs=[
            pl.BlockSpec(memory_space=pltpu.SMEM),
            pl.BlockSpec((1, tqs, w), lambda bi, p, qi: (bi, qi, qoff + p)),
            pl.BlockSpec((1, s, w), lambda bi, p, qi: (bi, 0, koff + p)),
            pl.BlockSpec((1, s, w), lambda bi, p, qi: (bi, 0, voff + p)),
        ],
        out_specs=pl.BlockSpec((1, tqs, w), lambda bi, p, qi: (bi, qi, p)),
        out_shape=jax.ShapeDtypeStruct((b, s, SB_WIDTH), BF16),
        scratch_shapes=[pltpu.VMEM((s // tq, SB_GROUP * tq, w), BF16),
                        pltpu.VMEM((nq, tq, w), F32), pltpu.VMEM((nq, SB_GROUP, tq, LANES), F32),
                        pltpu.VMEM((SB_GROUP, tq, LANES), F32), pltpu.VMEM((SB_GROUP, tq, tq), F32),
                        pltpu.VMEM((SB_GROUP, tq, tq), F32), pltpu.VMEM((tq, SB_GROUP * tq), BF16)],
        compiler_params=_params(("parallel", "parallel", "arbitrary")),
        name="sb_attention",
    )(carry_done, proj3, proj3, proj3)


def _mem_kv_kernel(m_ref, g_ref, w_ref, gk_ref, k_ref, v_ref):
    mn = _rms(m_ref[0], g_ref[...]).astype(BF16)
    kv = jnp.dot(mn, w_ref[...], preferred_element_type=F32)
    k = kv[:, :X_WIDTH]
    ms = jnp.dot((k * k).astype(BF16), _group_mean_matrix(X_WIDTH, X_HEAD_DIM), preferred_element_type=F32)
    k_ref[0] = (k * lax.rsqrt(ms + EPS) * gk_ref[...]).astype(BF16)
    v_ref[0] = kv[:, X_WIDTH:].astype(BF16)


def _mem_kv(mem, g_mem, w_kv, gk_cols):
    b, m, d = mem.shape
    return pl.pallas_call(
        _mem_kv_kernel,
        grid=(b,),
        in_specs=[
            pl.BlockSpec((1, m, d), lambda bi: (bi, 0, 0)),
            pl.BlockSpec((1, d), lambda bi: (0, 0)),
            pl.BlockSpec((d, 2 * X_WIDTH), lambda bi: (0, 0)),
            pl.BlockSpec((1, X_WIDTH), lambda bi: (0, 0)),
        ],
        out_specs=[pl.BlockSpec((1, m, X_WIDTH), lambda bi: (bi, 0, 0))] * 2,
        out_shape=[jax.ShapeDtypeStruct((b, m, X_WIDTH), BF16)] * 2,
        compiler_params=_params(("parallel",)),
        name="mem_kv",
    )(mem, g_mem, w_kv, gk_cols)


def _split_bf16(v):
    hi = v.astype(BF16)
    return hi, (v - hi.astype(F32)).astype(BF16)


def _router(logits):
    lane = lax.broadcasted_iota(jnp.int32, logits.shape, 1).astype(F32)
    ninf = -jnp.inf
    far = float(LANES)
    is_group = (lane >= N_EXPERTS) & (lane < N_EXPERTS + N_GROUPS)
    gl = jnp.where(is_group, logits, ninf)
    gmax = jnp.max(gl, axis=-1, keepdims=True)
    gidx = jnp.min(jnp.where(gl == gmax, lane, far), axis=-1, keepdims=True) - N_EXPERTS
    g_prob = 1.0 / jnp.sum(jnp.exp(gl - gmax), axis=-1, keepdims=True)
    first = gidx * EXPERTS_PER_GROUP
    el = jnp.where((lane >= first) & (lane < first + EXPERTS_PER_GROUP), logits, ninf)
    m1 = jnp.max(el, axis=-1, keepdims=True)
    i1 = jnp.min(jnp.where(el == m1, lane, far), axis=-1, keepdims=True)
    el2 = jnp.where(lane == i1, ninf, el)
    m2 = jnp.max(el2, axis=-1, keepdims=True)
    i2 = jnp.min(jnp.where(el2 == m2, lane, far), axis=-1, keepdims=True)
    e2 = jnp.exp(m2 - m1)
    w1 = 1.0 / (1.0 + e2)
    w2 = e2 / (1.0 + e2)
    combine = g_prob * (jnp.where(lane == i1, w1, 0.0) + jnp.where(lane == i2, w2, 0.0))
    return combine + jnp.where(lane == MOE_GID_LANE, gidx, 0.0)


def _merge_kernel(qx_ref, g0_ref, g1_ref, g2_ref, osb_ref, olru_ref, kx_ref, vx_ref, x_ref,
                  wb_ref, wo_ref, gf_ref, wr_ref, br_ref, x1_ref, hn_ref, comb_ref):
    tm = x_ref.shape[1]
    n_part = 2 if tm % (2 * BF16_ROWS) == 0 else 1
    part = tm // n_part
    head_cols = [slice(h * X_HEAD_DIM, (h + 1) * X_HEAD_DIM) for h in range(X_HEADS)]
    w_hi, w_lo = _split_bf16(wr_ref[...])
    w_hi_lo = jnp.concatenate([w_hi, w_lo], axis=1)
    val = [{} for _ in range(n_part)]

    def scores(a, rows):
        v = val[a]
        qx = qx_ref[0, rows, :]
        v["s"] = [lax.dot_general(qx[:, c], kx_ref[0][:, c], (((1,), (1,)), ((), ())),
                                  preferred_element_type=F32) for c in head_cols]
        v["u01"] = (g0_ref[0, rows, :].astype(F32) * jnp.dot(osb_ref[0, rows, :], wb_ref[0],
                                                               preferred_element_type=F32)
                    + g1_ref[0, rows, :].astype(F32) * jnp.dot(olru_ref[0, rows, :], wb_ref[1],
                                                                 preferred_element_type=F32))

    def attend(a, rows):
        v = val[a]
        outs = []
        for s, c in zip(v["s"], head_cols):
            p = jnp.exp(s - jnp.max(s, axis=-1, keepdims=True))
            p = p / jnp.sum(p, axis=-1, keepdims=True)
            outs.append(jnp.dot(p.astype(BF16), vx_ref[0][:, c], preferred_element_type=F32))
        v["o_x"] = jnp.concatenate(outs, axis=1).astype(BF16)

    def project(a, rows):
        v = val[a]
        merged = v["u01"] + g2_ref[0, rows, :].astype(F32) * jnp.dot(v["o_x"], wb_ref[2],
                                                                       preferred_element_type=F32)
        x1 = x_ref[0, rows, :] + jnp.dot(merged.astype(BF16), wo_ref[...], preferred_element_type=F32)
        x1_ref[0, rows, :] = x1
        v["hn"] = _rms(x1, gf_ref[...])
        hn_ref[0, rows, :] = v["hn"].astype(BF16)

    def route(a, rows):
        h_hi, h_lo = _split_bf16(val[a]["hn"])
        both = jnp.dot(h_hi, w_hi_lo, preferred_element_type=F32)
        logits = both[:, :LANES] + both[:, LANES:] + jnp.dot(h_lo, w_hi, preferred_element_type=F32) + br_ref[...]
        comb_ref[0, rows, :] = _router(logits)

    stages = (scores, attend, project, route)
    for t in range(n_part + len(stages) - 1):
        for s, stage in enumerate(stages):
            a = t - s
            if 0 <= a < n_part:
                stage(a, pl.ds(a * part, part))


def _merge(proj3, o_sb, o_lru, kx, vx, x, w_branch, w_out, g_ffn, w_router, b_router, tm):
    b, s, d = x.shape
    m = kx.shape[1]
    gate_blk = lambda n: pl.BlockSpec((1, tm, d), lambda bi, si, n=n: (bi, si, COL_GATE * PROJ_TILE // d + n))
    tok512 = lambda: pl.BlockSpec((1, tm, SB_WIDTH), lambda bi, si: (bi, si, 0))
    const2 = lambda shape: pl.BlockSpec(shape, lambda bi, si: (0, 0), pipeline_mode=pl.Buffered(1))
    return pl.pallas_call(
        _merge_kernel,
        grid=(b, s // tm),
        in_specs=[
            pl.BlockSpec((1, tm, X_WIDTH), lambda bi, si: (bi, si, COL_QX)),
            gate_blk(0), gate_blk(1), gate_blk(2),
            tok512(), tok512(),
            pl.BlockSpec((1, m, X_WIDTH), lambda bi, si: (bi, 0, 0)),
            pl.BlockSpec((1, m, X_WIDTH), lambda bi, si: (bi, 0, 0)),
            pl.BlockSpec((1, tm, d), lambda bi, si: (bi, si, 0)),
            pl.BlockSpec((N_BRANCH, SB_WIDTH, d), lambda bi, si: (0, 0, 0), pipeline_mode=pl.Buffered(1)),
            const2((d, d)), const2((1, d)), const2((d, LANES)), const2((1, LANES)),
        ],
        out_specs=[
            pl.BlockSpec((1, tm, d), lambda bi, si: (bi, si, 0)),
            pl.BlockSpec((1, tm, d), lambda bi, si: (bi, si, 0)),
            pl.BlockSpec((1, tm, LANES), lambda bi, si: (bi, si, 0)),
        ],
        out_shape=[
            jax.ShapeDtypeStruct((b, s, d), F32),
            jax.ShapeDtypeStruct((b, s, d), BF16),
            jax.ShapeDtypeStruct((b, s, LANES), F32),
        ],
        compiler_params=_params(("parallel", "parallel")),
        name="merge_router",
    )(proj3, proj3, proj3, proj3, o_sb, o_lru, kx, vx, x, w_branch, w_out, g_ffn, w_router, b_router)


def _moe_sort(hn_ref, comb_ref, xs_ref, cs_ref, ys_ref, pt_ref, seg_ref, *, tt, rows, sub):
    comb = comb_ref[...]
    comb_t = comb.T
    gid_row = comb_t[MOE_GID_LANE:MOE_GID_LANE + 1, :]
    group_of_row = lax.broadcasted_iota(jnp.int32, (SUBLANES, tt), 0).astype(F32)
    member = jnp.where(group_of_row == gid_row, 1.0, 0.0)
    r = lax.broadcasted_iota(jnp.int32, (MXU_TILE, MXU_TILE), 0)
    c = lax.broadcasted_iota(jnp.int32, (MXU_TILE, MXU_TILE), 1)
    upto = jnp.where(r <= c, 1.0, 0.0).astype(BF16)
    before = jnp.zeros((SUBLANES, 1), F32)
    ranks = []
    for blk in range(tt // MXU_TILE):
        m = member[:, blk * MXU_TILE:(blk + 1) * MXU_TILE]
        incl = jnp.dot(m.astype(BF16), upto, preferred_element_type=F32) + before
        ranks.append(jnp.sum(m * (incl - m), axis=0, keepdims=True))
        before = before + jnp.sum(m, axis=1, keepdims=True)
    pos_row = jnp.concatenate(ranks, axis=1)
    start = jnp.int32(0)
    for g in range(N_GROUPS):
        count = jnp.sum(member[g:g + 1, :]).astype(jnp.int32)
        padded = lax.shift_left(lax.shift_right_logical(count + (MOE_PAD - 1), MOE_PAD_LOG2), MOE_PAD_LOG2)
        seg_ref[g] = start
        seg_ref[N_GROUPS + g] = padded
        pos_row = pos_row + member[g:g + 1, :] * start.astype(F32)
        start = start + padded

    hi_lo = jnp.concatenate(_split_bf16(comb), axis=1)
    hn = hn_ref[...]
    for blk in range(rows // LANES):
        rr = (lax.broadcasted_iota(jnp.int32, (LANES, tt), 0) + blk * LANES).astype(F32)
        p = jnp.where(rr == pos_row, 1.0, 0.0).astype(BF16)
        sl = pl.ds(blk * LANES, LANES)
        xs_ref[sl, :] = jnp.dot(p, hn, preferred_element_type=F32).astype(BF16)
        hl = jnp.dot(p, hi_lo, preferred_element_type=F32)
        cs_ref[sl, :] = hl[:, :LANES] + hl[:, LANES:]
    tail = pl.ds(rows, sub)
    xs_ref[tail, :] = jnp.zeros((sub, xs_ref.shape[1]), BF16)
    cs_ref[tail, :] = jnp.zeros((sub, LANES), F32)
    ys_ref[...] = jnp.zeros_like(ys_ref)

    sub_t = lax.broadcasted_iota(jnp.int32, (LANES, tt), 0)
    pos_col = jnp.where(sub_t == MOE_POS_LANE, pos_row, comb_t).T[:, MOE_POS_LANE:MOE_POS_LANE + 1]
    for blk in range(tt // LANES):
        cc = lax.broadcasted_iota(jnp.int32, (LANES, rows), 1).astype(F32)
        sl = pl.ds(blk * LANES, LANES)
        pt_ref[sl, :] = jnp.where(cc == pos_col[blk * LANES:(blk + 1) * LANES, :], 1.0, 0.0).astype(BF16)


def _moe_kernel(hn_ref, comb_ref, x1_ref, wgu_ref, wd_ref, o_ref, xs_ref, cs_ref, ys_ref, pt_ref, seg_ref,
                *, tt, rows, sub):
    c = pl.program_id(1)

    @pl.when(c == 0)
    def _():
        _moe_sort(hn_ref, comb_ref, xs_ref, cs_ref, ys_ref, pt_ref, seg_ref, tt=tt, rows=rows, sub=sub)

    group = lax.shift_right_logical(c, (EXPERTS_PER_GROUP // MOE_CHUNK).bit_length() - 1)
    start = seg_ref[group]
    padded = seg_ref[N_GROUPS + group]
    n_sub = sum((padded > k * sub).astype(jnp.int32) for k in range(-(-rows // sub)))
    lane = lax.broadcasted_iota(jnp.int32, (sub, LANES), 1)

    def sub_tile(s, carry):
        sl = pl.ds(pl.multiple_of(start + s * sub, MOE_PAD), sub)
        x = xs_ref[sl, :]
        cw = cs_ref[sl, :]
        acc = jnp.zeros((sub, ys_ref.shape[1]), F32)
        gate_up = {}
        for step in range(MOE_CHUNK + 1):
            if step < MOE_CHUNK:
                gate_up[step] = jnp.dot(x, wgu_ref[step], preferred_element_type=F32)
            if step > 0:
                k = step - 1
                gate, up = gate_up[k][:, :EXPERT_FF], gate_up[k][:, EXPERT_FF:]
                weight = jnp.sum(jnp.where(lane == c * MOE_CHUNK + k, cw, 0.0), axis=-1, keepdims=True)
                act = jax.nn.silu(gate) * up * weight
                acc = acc + jnp.dot(act.astype(BF16), wd_ref[k], preferred_element_type=F32)
        ys_ref[sl, :] = acc.astype(BF16)
        return carry

    lax.fori_loop(0, n_sub, sub_tile, 0)

    @pl.when(c == pl.num_programs(1) - 1)
    def _():
        o_ref[...] = x1_ref[...] + jnp.dot(pt_ref[...], ys_ref[pl.ds(0, rows), :], preferred_element_type=F32)


def _moe(hn2, comb2, x1_2, w_gu, w_down, tt):
    t, d = hn2.shape
    rows = -(-(tt + N_GROUPS * MOE_PAD) // LANES) * LANES
    spread = MOE_SUB_SIGMAS * (tt * (N_GROUPS - 1)) ** 0.5 / N_GROUPS
    sub = min(rows, -(-int(tt / N_GROUPS + spread) // MOE_PAD) * MOE_PAD)
    return pl.pallas_call(
        functools.partial(_moe_kernel, tt=tt, rows=rows, sub=sub),
        grid=(t // tt, N_EXPERTS // MOE_CHUNK),
        in_specs=[
            pl.BlockSpec((tt, d), lambda i, c: (i, 0)),
            pl.BlockSpec((tt, LANES), lambda i, c: (i, 0)),
            pl.BlockSpec((tt, d), lambda i, c: (i, 0)),
            pl.BlockSpec((MOE_CHUNK, d, 2 * EXPERT_FF), lambda i, c: (c, 0, 0)),
            pl.BlockSpec((MOE_CHUNK, EXPERT_FF, d), lambda i, c: (c, 0, 0)),
        ],
        out_specs=pl.BlockSpec((tt, d), lambda i, c: (i, 0)),
        out_shape=jax.ShapeDtypeStruct((t, d), F32),
        scratch_shapes=[pltpu.VMEM((rows + sub, d), BF16), pltpu.VMEM((rows + sub, LANES), F32),
                        pltpu.VMEM((rows + sub, d), BF16), pltpu.VMEM((tt, rows), BF16),
                        pltpu.SMEM((2 * N_GROUPS,), jnp.int32)],
        compiler_params=_params(("parallel", "arbitrary"), MOE_VMEM_LIMIT),
        name="moe",
    )(hn2, comb2, x1_2, w_gu, w_down)


def _block_diag(w):
    n, bd, _ = w.shape
    eye = jnp.eye(n, dtype=w.dtype)
    return jnp.einsum("nij,nm->nimj", w, eye).reshape(n * bd, n * bd)


def _tile(n, pref):
    while n % pref:
        pref //= 2
    return pref


def _layer(x, mem, g_mix, w_in, g_q_sb, g_k_sb, conv_w, conv_b, lru_w_a, lru_b_a, lru_w_i, lru_b_i,
           lru_lambda, g_mem, w_mem_kv, g_q_x, g_k_x, w_branch, w_out, g_ffn, w_group, b_group,
           w_expert, b_expert, w_gate, w_up, w_down):
    b, s, d = x.shape
    t = b * s
    row = lambda v: v.reshape(1, -1).astype(F32)

    ones = jnp.ones((PROJ_TILE,), F32)
    col_gain = jnp.concatenate([
        jnp.tile(g_q_sb, SB_WIDTH // SB_HEAD_DIM) * (SB_HEAD_DIM ** -0.5 * LOG2E),
        jnp.tile(g_k_sb, SB_WIDTH // SB_HEAD_DIM),
        ones, ones, ones,
        jnp.tile(g_q_x, X_HEADS) * X_HEAD_DIM ** -0.5,
    ] + [ones] * 6).reshape(1, -1)
    w_router = jnp.zeros((d, LANES), F32).at[:, :N_EXPERTS].set(w_expert)
    w_router = w_router.at[:, N_EXPERTS:N_EXPERTS + N_GROUPS].set(w_group)
    b_router = jnp.zeros((1, LANES), F32).at[0, :N_EXPERTS].set(b_expert)
    b_router = b_router.at[0, N_EXPERTS:N_EXPERTS + N_GROUPS].set(b_group)
    w_gu = jnp.concatenate([w_gate, w_up], axis=-1).reshape(N_EXPERTS, d, 2 * EXPERT_FF).astype(BF16)
    w_dn = w_down.reshape(N_EXPERTS, EXPERT_FF, d).astype(BF16)

    proj, o_lru = _in_proj(x.reshape(t, d), row(g_mix), w_in.astype(BF16), col_gain, conv_w, row(conv_b),
                           _block_diag(lru_w_a).astype(BF16), row(lru_b_a), _block_diag(lru_w_i).astype(BF16),
                           row(lru_b_i), row(lru_lambda), _tile(s, 512), s)
    proj3 = proj.reshape(b, s, -1)
    o_lru = o_lru.reshape(b, s, LRU_WIDTH)
    q_gain = jnp.abs(col_gain[0, :PROJ_TILE])
    z_max = 1.05 * SB_HEAD_DIM * jnp.max(q_gain) * jnp.max(jnp.abs(g_k_sb))
    carry_done = (z_max + BF16_ZERO_EXP).reshape(1, 1).astype(F32)
    o_sb = _sb_attention(proj3, carry_done, _tile(s, 256))
    kx, vx = _mem_kv(mem, row(g_mem), w_mem_kv.astype(BF16), row(jnp.tile(g_k_x, X_HEADS)))
    x1, hn, comb = _merge(proj3, o_sb, o_lru, kx, vx, x, w_branch.astype(BF16), w_out.astype(BF16),
                          row(g_ffn), w_router, b_router, _tile(s, 1024))
    out = _moe(hn.reshape(t, d), comb.reshape(t, LANES), x1.reshape(t, d), w_gu, w_dn, _tile(t, 1024))
    return out.reshape(b, s, d)


def kernel(x, mem, g_mix, w_in, g_q_sb, g_k_sb, conv_w, conv_b, lru_w_a, lru_b_a, lru_w_i, lru_b_i, lru_lambda, g_mem, w_mem_kv, g_q_x, g_k_x, w_branch, w_out, g_ffn, w_group, b_group, w_expert, b_expert, w_gate, w_up, w_down):
    params = (g_mix, w_in, g_q_sb, g_k_sb, conv_w, conv_b, lru_w_a, lru_b_a, lru_w_i, lru_b_i, lru_lambda,
              g_mem, w_mem_kv, g_q_x, g_k_x, w_branch, w_out, g_ffn, w_group, b_group, w_expert, b_expert,
              w_gate, w_up, w_down)
    for layer in range(g_mix.shape[0]):
        x = _layer(x, mem, *[p[layer] for p in params])
    return x
```

```python
import functools

import jax
import jax.numpy as jnp
from jax import lax
from jax.experimental import pallas as pl
from jax.experimental.pallas import tpu as pltpu

F32 = jnp.float32
BF16 = jnp.bfloat16

EPS = 1e-6
SB_HEAD_DIM = 64
SB_WIDTH = 512
LRU_WIDTH = 512
CONV_WIDTH = 4
LRU_C = 8.0
X_HEADS = 4
X_WIDTH = 512
X_HEAD_DIM = 128
N_BRANCH = 3
N_GROUPS = 4
EXPERTS_PER_GROUP = 8
N_EXPERTS = N_GROUPS * EXPERTS_PER_GROUP
EXPERT_FF = 256

LOG2E = 1.4426950408889634
SB_GROUP = 4
SB_GROUP_WIDTH = SB_GROUP * SB_HEAD_DIM
SB_QBLOCKS = 4
BF16_ZERO_EXP = 160.0

MOE_CHUNK = 8
MXU_TILE = 256
MOE_PAD_LOG2 = 4
MOE_PAD = 1 << MOE_PAD_LOG2
MOE_GID_LANE = N_EXPERTS
MOE_POS_LANE = N_EXPERTS + 1
SUBLANES = 8

LANES = 128
BF16_ROWS = 16
VMEM_LIMIT = 56 * 1024 * 1024
MOE_VMEM_LIMIT = 60 * 1024 * 1024

PROJ_TILE = 512
KIND_HEAD64, KIND_HEAD128, KIND_PLAIN, KIND_SIGMOID, KIND_LRU = 0, 1, 2, 3, 4
PROJ_KINDS = (KIND_HEAD64, KIND_HEAD64, KIND_PLAIN, KIND_LRU, KIND_LRU, KIND_HEAD128) + (KIND_SIGMOID,) * 6
W_COL_XL, W_COL_YL = 3, 4
PROJ_TILES_BEFORE_LRU = 2
PROJ_OUT_COL = {j: sum(k != KIND_LRU for k in PROJ_KINDS[:j]) for j, kind in enumerate(PROJ_KINDS) if kind != KIND_LRU}
COL_Q, COL_K, COL_V, COL_QX, COL_GATE = (PROJ_OUT_COL[j] for j in (0, 1, 2, 5, 6))


def _params(sem, vmem_limit=VMEM_LIMIT):
    return pltpu.CompilerParams(dimension_semantics=sem, vmem_limit_bytes=vmem_limit)


def _rms(xf, g):
    return xf * lax.rsqrt(jnp.mean(xf * xf, axis=-1, keepdims=True) + EPS) * g


def _sigmoid(v):
    return 0.5 * jnp.tanh(0.5 * v) + 0.5


def _group_mean_matrix(n, group):
    shift = group.bit_length() - 1
    r = lax.shift_right_logical(lax.broadcasted_iota(jnp.int32, (n, n), 0), shift)
    c = lax.shift_right_logical(lax.broadcasted_iota(jnp.int32, (n, n), 1), shift)
    return jnp.where(r == c, 1.0 / group, 0.0).astype(BF16)


def _proj_kernel(x_ref, g_ref, w_ref, cg_ref, cw_ref, cb_ref, wa_ref, ba_ref, wi_ref, bi_ref, lam_ref,
                 o_ref, olru_ref, tail_ref, h_ref, *, tm, tiles_per_seq):
    seq_tile = pl.program_id(0) % tiles_per_seq

    @pl.when(seq_tile == 0)
    def _():
        tail_ref[...] = jnp.zeros_like(tail_ref)
        h_ref[...] = jnp.zeros_like(h_ref)

    hn = _rms(x_ref[...], g_ref[...]).astype(BF16)
    group_mean = {KIND_HEAD64: _group_mean_matrix(PROJ_TILE, SB_HEAD_DIM)}

    def column_tile(j):
        return jnp.dot(hn, w_ref[:, pl.ds(j * PROJ_TILE, PROJ_TILE)], preferred_element_type=F32)

    def stored_tile(j):
        kind, acc = PROJ_KINDS[j], column_tile(j)
        if kind == KIND_HEAD128:
            sq = acc * acc
            ms = jnp.concatenate(
                [jnp.broadcast_to(jnp.mean(sq[:, h * LANES:(h + 1) * LANES], axis=-1, keepdims=True), (tm, LANES))
                 for h in range(PROJ_TILE // LANES)], axis=1)
            acc = acc * lax.rsqrt(ms + EPS) * cg_ref[:, pl.ds(j * PROJ_TILE, PROJ_TILE)]
        elif kind in group_mean:
            ms = jnp.dot((acc * acc).astype(BF16), group_mean[kind], preferred_element_type=F32)
            acc = acc * lax.rsqrt(ms + EPS) * cg_ref[:, pl.ds(j * PROJ_TILE, PROJ_TILE)]
        elif kind == KIND_SIGMOID:
            acc = _sigmoid(acc)
        o_ref[:, pl.ds(PROJ_OUT_COL[j] * PROJ_TILE, PROJ_TILE)] = acc.astype(o_ref.dtype)

    stored = [j for j, kind in enumerate(PROJ_KINDS) if kind != KIND_LRU]
    x_lru, y_lru = column_tile(W_COL_XL), column_tile(W_COL_YL)
    for j in stored[:PROJ_TILES_BEFORE_LRU]:
        stored_tile(j)

    x3 = x_lru.reshape(tm // SUBLANES, SUBLANES, LRU_WIDTH)
    sub_row = lax.broadcasted_iota(jnp.int32, x3.shape, 1)
    xc3 = cb_ref[...] + cw_ref[pl.ds(CONV_WIDTH - 1, 1), :] * x3
    for d in range(1, CONV_WIDTH):
        rolled = pltpu.roll(x3, d, 1)
        before = jnp.concatenate([pltpu.roll(tail_ref[...], d, 0)[None], rolled[:-1]], axis=0)
        xc3 = xc3 + cw_ref[pl.ds(CONV_WIDTH - 1 - d, 1), :] * jnp.where(sub_row >= d, rolled, before)
    tail_ref[...] = x3[tm // SUBLANES - 1]
    xc = xc3.reshape(tm, LRU_WIDTH)
    xcb = xc.astype(BF16)
    r = _sigmoid(jnp.dot(xcb, wa_ref[...], preferred_element_type=F32) + ba_ref[...])
    gi = _sigmoid(jnp.dot(xcb, wi_ref[...], preferred_element_type=F32) + bi_ref[...])

    nlam = -lam_ref[...]
    sp_nlam = jnp.maximum(nlam, 0.0) + jnp.log(1.0 + jnp.exp(-jnp.abs(nlam)))
    state = h_ref[pl.ds(0, 1), :]
    rest = stored[PROJ_TILES_BEFORE_LRU:]
    chunk = tm // len(rest)
    for c, j in enumerate(rest):
        rows = slice(c * chunk, (c + 1) * chunk)
        log_a = (-LRU_C * r[rows]) * sp_nlam
        a = jnp.exp(log_a)
        th = jnp.tanh(log_a)
        n = -2.0 * th
        coef = jnp.where(n > 0.0, n * lax.rsqrt(n * (1.0 - th)), 0.0)
        b = coef * (gi[rows] * xc[rows])
        a = a.reshape(chunk // SUBLANES, SUBLANES, LRU_WIDTH)
        b = b.reshape(chunk // SUBLANES, SUBLANES, LRU_WIDTH)
        row_in_vreg = lax.broadcasted_iota(jnp.int32, a.shape, 1)
        d = 1
        while d < SUBLANES:
            keep = row_in_vreg >= d
            b = jnp.where(keep, a * pltpu.roll(b, d, 1), 0.0) + b
            a = jnp.where(keep, a * pltpu.roll(a, d, 1), a)
            d *= 2
        pieces = []
        for i in range(chunk // SUBLANES):
            pieces.append(b[i] + a[i] * state)
            state = pieces[-1][SUBLANES - 1:SUBLANES, :]
        h = jnp.concatenate(pieces, axis=0)
        olru_ref[pl.ds(c * chunk, chunk), :] = (h * jax.nn.gelu(y_lru[rows])).astype(olru_ref.dtype)
        stored_tile(j)
    h_ref[...] = jnp.broadcast_to(state, h_ref.shape)


def _in_proj(x2, g_mix, w_cat, col_gain, conv_w, conv_b, wa_bd, b_a, wi_bd, b_i, lam, tm, seq_len):
    t, d = x2.shape
    ncols = len([k for k in PROJ_KINDS if k != KIND_LRU]) * PROJ_TILE
    resident = lambda shape: pl.BlockSpec(shape, lambda i: (0, 0), pipeline_mode=pl.Buffered(1))
    vec = lambda: resident((1, LRU_WIDTH))
    mat = lambda: resident((LRU_WIDTH, LRU_WIDTH))
    return pl.pallas_call(
        functools.partial(_proj_kernel, tm=tm, tiles_per_seq=seq_len // tm),
        grid=(t // tm,),
        in_specs=[
            pl.BlockSpec((tm, d), lambda i: (i, 0)),
            resident((1, d)),
            resident((d, w_cat.shape[1])),
            resident((1, w_cat.shape[1])),
            resident((CONV_WIDTH, LRU_WIDTH)), vec(), mat(), vec(), mat(), vec(), vec(),
        ],
        out_specs=[pl.BlockSpec((tm, ncols), lambda i: (i, 0)),
                   pl.BlockSpec((tm, LRU_WIDTH), lambda i: (i, 0))],
        out_shape=[jax.ShapeDtypeStruct((t, ncols), BF16), jax.ShapeDtypeStruct((t, LRU_WIDTH), BF16)],
        scratch_shapes=[pltpu.VMEM((SUBLANES, LRU_WIDTH), F32), pltpu.VMEM((SUBLANES, LRU_WIDTH), F32)],
        compiler_params=_params(("arbitrary",)),
        name="in_proj_rglru",
    )(x2, g_mix, w_cat, col_gain, conv_w, conv_b, wa_bd, b_a, wi_bd, b_i, lam)


def _sb_attn_kernel(done_ref, q_ref, k_ref, v_ref, o_ref, vexp_ref, acc_ref, carry_ref, cprev_ref, z_ref, d_ref, w_ref,
                    *, tq, nkb):
    nq = acc_ref.shape[0]
    step = pl.program_id(2)
    lane = lax.broadcasted_iota(jnp.int32, (tq, SB_GROUP_WIDTH), 1)
    head_lanes = [(lane >= h * SB_HEAD_DIM) & (lane < (h + 1) * SB_HEAD_DIM) for h in range(SB_GROUP)]
    keep_head = lambda m, a: jnp.where(m, a.astype(F32), 0.0).astype(BF16)

    @pl.when(step == 0)
    def _():
        def fill(j, c):
            v4 = v_ref[0, pl.ds(pl.multiple_of(j * tq, tq), tq), :]
            for h in range(SB_GROUP):
                vexp_ref[j, pl.ds(h * tq, tq), :] = keep_head(head_lanes[h], v4)
            return c
        lax.fori_loop(0, nkb, fill, 0)

    q_h = [[keep_head(m, q_ref[0, pl.ds(u * tq, tq), :]) for m in head_lanes] for u in range(nq)]
    qb = [step * nq + u for u in range(nq)]
    row = lax.broadcasted_iota(jnp.int32, (tq, tq), 0)
    col = lax.broadcasted_iota(jnp.int32, (tq, tq), 1)
    later_or_self = jnp.where(row >= col, 1.0, 0.0).astype(BF16)
    causal = col < row
    heads = range(SB_GROUP)

    def keys(kb):
        return k_ref[0, pl.ds(pl.multiple_of(kb * tq, tq), tq), :]

    def score(u, h, k4):
        return lax.dot_general(q_h[u][h], k4, (((1,), (1,)), ((), ())), preferred_element_type=F32)

    def softplus2(z):
        return jnp.maximum(z, 0.0) + jnp.log(1.0 + jnp.exp2(-jnp.abs(z))) * LOG2E

    def later_sums(sp):
        return jnp.dot(sp.astype(BF16), later_or_self, preferred_element_type=F32)

    has_left = [q > 0 for q in qb]
    left = [jnp.maximum(q - 1, 0) for q in qb]
    k_pair = [(keys(qb[u]), keys(left[u])) for u in range(nq)]
    per_q = 2 * SB_GROUP
    n_chain = nq * per_q
    z_pair, sp_pair, totals, d_pair, w_pair, acc_pair = {}, {}, {}, {}, {}, {}

    def pair_scores(c):
        u, i = divmod(c, per_q)
        z_pair[c] = score(u, i % SB_GROUP, k_pair[u][i // SB_GROUP])

    def pair_softplus(c):
        sp = softplus2(z_pair[c])
        sp_pair[c] = jnp.where(causal, sp, 0.0) if c % per_q < SB_GROUP else sp
        totals[c] = jnp.sum(sp_pair[c], axis=-1, keepdims=True)

    def pair_sums(c):
        d = z_pair[c] - later_sums(sp_pair[c])
        if c % per_q < SB_GROUP:
            d_pair[c] = jnp.where(causal, d, -jnp.inf)
        else:
            d_pair[c] = d - jnp.where(has_left[c // per_q], totals[c - SB_GROUP], jnp.inf)

    def pair_weights(c):
        w_pair[c] = jnp.exp2(d_pair[c]).astype(BF16)
        if c % SB_GROUP == SB_GROUP - 1:
            u, i = divmod(c, per_q)
            w = jnp.concatenate([w_pair[c - SB_GROUP + 1 + h] for h in heads], axis=1)
            kb = qb[u] if i < SB_GROUP else left[u]
            acc_pair[c] = jnp.dot(w, vexp_ref[kb], preferred_element_type=F32)

    stages = (pair_scores, pair_softplus, pair_sums, pair_weights)
    for t in range(n_chain + len(stages) - 1):
        for s, stage in enumerate(stages):
            if 0 <= t - s < n_chain:
                stage(t - s)
    for u in range(nq):
        base = u * per_q
        acc_ref[u] = acc_pair[base + SB_GROUP - 1] + acc_pair[base + per_q - 1]
        for h in heads:
            both = totals[base + h] + jnp.where(has_left[u], totals[base + SB_GROUP + h], 0.0)
            carry_ref[u, h] = jnp.broadcast_to(both, (tq, LANES))

    def stick_left(u):
        least = functools.reduce(jnp.minimum, [carry_ref[u, h] for h in heads])
        return jnp.min(least) < done_ref[0, 0]

    def sweep(u):
        def scores(h, k4):
            z_ref[h] = score(u, h, k4)

        def sums(h):
            z = z_ref[h]
            sp = softplus2(z)
            d_ref[h] = z - later_sums(sp)
            carry = carry_ref[u, h]
            cprev_ref[h] = carry
            carry_ref[u, h] = carry + jnp.sum(sp, axis=-1, keepdims=True)

        def weights(h):
            carry_t = jnp.concatenate([cprev_ref[h]] * (tq // LANES), axis=1)
            w_ref[:, pl.ds(h * tq, tq)] = jnp.exp2(d_ref[h] - carry_t).astype(BF16)

        def values(kb):
            acc_ref[u] += jnp.dot(w_ref[...], vexp_ref[kb], preferred_element_type=F32)

        first = qb[u] - 2
        w_ref[...] = jnp.zeros_like(w_ref)
        k_first = keys(first)
        for h in heads:
            scores(h, k_first)
        k_next = keys(jnp.maximum(first - 1, 0))
        for h in heads:
            sums(h)
            scores(h, k_next)

        def body(c):
            j, _ = c
            values(jnp.minimum(first - j + 2, nkb - 1))
            for h in heads:
                weights(h)
            k_next = keys(jnp.maximum(first - j - 1, 0))
            for h in heads:
                sums(h)
                scores(h, k_next)
            return j + 1, stick_left(u)

        j_end, _ = lax.while_loop(lambda c: (c[0] <= first) & c[1], body, (jnp.int32(1), stick_left(u)))
        values(jnp.minimum(first - j_end + 2, nkb - 1))
        for h in heads:
            weights(h)
        values(first - j_end + 1)

    for u in range(nq):
        pl.when((qb[u] >= 2) & stick_left(u))(functools.partial(sweep, u))
        o_ref[0, pl.ds(u * tq, tq), :] = acc_ref[u].astype(o_ref.dtype)


def _sb_attention(proj3, carry_done, tq):
    b, s, _ = proj3.shape
    w = SB_GROUP_WIDTH
    groups = SB_WIDTH // w
    qoff, koff, voff = (COL_Q * PROJ_TILE // w, COL_K * PROJ_TILE // w, COL_V * PROJ_TILE // w)
    nq = SB_QBLOCKS
    while (s // tq) % nq:
        nq //= 2
    tqs = nq * tq
    return pl.pallas_call(
        functools.partial(_sb_attn_kernel, tq=tq, nkb=s // tq),
        grid=(b, groups, s // tqs),
        in_specs=[
            pl.BlockSpec(memory_space=pltpu.SMEM),
            pl.BlockSpec((1, tqs, w), lambda bi, p, qi: (bi, qi, qoff + p)),
            pl.BlockSpec((1, s, w), lambda bi, p, qi: (bi, 0, koff + p)),
            pl.BlockSpec((1, s, w), lambda bi, p, qi: (bi, 0, voff + p)),
        ],
        out_specs=pl.BlockSpec((1, tqs, w), lambda bi, p, qi: (bi, qi, p)),
        out_shape=jax.ShapeDtypeStruct((b, s, SB_WIDTH), BF16),
        scratch_shapes=[pltpu.VMEM((s // tq, SB_GROUP * tq, w), BF16),
                        pltpu.VMEM((nq, tq, w), F32), pltpu.VMEM((nq, SB_GROUP, tq, LANES), F32),
                        pltpu.VMEM((SB_GROUP, tq, LANES), F32), pltpu.VMEM((SB_GROUP, tq, tq), F32),
                        pltpu.VMEM((SB_GROUP, tq, tq), F32), pltpu.VMEM((tq, SB_GROUP * tq), BF16)],
        compiler_params=_params(("parallel", "parallel", "arbitrary")),
        name="sb_attention",
    )(carry_done, proj3, proj3, proj3)


def _mem_kv_kernel(m_ref, g_ref, w_ref, gk_ref, k_ref, v_ref):
    mn = _rms(m_ref[0], g_ref[...]).astype(BF16)
    kv = jnp.dot(mn, w_ref[...], preferred_element_type=F32)
    k = kv[:, :X_WIDTH]
    ms = jnp.dot((k * k).astype(BF16), _group_mean_matrix(X_WIDTH, X_HEAD_DIM), preferred_element_type=F32)
    k_ref[0] = (k * lax.rsqrt(ms + EPS) * gk_ref[...]).astype(BF16)
    v_ref[0] = kv[:, X_WIDTH:].astype(BF16)


def _mem_kv(mem, g_mem, w_kv, gk_cols):
    b, m, d = mem.shape
    return pl.pallas_call(
        _mem_kv_kernel,
        grid=(b,),
        in_specs=[
            pl.BlockSpec((1, m, d), lambda bi: (bi, 0, 0)),
            pl.BlockSpec((1, d), lambda bi: (0, 0)),
            pl.BlockSpec((d, 2 * X_WIDTH), lambda bi: (0, 0)),
            pl.BlockSpec((1, X_WIDTH), lambda bi: (0, 0)),
        ],
        out_specs=[pl.BlockSpec((1, m, X_WIDTH), lambda bi: (bi, 0, 0))] * 2,
        out_shape=[jax.ShapeDtypeStruct((b, m, X_WIDTH), BF16)] * 2,
        compiler_params=_params(("parallel",)),
        name="mem_kv",
    )(mem, g_mem, w_kv, gk_cols)


def _split_bf16(v):
    hi = v.astype(BF16)
    return hi, (v - hi.astype(F32)).astype(BF16)


def _router(logits):
    lane = lax.broadcasted_iota(jnp.int32, logits.shape, 1).astype(F32)
    ninf = -jnp.inf
    far = float(LANES)
    is_group = (lane >= N_EXPERTS) & (lane < N_EXPERTS + N_GROUPS)
    gl = jnp.where(is_group, logits, ninf)
    gmax = jnp.max(gl, axis=-1, keepdims=True)
    gidx = jnp.min(jnp.where(gl == gmax, lane, far), axis=-1, keepdims=True) - N_EXPERTS
    g_prob = 1.0 / jnp.sum(jnp.exp(gl - gmax), axis=-1, keepdims=True)
    first = gidx * EXPERTS_PER_GROUP
    el = jnp.where((lane >= first) & (lane < first + EXPERTS_PER_GROUP), logits, ninf)
    m1 = jnp.max(el, axis=-1, keepdims=True)
    i1 = jnp.min(jnp.where(el == m1, lane, far), axis=-1, keepdims=True)
    el2 = jnp.where(lane == i1, ninf, el)
    m2 = jnp.max(el2, axis=-1, keepdims=True)
    i2 = jnp.min(jnp.where(el2 == m2, lane, far), axis=-1, keepdims=True)
    e2 = jnp.exp(m2 - m1)
    w1 = 1.0 / (1.0 + e2)
    w2 = e2 / (1.0 + e2)
    combine = g_prob * (jnp.where(lane == i1, w1, 0.0) + jnp.where(lane == i2, w2, 0.0))
    return combine + jnp.where(lane == MOE_GID_LANE, gidx, 0.0)


def _merge_kernel(qx_ref, g0_ref, g1_ref, g2_ref, osb_ref, olru_ref, kx_ref, vx_ref, x_ref,
                  wb_ref, wo_ref, gf_ref, wr_ref, br_ref, x1_ref, hn_ref, comb_ref):
    tm = x_ref.shape[1]
    n_part = 2 if tm % (2 * BF16_ROWS) == 0 else 1
    part = tm // n_part
    head_cols = [slice(h * X_HEAD_DIM, (h + 1) * X_HEAD_DIM) for h in range(X_HEADS)]
    w_hi, w_lo = _split_bf16(wr_ref[...])
    w_hi_lo = jnp.concatenate([w_hi, w_lo], axis=1)
    val = [{} for _ in range(n_part)]

    def scores(a, rows):
        v = val[a]
        qx = qx_ref[0, rows, :]
        v["s"] = [lax.dot_general(qx[:, c], kx_ref[0][:, c], (((1,), (1,)), ((), ())),
                                  preferred_element_type=F32) for c in head_cols]
        v["u01"] = (g0_ref[0, rows, :].astype(F32) * jnp.dot(osb_ref[0, rows, :], wb_ref[0],
                                                               preferred_element_type=F32)
                    + g1_ref[0, rows, :].astype(F32) * jnp.dot(olru_ref[0, rows, :], wb_ref[1],
                                                                 preferred_element_type=F32))

    def attend(a, rows):
        v = val[a]
        outs = []
        for s, c in zip(v["s"], head_cols):
            p = jnp.exp(s - jnp.max(s, axis=-1, keepdims=True))
            p = p / jnp.sum(p, axis=-1, keepdims=True)
            outs.append(jnp.dot(p.astype(BF16), vx_ref[0][:, c], preferred_element_type=F32))
        v["o_x"] = jnp.concatenate(outs, axis=1).astype(BF16)

    def project(a, rows):
        v = val[a]
        merged = v["u01"] + g2_ref[0, rows, :].astype(F32) * jnp.dot(v["o_x"], wb_ref[2],
                                                                       preferred_element_type=F32)
        x1 = x_ref[0, rows, :] + jnp.dot(merged.astype(BF16), wo_ref[...], preferred_element_type=F32)
        x1_ref[0, rows, :] = x1
        v["hn"] = _rms(x1, gf_ref[...])
        hn_ref[0, rows, :] = v["hn"].astype(BF16)

    def route(a, rows):
        h_hi, h_lo = _split_bf16(val[a]["hn"])
        both = jnp.dot(h_hi, w_hi_lo, preferred_element_type=F32)
        logits = both[:, :LANES] + both[:, LANES:] + jnp.dot(h_lo, w_hi, preferred_element_type=F32) + br_ref[...]
        comb_ref[0, rows, :] = _router(logits)

    stages = (scores, attend, project, route)
    for t in range(n_part + len(stages) - 1):
        for s, stage in enumerate(stages):
            a = t - s
            if 0 <= a < n_part:
                stage(a, pl.ds(a * part, part))


def _merge(proj3, o_sb, o_lru, kx, vx, x, w_branch, w_out, g_ffn, w_router, b_router, tm):
    b, s, d = x.shape
    m = kx.shape[1]
    gate_blk = lambda n: pl.BlockSpec((1, tm, d), lambda bi, si, n=n: (bi, si, COL_GATE * PROJ_TILE // d + n))
    tok512 = lambda: pl.BlockSpec((1, tm, SB_WIDTH), lambda bi, si: (bi, si, 0))
    const2 = lambda shape: pl.BlockSpec(shape, lambda bi, si: (0, 0), pipeline_mode=pl.Buffered(1))
    return pl.pallas_call(
        _merge_kernel,
        grid=(b, s // tm),
        in_specs=[
            pl.BlockSpec((1, tm, X_WIDTH), lambda bi, si: (bi, si, COL_QX)),
            gate_blk(0), gate_blk(1), gate_blk(2),
            tok512(), tok512(),
            pl.BlockSpec((1, m, X_WIDTH), lambda bi, si: (bi, 0, 0)),
            pl.BlockSpec((1, m, X_WIDTH), lambda bi, si: (bi, 0, 0)),
            pl.BlockSpec((1, tm, d), lambda bi, si: (bi, si, 0)),
            pl.BlockSpec((N_BRANCH, SB_WIDTH, d), lambda bi, si: (0, 0, 0), pipeline_mode=pl.Buffered(1)),
            const2((d, d)), const2((1, d)), const2((d, LANES)), const2((1, LANES)),
        ],
        out_specs=[
            pl.BlockSpec((1, tm, d), lambda bi, si: (bi, si, 0)),
            pl.BlockSpec((1, tm, d), lambda bi, si: (bi, si, 0)),
            pl.BlockSpec((1, tm, LANES), lambda bi, si: (bi, si, 0)),
        ],
        out_shape=[
            jax.ShapeDtypeStruct((b, s, d), F32),
            jax.ShapeDtypeStruct((b, s, d), BF16),
            jax.ShapeDtypeStruct((b, s, LANES), F32),
        ],
        compiler_params=_params(("parallel", "parallel")),
        name="merge_router",
    )(proj3, proj3, proj3, proj3, o_sb, o_lru, kx, vx, x, w_branch, w_out, g_ffn, w_router, b_router)


def _moe_sort(hn_ref, comb_ref, xs_ref, cs_ref, ys_ref, pt_ref, seg_ref, *, tt, rows, sub):
    comb = comb_ref[...]
    comb_t = comb.T
    gid_row = comb_t[MOE_GID_LANE:MOE_GID_LANE + 1, :]
    group_of_row = lax.broadcasted_iota(jnp.int32, (SUBLANES, tt), 0).astype(F32)
    member = jnp.where(group_of_row == gid_row, 1.0, 0.0)
    r = lax.broadcasted_iota(jnp.int32, (MXU_TILE, MXU_TILE), 0)
    c = lax.broadcasted_iota(jnp.int32, (MXU_TILE, MXU_TILE), 1)
    upto = jnp.where(r <= c, 1.0, 0.0).astype(BF16)
    before = jnp.zeros((SUBLANES, 1), F32)
    ranks = []
    for blk in range(tt // MXU_TILE):
        m = member[:, blk * MXU_TILE:(blk + 1) * MXU_TILE]
        incl = jnp.dot(m.astype(BF16), upto, preferred_element_type=F32) + before
        ranks.append(jnp.sum(m * (incl - m), axis=0, keepdims=True))
        before = before + jnp.sum(m, axis=1, keepdims=True)
    pos_row = jnp.concatenate(ranks, axis=1)
    start = jnp.int32(0)
    for g in range(N_GROUPS):
        count = jnp.sum(member[g:g + 1, :]).astype(jnp.int32)
        padded = lax.shift_left(lax.shift_right_logical(count + (MOE_PAD - 1), MOE_PAD_LOG2), MOE_PAD_LOG2)
        seg_ref[g] = start
        seg_ref[N_GROUPS + g] = padded
        pos_row = pos_row + member[g:g + 1, :] * start.astype(F32)
        start = start + padded

    hi_lo = jnp.concatenate(_split_bf16(comb), axis=1)
    hn = hn_ref[...]
    for blk in range(rows // LANES):
        rr = (lax.broadcasted_iota(jnp.int32, (LANES, tt), 0) + blk * LANES).astype(F32)
        p = jnp.where(rr == pos_row, 1.0, 0.0).astype(BF16)
        sl = pl.ds(blk * LANES, LANES)
        xs_ref[sl, :] = jnp.dot(p, hn, preferred_element_type=F32).astype(BF16)
        hl = jnp.dot(p, hi_lo, preferred_element_type=F32)
        cs_ref[sl, :] = hl[:, :LANES] + hl[:, LANES:]
    tail = pl.ds(rows, sub)
    xs_ref[tail, :] = jnp.zeros((sub, xs_ref.shape[1]), BF16)
    cs_ref[tail, :] = jnp.zeros((sub, LANES), F32)
    ys_ref[...] = jnp.zeros_like(ys_ref)

    sub_t = lax.broadcasted_iota(jnp.int32, (LANES, tt), 0)
    pos_col = jnp.where(sub_t == MOE_POS_LANE, pos_row, comb_t).T[:, MOE_POS_LANE:MOE_POS_LANE + 1]
    for blk in range(tt // LANES):
        cc = lax.broadcasted_iota(jnp.int32, (LANES, rows), 1).astype(F32)
        sl = pl.ds(blk * LANES, LANES)
        pt_ref[sl, :] = jnp.where(cc == pos_col[blk * LANES:(blk + 1) * LANES, :], 1.0, 0.0).astype(BF16)


def _moe_kernel(hn_ref, comb_ref, x1_ref, wgu_ref, wd_ref, o_ref, xs_ref, cs_ref, ys_ref, pt_ref, seg_ref,
                *, tt, rows, sub):
    c = pl.program_id(1)

    @pl.when(c == 0)
    def _():
        _moe_sort(hn_ref, comb_ref, xs_ref, cs_ref, ys_ref, pt_ref, seg_ref, tt=tt, rows=rows, sub=sub)

    group = lax.shift_right_logical(c, (EXPERTS_PER_GROUP // MOE_CHUNK).bit_length() - 1)
    start = seg_ref[group]
    padded = seg_ref[N_GROUPS + group]
    n_sub = sum((padded > k * sub).astype(jnp.int32) for k in range(-(-rows // sub)))
    lane = lax.broadcasted_iota(jnp.int32, (sub, LANES), 1)

    def sub_tile(s, carry):
        sl = pl.ds(pl.multiple_of(start + s * sub, MOE_PAD), sub)
        x = xs_ref[sl, :]
        cw = cs_ref[sl, :]
        acc = jnp.zeros((sub, ys_ref.shape[1]), F32)
        gate_up = {}
        for step in range(MOE_CHUNK + 1):
            if step < MOE_CHUNK:
                gate_up[step] = jnp.dot(x, wgu_ref[step], preferred_element_type=F32)
            if step > 0:
                k = step - 1
                gate, up = gate_up[k][:, :EXPERT_FF], gate_up[k][:, EXPERT_FF:]
                weight = jnp.sum(jnp.where(lane == c * MOE_CHUNK + k, cw, 0.0), axis=-1, keepdims=True)
                act = jax.nn.silu(gate) * up * weight
                acc = acc + jnp.dot(act.astype(BF16), wd_ref[k], preferred_element_type=F32)
        ys_ref[sl, :] = acc.astype(BF16)
        return carry

    lax.fori_loop(0, n_sub, sub_tile, 0)

    @pl.when(c == pl.num_programs(1) - 1)
    def _():
        o_ref[...] = x1_ref[...] + jnp.dot(pt_ref[...], ys_ref[pl.ds(0, rows), :], preferred_element_type=F32)


def _moe(hn2, comb2, x1_2, w_gu, w_down, tt):
    t, d = hn2.shape
    rows = -(-(tt + N_GROUPS * MOE_PAD) // LANES) * LANES
    spread = 3.5 * (tt * (N_GROUPS - 1)) ** 0.5 / N_GROUPS
    sub = min(rows, -(-int(tt / N_GROUPS + spread) // MOE_PAD) * MOE_PAD)
    return pl.pallas_call(
        functools.partial(_moe_kernel, tt=tt, rows=rows, sub=sub),
        grid=(t // tt, N_EXPERTS // MOE_CHUNK),
        in_specs=[
            pl.BlockSpec((tt, d), lambda i, c: (i, 0)),
            pl.BlockSpec((tt, LANES), lambda i, c: (i, 0)),
            pl.BlockSpec((tt, d), lambda i, c: (i, 0)),
            pl.BlockSpec((MOE_CHUNK, d, 2 * EXPERT_FF), lambda i, c: (c, 0, 0)),
            pl.BlockSpec((MOE_CHUNK, EXPERT_FF, d), lambda i, c: (c, 0, 0)),
        ],
        out_specs=pl.BlockSpec((tt, d), lambda i, c: (i, 0)),
        out_shape=jax.ShapeDtypeStruct((t, d), F32),
        scratch_shapes=[pltpu.VMEM((rows + sub, d), BF16), pltpu.VMEM((rows + sub, LANES), F32),
                        pltpu.VMEM((rows + sub, d), BF16), pltpu.VMEM((tt, rows), BF16),
                        pltpu.SMEM((2 * N_GROUPS,), jnp.int32)],
        compiler_params=_params(("parallel", "arbitrary"), MOE_VMEM_LIMIT),
        name="moe",
    )(hn2, comb2, x1_2, w_gu, w_down)


def _block_diag(w):
    n, bd, _ = w.shape
    eye = jnp.eye(n, dtype=w.dtype)
    return jnp.einsum("nij,nm->nimj", w, eye).reshape(n * bd, n * bd)


def _tile(n, pref):
    while n % pref:
        pref //= 2
    return pref


def _layer(x, mem, g_mix, w_in, g_q_sb, g_k_sb, conv_w, conv_b, lru_w_a, lru_b_a, lru_w_i, lru_b_i,
           lru_lambda, g_mem, w_mem_kv, g_q_x, g_k_x, w_branch, w_out, g_ffn, w_group, b_group,
           w_expert, b_expert, w_gate, w_up, w_down):
    b, s, d = x.shape
    t = b * s
    row = lambda v: v.reshape(1, -1).astype(F32)

    ones = jnp.ones((PROJ_TILE,), F32)
    col_gain = jnp.concatenate([
        jnp.tile(g_q_sb, SB_WIDTH // SB_HEAD_DIM) * (SB_HEAD_DIM ** -0.5 * LOG2E),
        jnp.tile(g_k_sb, SB_WIDTH // SB_HEAD_DIM),
        ones, ones, ones,
        jnp.tile(g_q_x, X_HEADS) * X_HEAD_DIM ** -0.5,
    ] + [ones] * 6).reshape(1, -1)
    w_router = jnp.zeros((d, LANES), F32).at[:, :N_EXPERTS].set(w_expert)
    w_router = w_router.at[:, N_EXPERTS:N_EXPERTS + N_GROUPS].set(w_group)
    b_router = jnp.zeros((1, LANES), F32).at[0, :N_EXPERTS].set(b_expert)
    b_router = b_router.at[0, N_EXPERTS:N_EXPERTS + N_GROUPS].set(b_group)
    w_gu = jnp.concatenate([w_gate, w_up], axis=-1).reshape(N_EXPERTS, d, 2 * EXPERT_FF).astype(BF16)
    w_dn = w_down.reshape(N_EXPERTS, EXPERT_FF, d).astype(BF16)

    proj, o_lru = _in_proj(x.reshape(t, d), row(g_mix), w_in.astype(BF16), col_gain, conv_w, row(conv_b),
                           _block_diag(lru_w_a).astype(BF16), row(lru_b_a), _block_diag(lru_w_i).astype(BF16),
                           row(lru_b_i), row(lru_lambda), _tile(s, 512), s)
    proj3 = proj.reshape(b, s, -1)
    o_lru = o_lru.reshape(b, s, LRU_WIDTH)
    q_gain = jnp.abs(col_gain[0, :PROJ_TILE])
    z_max = 1.05 * SB_HEAD_DIM * jnp.max(q_gain) * jnp.max(jnp.abs(g_k_sb))
    carry_done = (z_max + BF16_ZERO_EXP).reshape(1, 1).astype(F32)
    o_sb = _sb_attention(proj3, carry_done, _tile(s, 256))
    kx, vx = _mem_kv(mem, row(g_mem), w_mem_kv.astype(BF16), row(jnp.tile(g_k_x, X_HEADS)))
    x1, hn, comb = _merge(proj3, o_sb, o_lru, kx, vx, x, w_branch.astype(BF16), w_out.astype(BF16),
                          row(g_ffn), w_router, b_router, _tile(s, 1024))
    out = _moe(hn.reshape(t, d), comb.reshape(t, LANES), x1.reshape(t, d), w_gu, w_dn, _tile(t, 1024))
    return out.reshape(b, s, d)


def kernel(x, mem, g_mix, w_in, g_q_sb, g_k_sb, conv_w, conv_b, lru_w_a, lru_b_a, lru_w_i, lru_b_i, lru_lambda, g_mem, w_mem_kv, g_q_x, g_k_x, w_branch, w_out, g_ffn, w_group, b_group, w_expert, b_expert, w_gate, w_up, w_down):
    params = (g_mix, w_in, g_q_sb, g_k_sb, conv_w, conv_b, lru_w_a, lru_b_a, lru_w_i, lru_b_i, lru_lambda,
              g_mem, w_mem_kv, g_q_x, g_k_x, w_branch, w_out, g_ffn, w_group, b_group, w_expert, b_expert,
              w_gate, w_up, w_down)
    for layer in range(g_mix.shape[0]):
        x = _layer(x, mem, *[p[layer] for p in params])
    return x
```

```python
import functools

import jax
import jax.numpy as jnp
from jax import lax
from jax.experimental import pallas as pl
from jax.experimental.pallas import tpu as pltpu

F32 = jnp.float32
BF16 = jnp.bfloat16

EPS = 1e-6
SB_HEAD_DIM = 64
SB_WIDTH = 512
LRU_WIDTH = 512
CONV_WIDTH = 4
LRU_C = 8.0
X_HEADS = 4
X_WIDTH = 512
X_HEAD_DIM = 128
N_BRANCH = 3
N_GROUPS = 4
EXPERTS_PER_GROUP = 8
N_EXPERTS = N_GROUPS * EXPERTS_PER_GROUP
EXPERT_FF = 256

LOG2E = 1.4426950408889634
SB_GROUP = 4
SB_GROUP_WIDTH = SB_GROUP * SB_HEAD_DIM
SB_QBLOCKS = 4
BF16_ZERO_EXP = 160.0

MOE_CHUNK = 8
MXU_TILE = 256
MOE_PAD_LOG2 = 4
MOE_PAD = 1 << MOE_PAD_LOG2
MOE_GID_LANE = N_EXPERTS
MOE_POS_LANE = N_EXPERTS + 1
SUBLANES = 8

LANES = 128
BF16_ROWS = 16
VMEM_LIMIT = 56 * 1024 * 1024
MOE_VMEM_LIMIT = 60 * 1024 * 1024

PROJ_TILE = 512
KIND_HEAD64, KIND_HEAD128, KIND_PLAIN, KIND_SIGMOID, KIND_LRU = 0, 1, 2, 3, 4
PROJ_KINDS = (KIND_HEAD64, KIND_HEAD64, KIND_PLAIN, KIND_LRU, KIND_LRU, KIND_HEAD128) + (KIND_SIGMOID,) * 6
W_COL_XL, W_COL_YL = 3, 4
PROJ_TILES_BEFORE_LRU = 2
PROJ_OUT_COL = {j: sum(k != KIND_LRU for k in PROJ_KINDS[:j]) for j, kind in enumerate(PROJ_KINDS) if kind != KIND_LRU}
COL_Q, COL_K, COL_V, COL_QX, COL_GATE = (PROJ_OUT_COL[j] for j in (0, 1, 2, 5, 6))


def _params(sem, vmem_limit=VMEM_LIMIT):
    return pltpu.CompilerParams(dimension_semantics=sem, vmem_limit_bytes=vmem_limit)


def _rms(xf, g):
    return xf * lax.rsqrt(jnp.mean(xf * xf, axis=-1, keepdims=True) + EPS) * g


def _sigmoid(v):
    return 0.5 * jnp.tanh(0.5 * v) + 0.5


def _group_mean_matrix(n, group):
    shift = group.bit_length() - 1
    r = lax.shift_right_logical(lax.broadcasted_iota(jnp.int32, (n, n), 0), shift)
    c = lax.shift_right_logical(lax.broadcasted_iota(jnp.int32, (n, n), 1), shift)
    return jnp.where(r == c, 1.0 / group, 0.0).astype(BF16)


def _proj_kernel(x_ref, g_ref, w_ref, cg_ref, cw_ref, cb_ref, wa_ref, ba_ref, wi_ref, bi_ref, lam_ref,
                 o_ref, olru_ref, tail_ref, h_ref, *, tm, tiles_per_seq):
    seq_tile = pl.program_id(0) % tiles_per_seq

    @pl.when(seq_tile == 0)
    def _():
        tail_ref[...] = jnp.zeros_like(tail_ref)
        h_ref[...] = jnp.zeros_like(h_ref)

    hn = _rms(x_ref[...], g_ref[...]).astype(BF16)

    def column_tile(j):
        return jnp.dot(hn, w_ref[:, pl.ds(j * PROJ_TILE, PROJ_TILE)], preferred_element_type=F32)

    def stored_tile(j):
        kind, acc = PROJ_KINDS[j], column_tile(j)
        if kind == KIND_HEAD128:
            sq = acc * acc
            ms = jnp.concatenate(
                [jnp.broadcast_to(jnp.mean(sq[:, h * LANES:(h + 1) * LANES], axis=-1, keepdims=True), (tm, LANES))
                 for h in range(PROJ_TILE // LANES)], axis=1)
            acc = acc * lax.rsqrt(ms + EPS) * cg_ref[:, pl.ds(j * PROJ_TILE, PROJ_TILE)]
        elif kind == KIND_HEAD64:
            sq = acc * acc
            low = lax.broadcasted_iota(jnp.int32, (tm, LANES), 1) < SB_HEAD_DIM
            parts = []
            for h in range(PROJ_TILE // LANES):
                s2 = sq[:, h * LANES:(h + 1) * LANES]
                first = jnp.sum(jnp.where(low, s2, 0.0), axis=-1, keepdims=True)
                second = jnp.sum(jnp.where(low, 0.0, s2), axis=-1, keepdims=True)
                parts.append(jnp.where(low, first, second) * (1.0 / SB_HEAD_DIM))
            acc = acc * lax.rsqrt(jnp.concatenate(parts, axis=1) + EPS) * cg_ref[:, pl.ds(j * PROJ_TILE, PROJ_TILE)]
        elif kind == KIND_SIGMOID:
            acc = _sigmoid(acc)
        o_ref[:, pl.ds(PROJ_OUT_COL[j] * PROJ_TILE, PROJ_TILE)] = acc.astype(o_ref.dtype)

    stored = [j for j, kind in enumerate(PROJ_KINDS) if kind != KIND_LRU]
    x_lru, y_lru = column_tile(W_COL_XL), column_tile(W_COL_YL)
    for j in stored[:PROJ_TILES_BEFORE_LRU]:
        stored_tile(j)

    x3 = x_lru.reshape(tm // SUBLANES, SUBLANES, LRU_WIDTH)
    sub_row = lax.broadcasted_iota(jnp.int32, x3.shape, 1)
    xc3 = cb_ref[...] + cw_ref[pl.ds(CONV_WIDTH - 1, 1), :] * x3
    for d in range(1, CONV_WIDTH):
        rolled = pltpu.roll(x3, d, 1)
        before = jnp.concatenate([pltpu.roll(tail_ref[...], d, 0)[None], rolled[:-1]], axis=0)
        xc3 = xc3 + cw_ref[pl.ds(CONV_WIDTH - 1 - d, 1), :] * jnp.where(sub_row >= d, rolled, before)
    tail_ref[...] = x3[tm // SUBLANES - 1]
    xc = xc3.reshape(tm, LRU_WIDTH)
    xcb = xc.astype(BF16)
    r = _sigmoid(jnp.dot(xcb, wa_ref[...], preferred_element_type=F32) + ba_ref[...])
    gi = _sigmoid(jnp.dot(xcb, wi_ref[...], preferred_element_type=F32) + bi_ref[...])

    nlam = -lam_ref[...]
    sp_nlam = jnp.maximum(nlam, 0.0) + jnp.log(1.0 + jnp.exp(-jnp.abs(nlam)))
    state = h_ref[pl.ds(0, 1), :]
    rest = stored[PROJ_TILES_BEFORE_LRU:]
    chunk = tm // len(rest)
    for c, j in enumerate(rest):
        rows = slice(c * chunk, (c + 1) * chunk)
        log_a = (-LRU_C * r[rows]) * sp_nlam
        a = jnp.exp(log_a)
        th = jnp.tanh(log_a)
        n = -2.0 * th
        coef = jnp.where(n > 0.0, n * lax.rsqrt(n * (1.0 - th)), 0.0)
        b = coef * (gi[rows] * xc[rows])
        a = a.reshape(chunk // SUBLANES, SUBLANES, LRU_WIDTH)
        b = b.reshape(chunk // SUBLANES, SUBLANES, LRU_WIDTH)
        row_in_vreg = lax.broadcasted_iota(jnp.int32, a.shape, 1)
        d = 1
        while d < SUBLANES:
            keep = row_in_vreg >= d
            b = jnp.where(keep, a * pltpu.roll(b, d, 1), 0.0) + b
            a = jnp.where(keep, a * pltpu.roll(a, d, 1), a)
            d *= 2
        pieces = []
        for i in range(chunk // SUBLANES):
            pieces.append(b[i] + a[i] * state)
            state = pieces[-1][SUBLANES - 1:SUBLANES, :]
        h = jnp.concatenate(pieces, axis=0)
        olru_ref[pl.ds(c * chunk, chunk), :] = (h * jax.nn.gelu(y_lru[rows])).astype(olru_ref.dtype)
        stored_tile(j)
    h_ref[...] = jnp.broadcast_to(state, h_ref.shape)


def _in_proj(x2, g_mix, w_cat, col_gain, conv_w, conv_b, wa_bd, b_a, wi_bd, b_i, lam, tm, seq_len):
    t, d = x2.shape
    ncols = len([k for k in PROJ_KINDS if k != KIND_LRU]) * PROJ_TILE
    resident = lambda shape: pl.BlockSpec(shape, lambda i: (0, 0), pipeline_mode=pl.Buffered(1))
    vec = lambda: resident((1, LRU_WIDTH))
    mat = lambda: resident((LRU_WIDTH, LRU_WIDTH))
    return pl.pallas_call(
        functools.partial(_proj_kernel, tm=tm, tiles_per_seq=seq_len // tm),
        grid=(t // tm,),
        in_specs=[
            pl.BlockSpec((tm, d), lambda i: (i, 0)),
            resident((1, d)),
            resident((d, w_cat.shape[1])),
            resident((1, w_cat.shape[1])),
            resident((CONV_WIDTH, LRU_WIDTH)), vec(), mat(), vec(), mat(), vec(), vec(),
        ],
        out_specs=[pl.BlockSpec((tm, ncols), lambda i: (i, 0)),
                   pl.BlockSpec((tm, LRU_WIDTH), lambda i: (i, 0))],
        out_shape=[jax.ShapeDtypeStruct((t, ncols), BF16), jax.ShapeDtypeStruct((t, LRU_WIDTH), BF16)],
        scratch_shapes=[pltpu.VMEM((SUBLANES, LRU_WIDTH), F32), pltpu.VMEM((SUBLANES, LRU_WIDTH), F32)],
        compiler_params=_params(("arbitrary",)),
        name="in_proj_rglru",
    )(x2, g_mix, w_cat, col_gain, conv_w, conv_b, wa_bd, b_a, wi_bd, b_i, lam)


def _sb_attn_kernel(done_ref, q_ref, k_ref, v_ref, o_ref, vexp_ref, acc_ref, carry_ref, cprev_ref, z_ref, d_ref, w_ref,
                    *, tq, nkb):
    nq = acc_ref.shape[0]
    step = pl.program_id(2)
    lane = lax.broadcasted_iota(jnp.int32, (tq, SB_GROUP_WIDTH), 1)
    head_lanes = [(lane >= h * SB_HEAD_DIM) & (lane < (h + 1) * SB_HEAD_DIM) for h in range(SB_GROUP)]
    keep_head = lambda m, a: jnp.where(m, a.astype(F32), 0.0).astype(BF16)

    @pl.when(step == 0)
    def _():
        def fill(j, c):
            v4 = v_ref[0, pl.ds(pl.multiple_of(j * tq, tq), tq), :]
            for h in range(SB_GROUP):
                vexp_ref[j, pl.ds(h * tq, tq), :] = keep_head(head_lanes[h], v4)
            return c
        lax.fori_loop(0, nkb, fill, 0)

    q_h = [[keep_head(m, q_ref[0, pl.ds(u * tq, tq), :]) for m in head_lanes] for u in range(nq)]
    qb = [step * nq + u for u in range(nq)]
    row = lax.broadcasted_iota(jnp.int32, (tq, tq), 0)
    col = lax.broadcasted_iota(jnp.int32, (tq, tq), 1)
    later_or_self = jnp.where(row >= col, 1.0, 0.0).astype(BF16)
    causal = col < row
    heads = range(SB_GROUP)

    def keys(kb):
        return k_ref[0, pl.ds(pl.multiple_of(kb * tq, tq), tq), :]

    def score(u, h, k4):
        return lax.dot_general(q_h[u][h], k4, (((1,), (1,)), ((), ())), preferred_element_type=F32)

    def softplus2(z):
        return jnp.maximum(z, 0.0) + jnp.log(1.0 + jnp.exp2(-jnp.abs(z))) * LOG2E

    def later_sums(sp):
        return jnp.dot(sp.astype(BF16), later_or_self, preferred_element_type=F32)

    has_left = [q > 0 for q in qb]
    left = [jnp.maximum(q - 1, 0) for q in qb]
    k_pair = [(keys(qb[u]), keys(left[u])) for u in range(nq)]
    per_q = 2 * SB_GROUP
    n_chain = nq * per_q
    z_pair, sp_pair, totals, d_pair, w_pair, acc_pair = {}, {}, {}, {}, {}, {}

    def pair_scores(c):
        u, i = divmod(c, per_q)
        z_pair[c] = score(u, i % SB_GROUP, k_pair[u][i // SB_GROUP])

    def pair_softplus(c):
        sp = softplus2(z_pair[c])
        sp_pair[c] = jnp.where(causal, sp, 0.0) if c % per_q < SB_GROUP else sp
        totals[c] = jnp.sum(sp_pair[c], axis=-1, keepdims=True)

    def pair_sums(c):
        d = z_pair[c] - later_sums(sp_pair[c])
        if c % per_q < SB_GROUP:
            d_pair[c] = jnp.where(causal, d, -jnp.inf)
        else:
            d_pair[c] = d - jnp.where(has_left[c // per_q], totals[c - SB_GROUP], jnp.inf)

    def pair_weights(c):
        w_pair[c] = jnp.exp2(d_pair[c]).astype(BF16)
        if c % SB_GROUP == SB_GROUP - 1:
            u, i = divmod(c, per_q)
            w = jnp.concatenate([w_pair[c - SB_GROUP + 1 + h] for h in heads], axis=1)
            kb = qb[u] if i < SB_GROUP else left[u]
            acc_pair[c] = jnp.dot(w, vexp_ref[kb], preferred_element_type=F32)

    stages = (pair_scores, pair_softplus, pair_sums, pair_weights)
    for t in range(n_chain + len(stages) - 1):
        for s, stage in enumerate(stages):
            if 0 <= t - s < n_chain:
                stage(t - s)
    for u in range(nq):
        base = u * per_q
        acc_ref[u] = acc_pair[base + SB_GROUP - 1] + acc_pair[base + per_q - 1]
        for h in heads:
            both = totals[base + h] + jnp.where(has_left[u], totals[base + SB_GROUP + h], 0.0)
            carry_ref[u, h] = jnp.broadcast_to(both, (tq, LANES))

    def stick_left(u):
        least = functools.reduce(jnp.minimum, [carry_ref[u, h] for h in heads])
        return jnp.min(least) < done_ref[0, 0]

    def sweep(u):
        def scores(h, k4):
            z_ref[h] = score(u, h, k4)

        def sums(h):
            z = z_ref[h]
            sp = softplus2(z)
            d_ref[h] = z - later_sums(sp)
            carry = carry_ref[u, h]
            cprev_ref[h] = carry
            carry_ref[u, h] = carry + jnp.sum(sp, axis=-1, keepdims=True)

        def weights(h):
            carry_t = jnp.concatenate([cprev_ref[h]] * (tq // LANES), axis=1)
            w_ref[:, pl.ds(h * tq, tq)] = jnp.exp2(d_ref[h] - carry_t).astype(BF16)

        def values(kb):
            acc_ref[u] += jnp.dot(w_ref[...], vexp_ref[kb], preferred_element_type=F32)

        first = qb[u] - 2
        w_ref[...] = jnp.zeros_like(w_ref)
        k_first = keys(first)
        for h in heads:
            scores(h, k_first)
        k_next = keys(jnp.maximum(first - 1, 0))
        for h in heads:
            sums(h)
            scores(h, k_next)

        def body(c):
            j, _ = c
            values(jnp.minimum(first - j + 2, nkb - 1))
            for h in heads:
                weights(h)
            k_next = keys(jnp.maximum(first - j - 1, 0))
            for h in heads:
                sums(h)
                scores(h, k_next)
            return j + 1, stick_left(u)

        j_end, _ = lax.while_loop(lambda c: (c[0] <= first) & c[1], body, (jnp.int32(1), stick_left(u)))
        values(jnp.minimum(first - j_end + 2, nkb - 1))
        for h in heads:
            weights(h)
        values(first - j_end + 1)

    for u in range(nq):
        pl.when((qb[u] >= 2) & stick_left(u))(functools.partial(sweep, u))
        o_ref[0, pl.ds(u * tq, tq), :] = acc_ref[u].astype(o_ref.dtype)


def _sb_attention(proj3, carry_done, tq):
    b, s, _ = proj3.shape
    w = SB_GROUP_WIDTH
    groups = SB_WIDTH // w
    qoff, koff, voff = (COL_Q * PROJ_TILE // w, COL_K * PROJ_TILE // w, COL_V * PROJ_TILE // w)
    nq = SB_QBLOCKS
    while (s // tq) % nq:
        nq //= 2
    tqs = nq * tq
    return pl.pallas_call(
        functools.partial(_sb_attn_kernel, tq=tq, nkb=s // tq),
        grid=(b, groups, s // tqs),
        in_specs=[
            pl.BlockSpec(memory_space=pltpu.SMEM),
            pl.BlockSpec((1, tqs, w), lambda bi, p, qi: (bi, qi, qoff + p)),
            pl.BlockSpec((1, s, w), lambda bi, p, qi: (bi, 0, koff + p)),
            pl.BlockSpec((1, s, w), lambda bi, p, qi: (bi, 0, voff + p)),
        ],
        out_specs=pl.BlockSpec((1, tqs, w), lambda bi, p, qi: (bi, qi, p)),
        out_shape=jax.ShapeDtypeStruct((b, s, SB_WIDTH), BF16),
        scratch_shapes=[pltpu.VMEM((s // tq, SB_GROUP * tq, w), BF16),
                        pltpu.VMEM((nq, tq, w), F32), pltpu.VMEM((nq, SB_GROUP, tq, LANES), F32),
                        pltpu.VMEM((SB_GROUP, tq, LANES), F32), pltpu.VMEM((SB_GROUP, tq, tq), F32),
                        pltpu.VMEM((SB_GROUP, tq, tq), F32), pltpu.VMEM((tq, SB_GROUP * tq), BF16)],
        compiler_params=_params(("parallel", "parallel", "arbitrary")),
        name="sb_attention",
    )(carry_done, proj3, proj3, proj3)


def _mem_kv_kernel(m_ref, g_ref, w_ref, gk_ref, k_ref, v_ref):
    mn = _rms(m_ref[0], g_ref[...]).astype(BF16)
    kv = jnp.dot(mn, w_ref[...], preferred_element_type=F32)
    k = kv[:, :X_WIDTH]
    ms = jnp.dot((k * k).astype(BF16), _group_mean_matrix(X_WIDTH, X_HEAD_DIM), preferred_element_type=F32)
    k_ref[0] = (k * lax.rsqrt(ms + EPS) * gk_ref[...]).astype(BF16)
    v_ref[0] = kv[:, X_WIDTH:].astype(BF16)


def _mem_kv(mem, g_mem, w_kv, gk_cols):
    b, m, d = mem.shape
    return pl.pallas_call(
        _mem_kv_kernel,
        grid=(b,),
        in_specs=[
            pl.BlockSpec((1, m, d), lambda bi: (bi, 0, 0)),
            pl.BlockSpec((1, d), lambda bi: (0, 0)),
            pl.BlockSpec((d, 2 * X_WIDTH), lambda bi: (0, 0)),
            pl.BlockSpec((1, X_WIDTH), lambda bi: (0, 0)),
        ],
        out_specs=[pl.BlockSpec((1, m, X_WIDTH), lambda bi: (bi, 0, 0))] * 2,
        out_shape=[jax.ShapeDtypeStruct((b, m, X_WIDTH), BF16)] * 2,
        compiler_params=_params(("parallel",)),
        name="mem_kv",
    )(mem, g_mem, w_kv, gk_cols)


def _split_bf16(v):
    hi = v.astype(BF16)
    return hi, (v - hi.astype(F32)).astype(BF16)


def _router(logits):
    lane = lax.broadcasted_iota(jnp.int32, logits.shape, 1).astype(F32)
    ninf = -jnp.inf
    far = float(LANES)
    is_group = (lane >= N_EXPERTS) & (lane < N_EXPERTS + N_GROUPS)
    gl = jnp.where(is_group, logits, ninf)
    gmax = jnp.max(gl, axis=-1, keepdims=True)
    gidx = jnp.min(jnp.where(gl == gmax, lane, far), axis=-1, keepdims=True) - N_EXPERTS
    g_prob = 1.0 / jnp.sum(jnp.exp(gl - gmax), axis=-1, keepdims=True)
    first = gidx * EXPERTS_PER_GROUP
    el = jnp.where((lane >= first) & (lane < first + EXPERTS_PER_GROUP), logits, ninf)
    m1 = jnp.max(el, axis=-1, keepdims=True)
    i1 = jnp.min(jnp.where(el == m1, lane, far), axis=-1, keepdims=True)
    el2 = jnp.where(lane == i1, ninf, el)
    m2 = jnp.max(el2, axis=-1, keepdims=True)
    i2 = jnp.min(jnp.where(el2 == m2, lane, far), axis=-1, keepdims=True)
    e2 = jnp.exp(m2 - m1)
    w1 = 1.0 / (1.0 + e2)
    w2 = e2 / (1.0 + e2)
    combine = g_prob * (jnp.where(lane == i1, w1, 0.0) + jnp.where(lane == i2, w2, 0.0))
    return combine + jnp.where(lane == MOE_GID_LANE, gidx, 0.0)


def _merge_kernel(qx_ref, g0_ref, g1_ref, g2_ref, osb_ref, olru_ref, kx_ref, vx_ref, x_ref,
                  wb_ref, wo_ref, gf_ref, wr_ref, br_ref, x1_ref, hn_ref, comb_ref):
    tm = x_ref.shape[1]
    n_part = 2 if tm % (2 * BF16_ROWS) == 0 else 1
    part = tm // n_part
    head_cols = [slice(h * X_HEAD_DIM, (h + 1) * X_HEAD_DIM) for h in range(X_HEADS)]
    w_hi, w_lo = _split_bf16(wr_ref[...])
    w_hi_lo = jnp.concatenate([w_hi, w_lo], axis=1)
    val = [{} for _ in range(n_part)]

    def scores(a, rows):
        v = val[a]
        qx = qx_ref[0, rows, :]
        v["s"] = [lax.dot_general(qx[:, c], kx_ref[0][:, c], (((1,), (1,)), ((), ())),
                                  preferred_element_type=F32) for c in head_cols]
        v["u01"] = (g0_ref[0, rows, :].astype(F32) * jnp.dot(osb_ref[0, rows, :], wb_ref[0],
                                                               preferred_element_type=F32)
                    + g1_ref[0, rows, :].astype(F32) * jnp.dot(olru_ref[0, rows, :], wb_ref[1],
                                                                 preferred_element_type=F32))

    def attend(a, rows):
        v = val[a]
        outs = []
        for s, c in zip(v["s"], head_cols):
            p = jnp.exp(s - jnp.max(s, axis=-1, keepdims=True))
            p = p / jnp.sum(p, axis=-1, keepdims=True)
            outs.append(jnp.dot(p.astype(BF16), vx_ref[0][:, c], preferred_element_type=F32))
        v["o_x"] = jnp.concatenate(outs, axis=1).astype(BF16)

    def project(a, rows):
        v = val[a]
        merged = v["u01"] + g2_ref[0, rows, :].astype(F32) * jnp.dot(v["o_x"], wb_ref[2],
                                                                       preferred_element_type=F32)
        x1 = x_ref[0, rows, :] + jnp.dot(merged.astype(BF16), wo_ref[...], preferred_element_type=F32)
        x1_ref[0, rows, :] = x1
        v["hn"] = _rms(x1, gf_ref[...])
        hn_ref[0, rows, :] = v["hn"].astype(BF16)

    def route(a, rows):
        h_hi, h_lo = _split_bf16(val[a]["hn"])
        both = jnp.dot(h_hi, w_hi_lo, preferred_element_type=F32)
        logits = both[:, :LANES] + both[:, LANES:] + jnp.dot(h_lo, w_hi, preferred_element_type=F32) + br_ref[...]
        comb_ref[0, rows, :] = _router(logits)

    stages = (scores, attend, project, route)
    for t in range(n_part + len(stages) - 1):
        for s, stage in enumerate(stages):
            a = t - s
            if 0 <= a < n_part:
                stage(a, pl.ds(a * part, part))


def _merge(proj3, o_sb, o_lru, kx, vx, x, w_branch, w_out, g_ffn, w_router, b_router, tm):
    b, s, d = x.shape
    m = kx.shape[1]
    gate_blk = lambda n: pl.BlockSpec((1, tm, d), lambda bi, si, n=n: (bi, si, COL_GATE * PROJ_TILE // d + n))
    tok512 = lambda: pl.BlockSpec((1, tm, SB_WIDTH), lambda bi, si: (bi, si, 0))
    const2 = lambda shape: pl.BlockSpec(shape, lambda bi, si: (0, 0), pipeline_mode=pl.Buffered(1))
    return pl.pallas_call(
        _merge_kernel,
        grid=(b, s // tm),
        in_specs=[
            pl.BlockSpec((1, tm, X_WIDTH), lambda bi, si: (bi, si, COL_QX)),
            gate_blk(0), gate_blk(1), gate_blk(2),
            tok512(), tok512(),
            pl.BlockSpec((1, m, X_WIDTH), lambda bi, si: (bi, 0, 0)),
            pl.BlockSpec((1, m, X_WIDTH), lambda bi, si: (bi, 0, 0)),
            pl.BlockSpec((1, tm, d), lambda bi, si: (bi, si, 0)),
            pl.BlockSpec((N_BRANCH, SB_WIDTH, d), lambda bi, si: (0, 0, 0), pipeline_mode=pl.Buffered(1)),
            const2((d, d)), const2((1, d)), const2((d, LANES)), const2((1, LANES)),
        ],
        out_specs=[
            pl.BlockSpec((1, tm, d), lambda bi, si: (bi, si, 0)),
            pl.BlockSpec((1, tm, d), lambda bi, si: (bi, si, 0)),
            pl.BlockSpec((1, tm, LANES), lambda bi, si: (bi, si, 0)),
        ],
        out_shape=[
            jax.ShapeDtypeStruct((b, s, d), F32),
            jax.ShapeDtypeStruct((b, s, d), BF16),
            jax.ShapeDtypeStruct((b, s, LANES), F32),
        ],
        compiler_params=_params(("parallel", "parallel")),
        name="merge_router",
    )(proj3, proj3, proj3, proj3, o_sb, o_lru, kx, vx, x, w_branch, w_out, g_ffn, w_router, b_router)


def _moe_sort(hn_ref, comb_ref, xs_ref, cs_ref, ys_ref, pt_ref, seg_ref, *, tt, rows, sub):
    comb = comb_ref[...]
    comb_t = comb.T
    gid_row = comb_t[MOE_GID_LANE:MOE_GID_LANE + 1, :]
    group_of_row = lax.broadcasted_iota(jnp.int32, (SUBLANES, tt), 0).astype(F32)
    member = jnp.where(group_of_row == gid_row, 1.0, 0.0)
    r = lax.broadcasted_iota(jnp.int32, (MXU_TILE, MXU_TILE), 0)
    c = lax.broadcasted_iota(jnp.int32, (MXU_TILE, MXU_TILE), 1)
    upto = jnp.where(r <= c, 1.0, 0.0).astype(BF16)
    before = jnp.zeros((SUBLANES, 1), F32)
    ranks = []
    for blk in range(tt // MXU_TILE):
        m = member[:, blk * MXU_TILE:(blk + 1) * MXU_TILE]
        incl = jnp.dot(m.astype(BF16), upto, preferred_element_type=F32) + before
        ranks.append(jnp.sum(m * (incl - m), axis=0, keepdims=True))
        before = before + jnp.sum(m, axis=1, keepdims=True)
    pos_row = jnp.concatenate(ranks, axis=1)
    start = jnp.int32(0)
    for g in range(N_GROUPS):
        count = jnp.sum(member[g:g + 1, :]).astype(jnp.int32)
        padded = lax.shift_left(lax.shift_right_logical(count + (MOE_PAD - 1), MOE_PAD_LOG2), MOE_PAD_LOG2)
        seg_ref[g] = start
        seg_ref[N_GROUPS + g] = padded
        pos_row = pos_row + member[g:g + 1, :] * start.astype(F32)
        start = start + padded

    hi_lo = jnp.concatenate(_split_bf16(comb), axis=1)
    hn = hn_ref[...]
    for blk in range(rows // LANES):
        rr = (lax.broadcasted_iota(jnp.int32, (LANES, tt), 0) + blk * LANES).astype(F32)
        p = jnp.where(rr == pos_row, 1.0, 0.0).astype(BF16)
        sl = pl.ds(blk * LANES, LANES)
        xs_ref[sl, :] = jnp.dot(p, hn, preferred_element_type=F32).astype(BF16)
        hl = jnp.dot(p, hi_lo, preferred_element_type=F32)
        cs_ref[sl, :] = hl[:, :LANES] + hl[:, LANES:]
    tail = pl.ds(rows, sub)
    xs_ref[tail, :] = jnp.zeros((sub, xs_ref.shape[1]), BF16)
    cs_ref[tail, :] = jnp.zeros((sub, LANES), F32)
    ys_ref[...] = jnp.zeros_like(ys_ref)

    sub_t = lax.broadcasted_iota(jnp.int32, (LANES, tt), 0)
    pos_col = jnp.where(sub_t == MOE_POS_LANE, pos_row, comb_t).T[:, MOE_POS_LANE:MOE_POS_LANE + 1]
    for blk in range(tt // LANES):
        cc = lax.broadcasted_iota(jnp.int32, (LANES, rows), 1).astype(F32)
        sl = pl.ds(blk * LANES, LANES)
        pt_ref[sl, :] = jnp.where(cc == pos_col[blk * LANES:(blk + 1) * LANES, :], 1.0, 0.0).astype(BF16)


def _moe_kernel(hn_ref, comb_ref, x1_ref, wgu_ref, wd_ref, o_ref, xs_ref, cs_ref, ys_ref, pt_ref, seg_ref,
                *, tt, rows, sub):
    c = pl.program_id(1)

    @pl.when(c == 0)
    def _():
        _moe_sort(hn_ref, comb_ref, xs_ref, cs_ref, ys_ref, pt_ref, seg_ref, tt=tt, rows=rows, sub=sub)

    group = lax.shift_right_logical(c, (EXPERTS_PER_GROUP // MOE_CHUNK).bit_length() - 1)
    start = seg_ref[group]
    padded = seg_ref[N_GROUPS + group]
    n_sub = sum((padded > k * sub).astype(jnp.int32) for k in range(-(-rows // sub)))
    lane = lax.broadcasted_iota(jnp.int32, (sub, LANES), 1)

    def sub_tile(s, carry):
        sl = pl.ds(pl.multiple_of(start + s * sub, MOE_PAD), sub)
        x = xs_ref[sl, :]
        cw = cs_ref[sl, :]
        acc = jnp.zeros((sub, ys_ref.shape[1]), F32)
        gate_up = {}
        for step in range(MOE_CHUNK + 1):
            if step < MOE_CHUNK:
                gate_up[step] = jnp.dot(x, wgu_ref[step], preferred_element_type=F32)
            if step > 0:
                k = step - 1
                gate, up = gate_up[k][:, :EXPERT_FF], gate_up[k][:, EXPERT_FF:]
                weight = jnp.sum(jnp.where(lane == c * MOE_CHUNK + k, cw, 0.0), axis=-1, keepdims=True)
                act = jax.nn.silu(gate) * up * weight
                acc = acc + jnp.dot(act.astype(BF16), wd_ref[k], preferred_element_type=F32)
        ys_ref[sl, :] = acc.astype(BF16)
        return carry

    lax.fori_loop(0, n_sub, sub_tile, 0)

    @pl.when(c == pl.num_programs(1) - 1)
    def _():
        o_ref[...] = x1_ref[...] + jnp.dot(pt_ref[...], ys_ref[pl.ds(0, rows), :], preferred_element_type=F32)


def _moe(hn2, comb2, x1_2, w_gu, w_down, tt):
    t, d = hn2.shape
    rows = -(-(tt + N_GROUPS * MOE_PAD) // LANES) * LANES
    spread = 3.5 * (tt * (N_GROUPS - 1)) ** 0.5 / N_GROUPS
    sub = min(rows, -(-int(tt / N_GROUPS + spread) // MOE_PAD) * MOE_PAD)
    return pl.pallas_call(
        functools.partial(_moe_kernel, tt=tt, rows=rows, sub=sub),
        grid=(t // tt, N_EXPERTS // MOE_CHUNK),
        in_specs=[
            pl.BlockSpec((tt, d), lambda i, c: (i, 0)),
            pl.BlockSpec((tt, LANES), lambda i, c: (i, 0)),
            pl.BlockSpec((tt, d), lambda i, c: (i, 0)),
            pl.BlockSpec((MOE_CHUNK, d, 2 * EXPERT_FF), lambda i, c: (c, 0, 0)),
            pl.BlockSpec((MOE_CHUNK, EXPERT_FF, d), lambda i, c: (c, 0, 0)),
        ],
        out_specs=pl.BlockSpec((tt, d), lambda i, c: (i, 0)),
        out_shape=jax.ShapeDtypeStruct((t, d), F32),
        scratch_shapes=[pltpu.VMEM((rows + sub, d), BF16), pltpu.VMEM((rows + sub, LANES), F32),
                        pltpu.VMEM((rows + sub, d), BF16), pltpu.VMEM((tt, rows), BF16),
                        pltpu.SMEM((2 * N_GROUPS,), jnp.int32)],
        compiler_params=_params(("parallel", "arbitrary"), MOE_VMEM_LIMIT),
        name="moe",
    )(hn2, comb2, x1_2, w_gu, w_down)


def _block_diag(w):
    n, bd, _ = w.shape
    eye = jnp.eye(n, dtype=w.dtype)
    return jnp.einsum("nij,nm->nimj", w, eye).reshape(n * bd, n * bd)


def _tile(n, pref):
    while n % pref:
        pref //= 2
    return pref


def _layer(x, mem, g_mix, w_in, g_q_sb, g_k_sb, conv_w, conv_b, lru_w_a, lru_b_a, lru_w_i, lru_b_i,
           lru_lambda, g_mem, w_mem_kv, g_q_x, g_k_x, w_branch, w_out, g_ffn, w_group, b_group,
           w_expert, b_expert, w_gate, w_up, w_down):
    b, s, d = x.shape
    t = b * s
    row = lambda v: v.reshape(1, -1).astype(F32)

    ones = jnp.ones((PROJ_TILE,), F32)
    col_gain = jnp.concatenate([
        jnp.tile(g_q_sb, SB_WIDTH // SB_HEAD_DIM) * (SB_HEAD_DIM ** -0.5 * LOG2E),
        jnp.tile(g_k_sb, SB_WIDTH // SB_HEAD_DIM),
        ones, ones, ones,
        jnp.tile(g_q_x, X_HEADS) * X_HEAD_DIM ** -0.5,
    ] + [ones] * 6).reshape(1, -1)
    w_router = jnp.zeros((d, LANES), F32).at[:, :N_EXPERTS].set(w_expert)
    w_router = w_router.at[:, N_EXPERTS:N_EXPERTS + N_GROUPS].set(w_group)
    b_router = jnp.zeros((1, LANES), F32).at[0, :N_EXPERTS].set(b_expert)
    b_router = b_router.at[0, N_EXPERTS:N_EXPERTS + N_GROUPS].set(b_group)
    w_gu = jnp.concatenate([w_gate, w_up], axis=-1).reshape(N_EXPERTS, d, 2 * EXPERT_FF).astype(BF16)
    w_dn = w_down.reshape(N_EXPERTS, EXPERT_FF, d).astype(BF16)

    proj, o_lru = _in_proj(x.reshape(t, d), row(g_mix), w_in.astype(BF16), col_gain, conv_w, row(conv_b),
                           _block_diag(lru_w_a).astype(BF16), row(lru_b_a), _block_diag(lru_w_i).astype(BF16),
                           row(lru_b_i), row(lru_lambda), _tile(s, 512), s)
    proj3 = proj.reshape(b, s, -1)
    o_lru = o_lru.reshape(b, s, LRU_WIDTH)
    q_gain = jnp.abs(col_gain[0, :PROJ_TILE])
    z_max = 1.05 * SB_HEAD_DIM * jnp.max(q_gain) * jnp.max(jnp.abs(g_k_sb))
    carry_done = (z_max + BF16_ZERO_EXP).reshape(1, 1).astype(F32)
    o_sb = _sb_attention(proj3, carry_done, _tile(s, 256))
    kx, vx = _mem_kv(mem, row(g_mem), w_mem_kv.astype(BF16), row(jnp.tile(g_k_x, X_HEADS)))
    x1, hn, comb = _merge(proj3, o_sb, o_lru, kx, vx, x, w_branch.astype(BF16), w_out.astype(BF16),
                          row(g_ffn), w_router, b_router, _tile(s, 1024))
    out = _moe(hn.reshape(t, d), comb.reshape(t, LANES), x1.reshape(t, d), w_gu, w_dn, _tile(t, 1024))
    return out.reshape(b, s, d)


def kernel(x, mem, g_mix, w_in, g_q_sb, g_k_sb, conv_w, conv_b, lru_w_a, lru_b_a, lru_w_i, lru_b_i, lru_lambda, g_mem, w_mem_kv, g_q_x, g_k_x, w_branch, w_out, g_ffn, w_group, b_group, w_expert, b_expert, w_gate, w_up, w_down):
    params = (g_mix, w_in, g_q_sb, g_k_sb, conv_w, conv_b, lru_w_a, lru_b_a, lru_w_i, lru_b_i, lru_lambda,
              g_mem, w_mem_kv, g_q_x, g_k_x, w_branch, w_out, g_ffn, w_group, b_group, w_expert, b_expert,
              w_gate, w_up, w_down)
    for layer in range(g_mix.shape[0]):
        x = _layer(x, mem, *[p[layer] for p in params])
    return x
```
